```python
import jax, jax.numpy as jnp
from jax import lax
import numpy as np

D_MODEL = 4096
BATCH = 4
SEQ = 2048
DEPTH = 2
DEC_BATCH = 8
DEC_SEQ = 4
PAST_LEN = 16384
PAGE_SIZE = 128

HEAD_DIM = 128
D_MIX = D_MODEL
D_A = D_MIX // 2
GA_DIM = HEAD_DIM
G_A = D_A // GA_DIM
CHUNK = 128
D_B = D_MIX - D_A
N_HEADS = D_B // HEAD_DIM
N_KV = 4
GQA = N_HEADS // N_KV
D_KV = N_KV * HEAD_DIM
STRIDE = 16
L_CMP = 2 * STRIDE
L_SLC = 64
N_SEL = 16
WINDOW = 512
ROT_DIM = HEAD_DIM // 4
ROPE_THETA = 500000.0
D_FF = ((8 * D_MODEL // 3 + 255) // 256) * 256
CONV_W = 3
D_IN = 2 * D_A + D_B + 6 * D_KV + 3 * N_HEADS
ALPHA = (2 * DEPTH) ** 0.25
BETA = (8 * DEPTH) ** -0.25
LN_EPS = 1e-5
SCALE = HEAD_DIM ** -0.5
SEL_FORCE = 1e9
SLC_QBLK = 64
WIN_QBLK = 128

kernel_name = 'hymba_gmlp_nsa_convffn_step'


def layer_norm(x, g, b):
    xf = x.astype(jnp.float32)
    mu = xf.mean(-1, keepdims=True)
    var = jnp.square(xf - mu).mean(-1, keepdims=True)
    return ((xf - mu) * lax.rsqrt(var + LN_EPS) * g + b).astype(x.dtype)


def rotary(x, pos):
    half = ROT_DIM // 2
    inv = ROPE_THETA ** (-jnp.arange(half, dtype=jnp.float32) * 2.0 / ROT_DIM)
    ang = pos.astype(jnp.float32)[:, None] * inv[None, :]
    cos = jnp.cos(ang)[None, :, None, :]
    sin = jnp.sin(ang)[None, :, None, :]
    xr = x[..., :ROT_DIM].astype(jnp.float32)
    x1, x2 = xr[..., :half], xr[..., half:]
    rot = jnp.concatenate([x1 * cos - x2 * sin, x2 * cos + x1 * sin], axis=-1).astype(x.dtype)
    return jnp.concatenate([rot, x[..., ROT_DIM:]], axis=-1)


def masked_softmax(s, mask):
    s = jnp.where(mask, s.astype(jnp.float32), -jnp.inf)
    m = jnp.max(s, axis=-1, keepdims=True)
    m = jnp.where(jnp.isfinite(m), m, 0.0)
    p = jnp.exp(s - m)
    return p / jnp.maximum(p.sum(-1, keepdims=True), 1e-30)


def project(x, w, pos):
    B, T, _ = x.shape
    h = jnp.einsum('btd,de->bte', x, w)
    base = 2 * D_A + D_B
    cuts = [D_A, 2 * D_A, base] + [base + i * D_KV for i in range(1, 7)]
    u, v, q, kc, vc, ks, vs, kw, vw, g = jnp.split(h, cuts, axis=-1)
    q = rotary(q.reshape(B, T, N_HEADS, HEAD_DIM), pos)
    kc, ks, kw = [rotary(t.reshape(B, T, N_KV, HEAD_DIM), pos) for t in (kc, ks, kw)]
    vc, vs, vw = [t.reshape(B, T, N_KV, HEAD_DIM) for t in (vc, vs, vw)]
    g = jax.nn.sigmoid(g.astype(jnp.float32)).astype(x.dtype).reshape(B, T, 3, N_HEADS, 1)
    return jax.nn.gelu(u), jax.nn.gelu(v), q, kc, vc, ks, vs, kw, vw, g


def sgu(u, v, w_s, b_s, g):
    B, T, _ = v.shape
    vf = v.reshape(B, T, G_A, GA_DIM).astype(jnp.float32)
    mu = vf.mean(-1, keepdims=True)
    var = jnp.square(vf - mu).mean(-1, keepdims=True)
    vn = ((vf - mu) * lax.rsqrt(var + LN_EPS) * g.reshape(G_A, GA_DIM)).astype(v.dtype)
    n = -(-T // CHUNK)
    vc = jnp.pad(vn, ((0, 0), (0, n * CHUNK - T), (0, 0), (0, 0))).reshape(B, n, CHUNK, G_A, GA_DIM)
    w = jnp.where(jnp.tril(jnp.ones((CHUNK, CHUNK), dtype=bool)), w_s, 0)
    mixed = jnp.einsum('gts,bnsgc->bntgc', w, vc) + b_s.T[:, :, None]
    mixed = mixed.reshape(B, n * CHUNK, D_A)[:, :T]
    return u * mixed, vn.reshape(B, T, D_A)


def compress(kv, pe, w1, w2):
    B, T = kv.shape[:2]
    n = T // STRIDE
    ch = kv[:, :n * STRIDE].reshape(B, n, STRIDE, N_KV, HEAD_DIM)
    w1 = w1.reshape(2, STRIDE, HEAD_DIM, HEAD_DIM)
    pe = pe.reshape(2, STRIDE, HEAD_DIM)
    ha = jnp.einsum('bnshd,sde->bnhe', ch[:, :-1], w1[0])
    hb = jnp.einsum('bnshd,sde->bnhe', ch[:, 1:], w1[1])
    hpe = jnp.einsum('ksd,ksde->e', pe, w1)
    return jnp.einsum('bnhe,ef->bnhf', jax.nn.gelu(ha + hb + hpe), w2)


def cmp_branch(q, kc, vc, q_pos, pe, w1, w2):
    B, Tq = q.shape[:2]
    k_c = compress(kc, pe[0], w1[0], w2[0])
    v_c = compress(vc, pe[1], w1[1], w2[1])
    nc = k_c.shape[1]
    blk_end = jnp.arange(nc) * STRIDE + (L_CMP - 1)
    mask = blk_end[None, :] <= q_pos[:, None]
    qg = q.reshape(B, Tq, N_KV, GQA, HEAD_DIM)
    s = jnp.einsum('bqhgd,bnhd->bhgqn', qg, k_c).astype(jnp.float32) * SCALE
    p = masked_softmax(s, mask)
    o = jnp.einsum('bhgqn,bnhd->bqhgd', p.astype(v_c.dtype), v_c)
    return o.reshape(B, Tq, N_HEADS, HEAD_DIM), p


def overlap_matrix(nc, nb):
    i = np.arange(nc)[:, None]
    j = np.arange(nb)[None, :]
    lo = np.maximum(i * STRIDE, j * L_SLC)
    hi = np.minimum(i * STRIDE + L_CMP, (j + 1) * L_SLC)
    return (np.maximum(hi - lo, 0) / STRIDE).astype(np.float32)


def slc_attend(qb, idxb, posb, ks_b, vs_b):
    B, Tq = qb.shape[:2]
    bi = jnp.arange(B)[:, None, None, None]
    hi = jnp.arange(N_KV)[None, :, None, None]
    kg = ks_b[bi, hi, idxb]
    vg = vs_b[bi, hi, idxb]
    kpos = idxb[..., None] * L_SLC + jnp.arange(L_SLC)
    mask = kpos <= posb[None, None, :, None, None]
    qg = qb.reshape(B, Tq, N_KV, GQA, HEAD_DIM)
    s = jnp.einsum('bqhgd,bhqkld->bhgqkl', qg, kg).astype(jnp.float32) * SCALE
    kk = idxb.shape[-1]
    p = masked_softmax(s.reshape(B, N_KV, GQA, Tq, kk * L_SLC),
                       mask.reshape(B, N_KV, 1, Tq, kk * L_SLC)).reshape(s.shape)
    o = jnp.einsum('bhgqkl,bhqkld->bqhgd', p.astype(vg.dtype), vg)
    return o.reshape(B, Tq, N_HEADS, HEAD_DIM)


def slc_branch(q, ks, vs, p_grp, q_pos):
    B, Tq = q.shape[:2]
    Tk = ks.shape[1]
    nb = -(-Tk // L_SLC)
    padw = ((0, 0), (0, nb * L_SLC - Tk), (0, 0), (0, 0))
    ks_b = jnp.pad(ks, padw).reshape(B, nb, L_SLC, N_KV, HEAD_DIM).transpose(0, 3, 1, 2, 4)
    vs_b = jnp.pad(vs, padw).reshape(B, nb, L_SLC, N_KV, HEAD_DIM).transpose(0, 3, 1, 2, 4)
    ov = jnp.asarray(overlap_matrix(p_grp.shape[-1], nb))
    imp = jnp.einsum('bhqn,nj->bhqj', p_grp, ov)
    j = jnp.arange(nb)[None, :]
    cur = (q_pos // L_SLC)[:, None]
    forced = (j == 0) | (j == cur) | (j == cur - 1)
    valid = j * L_SLC <= q_pos[:, None]
    imp = jnp.where(valid, jnp.where(forced, SEL_FORCE, imp), -SEL_FORCE)
    k_sel = min(N_SEL, nb)
    _, idx = lax.top_k(imp, k_sel)
    if Tq > SLC_QBLK and Tq % SLC_QBLK == 0:
        nq = Tq // SLC_QBLK
        qs = q.reshape(B, nq, SLC_QBLK, N_HEADS, HEAD_DIM).swapaxes(0, 1)
        ids = idx.reshape(B, N_KV, nq, SLC_QBLK, k_sel).transpose(2, 0, 1, 3, 4)
        ps = q_pos.reshape(nq, SLC_QBLK)
        o = lax.map(lambda a: slc_attend(a[0], a[1], a[2], ks_b, vs_b), (qs, ids, ps))
        return o.swapaxes(0, 1).reshape(B, Tq, N_HEADS, HEAD_DIM)
    return slc_attend(q, idx, q_pos, ks_b, vs_b)


def win_attend(q, k, v, q_pos, k_pos):
    B, Tq = q.shape[:2]
    d = q_pos[:, None] - k_pos[None, :]
    mask = (d >= 0) & (d < WINDOW) & (k_pos[None, :] >= 0)
    qg = q.reshape(B, Tq, N_KV, GQA, HEAD_DIM)
    s = jnp.einsum('bqhgd,bkhd->bhgqk', qg, k).astype(jnp.float32) * SCALE
    p = masked_softmax(s, mask)
    o = jnp.einsum('bhgqk,bkhd->bqhgd', p.astype(v.dtype), v)
    return o.reshape(B, Tq, N_HEADS, HEAD_DIM)


def win_banded(q, kw, vw, pos):
    B, T = q.shape[:2]
    nq = T // WIN_QBLK
    padw = ((0, 0), (WINDOW, 0), (0, 0), (0, 0))
    idx = jnp.arange(nq)[:, None] * WIN_QBLK + jnp.arange(WINDOW + WIN_QBLK)[None, :]
    kb = jnp.pad(kw, padw)[:, idx]
    vb = jnp.pad(vw, padw)[:, idx]
    kpos = pos[0] + idx - WINDOW
    qb = q.reshape(B, nq, WIN_QBLK, N_HEADS, HEAD_DIM)
    o = jax.vmap(win_attend, in_axes=(1, 1, 1, 0, 0), out_axes=1)(qb, kb, vb, pos.reshape(nq, WIN_QBLK), kpos)
    return o.reshape(B, T, N_HEADS, HEAD_DIM)


def conv_ffn(x, w_in, conv_w, conv_b, w_down, conv_state):
    T = x.shape[1]
    h = jnp.einsum('btd,df->btf', x, w_in)
    gate, up = jnp.split(h, 2, axis=-1)
    hist = jnp.concatenate([conv_state.astype(gate.dtype), gate], axis=1)
    c = conv_b + hist[:, 0:T] * conv_w[0]
    for k in range(1, CONV_W):
        c = c + hist[:, k:k + T] * conv_w[k]
    y = jnp.einsum('btf,fd->btd', jax.nn.gelu(c) * up, w_down)
    return y, hist[:, T:]


def layer_forward(x, pos, lp, past_nsa, past_win, conv_state):
    (w_in_l, sgu_w_l, sgu_b_l, sgu_g_l, pe_l, w1_l, w2_l, w_o_l, ln_g_l, ln_b_l,
     f_in_l, f_cw_l, f_cb_l, f_down_l) = lp
    B, T, _ = x.shape
    u, v, q, kc, vc, ks, vs, kw, vw, g = project(x, w_in_l, pos)
    a_out, v_rows = sgu(u, v, sgu_w_l, sgu_b_l, sgu_g_l)
    new_nsa = jnp.stack([kc, vc, ks, vs], axis=2)
    new_win = jnp.stack([kw, vw], axis=2)
    full = new_nsa if past_nsa is None else jnp.concatenate([past_nsa, new_nsa], axis=1)
    o_c, p_c = cmp_branch(q, full[:, :, 0], full[:, :, 1], pos, pe_l, w1_l, w2_l)
    o_s = slc_branch(q, full[:, :, 2], full[:, :, 3], p_c.sum(2), pos)
    if past_win is None:
        o_w = win_banded(q, kw, vw, pos)
    else:
        buf, buf_pos = past_win
        kv_w = jnp.concatenate([buf, new_win], axis=1)
        o_w = win_attend(q, kv_w[:, :, 0], kv_w[:, :, 1], pos, jnp.concatenate([buf_pos, pos]))
    b_out = (g[:, :, 0] * o_c + g[:, :, 1] * o_s + g[:, :, 2] * o_w).reshape(B, T, D_B)
    mix = jnp.einsum('bte,ed->btd', jnp.concatenate([a_out, b_out], axis=-1), w_o_l)
    x = layer_norm(ALPHA * x + mix, ln_g_l[0], ln_b_l[0])
    f, conv_new = conv_ffn(x, f_in_l, f_cw_l, f_cb_l, f_down_l, conv_state)
    x = layer_norm(ALPHA * x + f, ln_g_l[1], ln_b_l[1])
    return x, new_nsa, new_win, v_rows, conv_new


def setup_inputs(seed: int = 0) -> dict:
    key = jax.random.key(seed)
    k = jax.random.split(key, 24)
    n_pages = PAST_LEN // PAGE_SIZE
    n_used = DEC_BATCH * n_pages
    n_pool = n_used + max(1, n_used // 4)
    w_buf = min(WINDOW, PAST_LEN)

    def nrm(kk, shape, scale):
        return jax.random.normal(kk, shape, jnp.float32) * scale

    page_table = jax.random.permutation(k[5], n_pool)[:n_used].reshape(DEC_BATCH, n_pages).astype(jnp.int32)
    return {
        'x_prompt': nrm(k[0], (BATCH, SEQ, D_MODEL), 1.0),
        'x_sample': nrm(k[1], (DEC_BATCH, DEC_SEQ, D_MODEL), 1.0),
        'cache_nsa_kv': nrm(k[2], (DEPTH, n_pool, PAGE_SIZE, 4, N_KV, HEAD_DIM), 1.0),
        'cache_win_kv': nrm(k[3], (DEPTH, DEC_BATCH, w_buf, 2, N_KV, HEAD_DIM), 1.0),
        'state_ffn_conv': nrm(k[4], (DEPTH, DEC_BATCH, CONV_W - 1, D_FF), 1.0),
        'page_table': page_table,
        'w_in': nrm(k[6], (DEPTH, D_MODEL, D_IN), D_MODEL ** -0.5),
        'sgu_w': nrm(k[7], (DEPTH, G_A, CHUNK, CHUNK), CHUNK ** -0.5),
        'sgu_b': 1.0 + nrm(k[8], (DEPTH, G_A, CHUNK), 0.1),
        'sgu_g': 1.0 + nrm(k[9], (DEPTH, D_A), 0.1),
        'cmp_pe': nrm(k[10], (DEPTH, 2, L_CMP, HEAD_DIM), 0.5),
        'cmp_w1': nrm(k[11], (DEPTH, 2, L_CMP, HEAD_DIM, HEAD_DIM), (L_CMP * HEAD_DIM) ** -0.5),
        'cmp_w2': nrm(k[12], (DEPTH, 2, HEAD_DIM, HEAD_DIM), HEAD_DIM ** -0.5),
        'w_o': nrm(k[13], (DEPTH, D_MIX, D_MODEL), BETA * D_MIX ** -0.5),
        'ln_g': 1.0 + nrm(k[14], (DEPTH, 2, D_MODEL), 0.1),
        'ln_b': nrm(k[15], (DEPTH, 2, D_MODEL), 0.02),
        'ffn_w_in': nrm(k[16], (DEPTH, D_MODEL, 2 * D_FF), D_MODEL ** -0.5),
        'ffn_conv_w': nrm(k[17], (DEPTH, CONV_W, D_FF), CONV_W ** -0.5),
        'ffn_conv_b': nrm(k[18], (DEPTH, D_FF), 0.02),
        'ffn_w_down': nrm(k[19], (DEPTH, D_FF, D_MODEL), BETA * D_FF ** -0.5),
    }


def reference(x_prompt, x_sample, cache_nsa_kv, cache_win_kv, state_ffn_conv, page_table,
              w_in, sgu_w, sgu_b, sgu_g, cmp_pe, cmp_w1, cmp_w2, w_o, ln_g, ln_b,
              ffn_w_in, ffn_conv_w, ffn_conv_b, ffn_w_down):
    Bp, T = x_prompt.shape[:2]
    Bs, Ts = x_sample.shape[:2]
    past = page_table.shape[1] * cache_nsa_kv.shape[2]
    w_buf = cache_win_kv.shape[2]
    pos_p = jnp.arange(T, dtype=jnp.int32)
    pos_s = past + jnp.arange(Ts, dtype=jnp.int32)
    pos_buf = past - w_buf + jnp.arange(w_buf, dtype=jnp.int32)
    xp, xs = x_prompt, x_sample
    p_nsa, s_nsa, p_win, s_win, s_v, p_conv, s_conv = [], [], [], [], [], [], []
    for l in range(DEPTH):
        lp = (w_in[l], sgu_w[l], sgu_b[l], sgu_g[l], cmp_pe[l], cmp_w1[l], cmp_w2[l], w_o[l],
              ln_g[l], ln_b[l], ffn_w_in[l], ffn_conv_w[l], ffn_conv_b[l], ffn_w_down[l])
        xp, nsa_p, win_p, _, conv_p = layer_forward(
            xp, pos_p, lp, None, None, jnp.zeros((Bp, CONV_W - 1, D_FF), xp.dtype))
        past_rows = cache_nsa_kv[l, page_table].reshape(Bs, past, 4, N_KV, HEAD_DIM)
        xs, nsa_s, win_s, v_s, conv_s = layer_forward(
            xs, pos_s, lp, past_rows, (cache_win_kv[l], pos_buf), state_ffn_conv[l])
        p_nsa.append(nsa_p)
        s_nsa.append(nsa_s)
        p_win.append(win_p[:, -min(WINDOW, T):])
        s_win.append(win_s)
        s_v.append(v_s)
        p_conv.append(conv_p)
        s_conv.append(conv_s)
    return (xp, xs, jnp.stack(p_nsa), jnp.stack(s_nsa), jnp.stack(p_win), jnp.stack(s_win),
            jnp.stack(s_v), jnp.stack(p_conv), jnp.stack(s_conv))
```

```python
import functools
import math

import jax
import jax.numpy as jnp
import numpy as np
from jax import lax
from jax.experimental import pallas as pl
from jax.experimental.pallas import tpu as pltpu

HEAD_DIM = 128
CHUNK = 128
STRIDE = 16
L_CMP = 2 * STRIDE
L_SLC = 64
N_SEL = 16
WINDOW = 512
ROT_DIM = HEAD_DIM // 4
ROT_HALF = ROT_DIM // 2
ROPE_THETA = 500000.0
CONV_W = 3
LN_EPS = 1e-5
SCALE = HEAD_DIM ** -0.5
SEL_FORCE = 1e9
NEG_BIG = -3.0e38
VMEM_LIMIT = 56 * 1024 * 1024

F32 = jnp.float32
BF16 = jnp.bfloat16


def _cparams(*sem):
    return pltpu.CompilerParams(dimension_semantics=sem, vmem_limit_bytes=VMEM_LIMIT)


def _tile(n, pref, unit=128):
    if n <= pref:
        return n
    t = (pref // unit) * unit
    while t > unit and n % t:
        t -= unit
    assert n % t == 0, (n, pref, unit)
    return t


def _gelu(x):
    return jax.nn.gelu(x, approximate=True)


def _dot(a, b):
    return jnp.dot(a, b, preferred_element_type=F32)


def _dot_nt(a, b):
    return lax.dot_general(a, b, (((1,), (1,)), ((), ())), preferred_element_type=F32)


def _split_hi_lo(x):
    hi = x.astype(BF16)
    lo = (x - hi.astype(F32)).astype(BF16)
    return hi, lo


def _masked_softmax(s, mask):
    sm = jnp.where(mask, s, NEG_BIG)
    m = jnp.max(sm, axis=-1, keepdims=True)
    p = jnp.where(mask, jnp.exp(sm - m), 0.0)
    return p / jnp.maximum(jnp.sum(p, axis=-1, keepdims=True), 1e-30)


def _mm_kernel(*refs, n_lhs, epilogue):
    acc = _dot(refs[0][...], refs[n_lhs][...])
    for k in range(1, n_lhs):
        acc = acc + _dot(refs[k][...], refs[n_lhs + k][...])
    o_ref = refs[2 * n_lhs]
    if epilogue == "gelu":
        acc = _gelu(acc)
    elif epilogue == "sigmoid":
        acc = jax.nn.sigmoid(acc)
    o_ref[...] = acc.astype(o_ref.dtype)


def _matmul(xs, ws, *, epilogue="none", out_dtype=F32, tm=512, tn=512, name):
    m = xs[0].shape[0]
    n = ws[0].shape[1]
    tm = _tile(m, tm, 8)
    tn = _tile(n, tn)
    n_lhs = len(xs)
    in_specs = [pl.BlockSpec((tm, x.shape[1]), lambda i, j: (i, 0)) for x in xs]
    in_specs += [pl.BlockSpec((w.shape[0], tn), lambda i, j: (0, j)) for w in ws]
    return pl.pallas_call(
        functools.partial(_mm_kernel, n_lhs=n_lhs, epilogue=epilogue),
        out_shape=jax.ShapeDtypeStruct((m, n), out_dtype),
        grid=(m // tm, n // tn),
        in_specs=in_specs,
        out_specs=pl.BlockSpec((tm, tn), lambda i, j: (i, j)),
        compiler_params=_cparams("parallel", "arbitrary"),
        name=name,
    )(*xs, *ws)


def _rope_mm_kernel(x_ref, w_ref, cos_ref, sin_ref, o_ref, *, n_q_tiles, heads_per_tile):
    j = pl.program_id(1)
    acc = _dot(x_ref[...], w_ref[...])
    rot = jnp.logical_or(j < n_q_tiles, lax.rem(j - n_q_tiles, 2) == 0)
    cosv = jnp.where(rot, cos_ref[...], 1.0)
    sinv = jnp.where(rot, sin_ref[...], 0.0)
    lane = lax.broadcasted_iota(jnp.int32, cosv.shape, 1)
    for h in range(heads_per_tile):
        hs = acc[:, h * HEAD_DIM:(h + 1) * HEAD_DIM]
        partner = jnp.where(lane < ROT_HALF,
                            pltpu.roll(hs, HEAD_DIM - ROT_HALF, 1),
                            pltpu.roll(hs, ROT_HALF, 1))
        o_ref[:, h * HEAD_DIM:(h + 1) * HEAD_DIM] = hs * cosv + partner * sinv


def _rope_matmul(x, w, cos_t, sin_t, *, d_b, d_kv, tm=512, name):
    m, k = x.shape
    n = w.shape[1]
    tm = _tile(m, tm, 8)
    tn = d_kv
    assert d_b % tn == 0
    kern = functools.partial(_rope_mm_kernel, n_q_tiles=d_b // tn, heads_per_tile=tn // HEAD_DIM)
    return pl.pallas_call(
        kern,
        out_shape=jax.ShapeDtypeStruct((m, n), F32),
        grid=(m // tm, n // tn),
        in_specs=[pl.BlockSpec((tm, k), lambda i, j: (i, 0)),
                  pl.BlockSpec((k, tn), lambda i, j: (0, j)),
                  pl.BlockSpec((tm, HEAD_DIM), lambda i, j: (i, 0)),
                  pl.BlockSpec((tm, HEAD_DIM), lambda i, j: (i, 0))],
        out_specs=pl.BlockSpec((tm, tn), lambda i, j: (i, j)),
        compiler_params=_cparams("parallel", "arbitrary"),
        name=name,
    )(x, w, cos_t, sin_t)


def _add_ln_kernel(x_ref, r_ref, g_ref, b_ref, y_ref, yb_ref, *, alpha):
    z = alpha * x_ref[...] + r_ref[...]
    mu = jnp.mean(z, axis=-1, keepdims=True)
    zc = z - mu
    var = jnp.mean(zc * zc, axis=-1, keepdims=True)
    y = zc * lax.rsqrt(var + LN_EPS) * g_ref[...] + b_ref[...]
    y_ref[...] = y
    yb_ref[...] = y.astype(BF16)


def _add_ln(x, r, g, b, *, alpha, name):
    m, d = x.shape
    tr = _tile(m, 256, 8)
    return pl.pallas_call(
        functools.partial(_add_ln_kernel, alpha=alpha),
        out_shape=(jax.ShapeDtypeStruct((m, d), F32), jax.ShapeDtypeStruct((m, d), BF16)),
        grid=(m // tr,),
        in_specs=[pl.BlockSpec((tr, d), lambda i: (i, 0)),
                  pl.BlockSpec((tr, d), lambda i: (i, 0)),
                  pl.BlockSpec((1, d), lambda i: (0, 0)),
                  pl.BlockSpec((1, d), lambda i: (0, 0))],
        out_specs=(pl.BlockSpec((tr, d), lambda i: (i, 0)),
                   pl.BlockSpec((tr, d), lambda i: (i, 0))),
        compiler_params=_cparams("parallel"),
        name=name,
    )(x, r, g.reshape(1, d), b.reshape(1, d))


def _sgu_kernel(uv_ref, w_ref, bt_ref, g_ref, a_ref, vn_ref, *, d_a, n_groups):
    rows = w_ref.shape[1]
    r = lax.broadcasted_iota(jnp.int32, (rows, rows), 0)
    c = lax.broadcasted_iota(jnp.int32, (rows, rows), 1)
    causal = r >= c
    for g in range(n_groups):
        lo = g * HEAD_DIM
        v = uv_ref[:, d_a + lo:d_a + lo + HEAD_DIM]
        mu = jnp.mean(v, axis=-1, keepdims=True)
        vc = v - mu
        var = jnp.mean(vc * vc, axis=-1, keepdims=True)
        vn = vc * lax.rsqrt(var + LN_EPS) * g_ref[:, lo:lo + HEAD_DIM]
        vn_ref[:, lo:lo + HEAD_DIM] = vn
        w = jnp.where(causal, w_ref[g], 0.0).astype(BF16)
        mixed = _dot(w, vn.astype(BF16)) + bt_ref[:, g:g + 1]
        a_ref[:, lo:lo + HEAD_DIM] = (uv_ref[:, lo:lo + HEAD_DIM] * mixed).astype(a_ref.dtype)


def _sgu(uv, w, bt, gain, *, rows, name):
    m = uv.shape[0]
    d_a = uv.shape[1] // 2
    n_groups = d_a // HEAD_DIM
    return pl.pallas_call(
        functools.partial(_sgu_kernel, d_a=d_a, n_groups=n_groups),
        out_shape=(jax.ShapeDtypeStruct((m, d_a), BF16), jax.ShapeDtypeStruct((m, d_a), F32)),
        grid=(m // rows,),
        in_specs=[pl.BlockSpec((rows, 2 * d_a), lambda i: (i, 0)),
                  pl.BlockSpec((n_groups, rows, rows), lambda i: (0, 0, 0)),
                  pl.BlockSpec((rows, n_groups), lambda i: (0, 0)),
                  pl.BlockSpec((1, d_a), lambda i: (0, 0))],
        out_specs=(pl.BlockSpec((rows, d_a), lambda i: (i, 0)),
                   pl.BlockSpec((rows, d_a), lambda i: (i, 0))),
        compiler_params=_cparams("parallel"),
        name=name,
    )(uv, w, bt, gain.reshape(1, d_a))


def _conv_act(gate, g1, g2, up, cw_ref, cb_ref):
    c = cb_ref[...] + g2 * cw_ref[0:1, :] + g1 * cw_ref[1:2, :] + gate * cw_ref[2:3, :]
    return _gelu(c) * up


def _ffn_in_seq_kernel(x_ref, wg_ref, wu_ref, cw_ref, cb_ref, st_ref, a_ref, cn_ref, carry_ref,
                       *, tiles_per_seq):
    i = pl.program_id(1)
    x = x_ref[...]
    gate = _dot(x, wg_ref[...])
    up = _dot(x, wu_ref[...])
    tm = gate.shape[0]

    @pl.when(lax.rem(i, tiles_per_seq) == 0)
    def _():
        carry_ref[0:2, :] = st_ref[...]

    prev2 = carry_ref[0:1, :]
    prev1 = carry_ref[1:2, :]
    row = lax.broadcasted_iota(jnp.int32, gate.shape, 0)
    g1 = jnp.where(row == 0, prev1, pltpu.roll(gate, 1, 0))
    g2 = jnp.where(row == 0, prev2, jnp.where(row == 1, prev1, pltpu.roll(gate, 2, 0)))
    a_ref[...] = _conv_act(gate, g1, g2, up, cw_ref, cb_ref).astype(a_ref.dtype)
    tail = gate[tm - 2:tm, :]
    carry_ref[0:2, :] = tail
    cn_ref[...] = tail


def _ffn_in_seq(x, wg, wu, cw, cb, state, *, seq_len, tm=512, tn=256, name):
    m, k = x.shape
    d_ff = wg.shape[1]
    tm = _tile(seq_len, tm, 8)
    tn = _tile(d_ff, tn)
    tps = seq_len // tm
    n_seq = m // seq_len
    return pl.pallas_call(
        functools.partial(_ffn_in_seq_kernel, tiles_per_seq=tps),
        out_shape=(jax.ShapeDtypeStruct((m, d_ff), BF16),
                   jax.ShapeDtypeStruct((n_seq, CONV_W - 1, d_ff), F32)),
        grid=(d_ff // tn, m // tm),
        in_specs=[pl.BlockSpec((tm, k), lambda j, i: (i, 0)),
                  pl.BlockSpec((k, tn), lambda j, i: (0, j)),
                  pl.BlockSpec((k, tn), lambda j, i: (0, j)),
                  pl.BlockSpec((CONV_W, tn), lambda j, i: (0, j)),
                  pl.BlockSpec((1, tn), lambda j, i: (0, j)),
                  pl.BlockSpec((None, CONV_W - 1, tn), lambda j, i: (i // tps, 0, j))],
        out_specs=(pl.BlockSpec((tm, tn), lambda j, i: (i, j)),
                   pl.BlockSpec((None, CONV_W - 1, tn), lambda j, i: (i // tps, 0, j))),
        scratch_shapes=[pltpu.VMEM((8, tn), F32)],
        compiler_params=_cparams("arbitrary", "arbitrary"),
        name=name,
    )(x, wg, wu, cw, cb.reshape(1, d_ff), state)


def _ffn_in_short_kernel(x_ref, wg_ref, wu_ref, cw_ref, cb_ref, h1_ref, h2_ref, a_ref, gate_ref,
                         *, seq_len):
    x = x_ref[...]
    gate = _dot(x, wg_ref[...])
    up = _dot(x, wu_ref[...])
    t = lax.rem(lax.broadcasted_iota(jnp.int32, gate.shape, 0), seq_len)
    g1 = jnp.where(t >= 1, pltpu.roll(gate, 1, 0), h1_ref[...])
    g2 = jnp.where(t >= 2, pltpu.roll(gate, 2, 0), h2_ref[...])
    a_ref[...] = _conv_act(gate, g1, g2, up, cw_ref, cb_ref).astype(a_ref.dtype)
    gate_ref[...] = gate


def _ffn_in_short(x, wg, wu, cw, cb, h1, h2, *, seq_len, tn=256, name):
    m, k = x.shape
    d_ff = wg.shape[1]
    tn = _tile(d_ff, tn)
    return pl.pallas_call(
        functools.partial(_ffn_in_short_kernel, seq_len=seq_len),
        out_shape=(jax.ShapeDtypeStruct((m, d_ff), BF16), jax.ShapeDtypeStruct((m, d_ff), F32)),
        grid=(d_ff // tn,),
        in_specs=[pl.BlockSpec((m, k), lambda j: (0, 0)),
                  pl.BlockSpec((k, tn), lambda j: (0, j)),
                  pl.BlockSpec((k, tn), lambda j: (0, j)),
                  pl.BlockSpec((CONV_W, tn), lambda j: (0, j)),
                  pl.BlockSpec((1, tn), lambda j: (0, j)),
                  pl.BlockSpec((m, tn), lambda j: (0, j)),
                  pl.BlockSpec((m, tn), lambda j: (0, j))],
        out_specs=(pl.BlockSpec((m, tn), lambda j: (0, j)),
                   pl.BlockSpec((m, tn), lambda j: (0, j))),
        compiler_params=_cparams("parallel"),
        name=name,
    )(x, wg, wu, cw, cb.reshape(1, d_ff), h1, h2)


def _compress_rows(xa, w1a, w1b, w2, hpe):
    a = _dot(xa, w1a)
    b = _dot(xa, w1b)
    n = a.shape[0]
    h = a + pltpu.roll(b, n - 1, 0) + hpe
    return _dot(_gelu(h).astype(BF16), w2)


def _pos_embed_term(pe_ref, w1_ref, kind):
    return _dot(pe_ref[kind], w1_ref[kind])[0:1, :]


def _compress_p_kernel(k_ref, v_ref, w1_ref, w2_ref, pe_ref, kc_ref, vc_ref, *, n_chunks):
    half = STRIDE * HEAD_DIM
    for kind, (src, dst) in enumerate(((k_ref, kc_ref), (v_ref, vc_ref))):
        xa = jnp.concatenate(
            [src[pl.ds(s, n_chunks, stride=STRIDE), :] for s in range(STRIDE)], axis=1).astype(BF16)
        hpe = _pos_embed_term(pe_ref, w1_ref, kind)
        out = _compress_rows(xa, w1_ref[kind, 0:half, :], w1_ref[kind, half:2 * half, :],
                             w2_ref[kind], hpe)
        dst[...] = out.astype(dst.dtype)


def _compress_prompt(qkv, w1, w2, pe, *, batch, seq, n_heads, n_kv, name):
    n_chunks = seq // STRIDE
    kcol = n_heads
    vcol = n_heads + n_kv
    out = jax.ShapeDtypeStruct((batch, n_kv, n_chunks, HEAD_DIM), BF16)
    ospec = pl.BlockSpec((None, None, n_chunks, HEAD_DIM), lambda b, h: (b, h, 0, 0))
    return pl.pallas_call(
        functools.partial(_compress_p_kernel, n_chunks=n_chunks),
        out_shape=(out, out),
        grid=(batch, n_kv),
        in_specs=[pl.BlockSpec((seq, HEAD_DIM), lambda b, h: (b, kcol + h)),
                  pl.BlockSpec((seq, HEAD_DIM), lambda b, h: (b, vcol + h)),
                  pl.BlockSpec(w1.shape, lambda b, h: (0, 0, 0)),
                  pl.BlockSpec(w2.shape, lambda b, h: (0, 0, 0)),
                  pl.BlockSpec(pe.shape, lambda b, h: (0, 0, 0))],
        out_specs=(ospec, ospec),
        compiler_params=_cparams("parallel", "parallel"),
        name=name,
    )(qkv, qkv, w1, w2, pe)


def _block_scores(imp, pos, n_blocks):
    j = lax.broadcasted_iota(jnp.int32, imp.shape, 1)
    cur = pos // L_SLC
    forced = (j == 0) | (j == cur) | (j == cur - 1)
    valid = j * L_SLC <= pos
    score = jnp.where(valid, jnp.where(forced, SEL_FORCE, imp), -SEL_FORCE)
    return jnp.where(j < n_blocks, score, NEG_BIG)


def _attn_p_kernel(q_ref, kc_ref, vc_ref, ks_ref, vs_ref, kw_ref, vw_ref, gt_ref, ov_ref, ex_ref,
                   o_ref, *, tq, seq, gqa, n_blocks, n_cmp, win_keys):
    qi = pl.program_id(2)
    t0 = qi * tq
    q = q_ref[...]
    qs = jnp.concatenate([q[:, g * HEAD_DIM:(g + 1) * HEAD_DIM] for g in range(gqa)],
                         axis=0).astype(BF16)
    rows = gqa * tq
    pos_t = t0 + lax.broadcasted_iota(jnp.int32, (tq, 1), 0)
    pos = jnp.concatenate([pos_t] * gqa, axis=0)

    kc = kc_ref[...]
    s = _dot_nt(qs, kc) * SCALE
    n = lax.broadcasted_iota(jnp.int32, s.shape, 1)
    p = _masked_softmax(s, (n * STRIDE + (L_CMP - 1) <= pos) & (n < n_cmp))
    o_c = _dot(p.astype(BF16), vc_ref[...])
    p_grp = p[0:tq]
    for g in range(1, gqa):
        p_grp = p_grp + p[g * tq:(g + 1) * tq]

    hi, lo = _split_hi_lo(p_grp)
    imp = _dot(hi, ov_ref[...]) + _dot(lo, ov_ref[...])
    score = _block_scores(imp, pos_t, n_blocks)
    lane = lax.broadcasted_iota(jnp.int32, score.shape, 1)
    rank = jnp.zeros(score.shape, F32)
    for i in range(n_blocks):
        ci = score[:, i:i + 1]
        beats = (ci > score) | ((ci == score) & (lane > i))
        rank = rank + jnp.where(beats, 1.0, 0.0)
    sel = jnp.where(rank < float(min(N_SEL, n_blocks)), 1.0, 0.0).astype(BF16)
    sel_keys = _dot(sel, ex_ref[...])
    sel_keys = jnp.concatenate([sel_keys] * gqa, axis=0)

    s = _dot_nt(qs, ks_ref[...].astype(BF16)) * SCALE
    kpos = lax.broadcasted_iota(jnp.int32, s.shape, 1)
    p = _masked_softmax(s, (sel_keys > 0.5) & (kpos <= pos))
    o_s = _dot(p.astype(BF16), vs_ref[...].astype(BF16))

    start = pl.multiple_of(jnp.maximum(t0 + tq - win_keys, 0), 128)
    kw = kw_ref[pl.ds(start, win_keys), :].astype(BF16)
    vw = vw_ref[pl.ds(start, win_keys), :].astype(BF16)
    s = _dot_nt(qs, kw) * SCALE
    d = pos - (start + lax.broadcasted_iota(jnp.int32, s.shape, 1))
    p = _masked_softmax(s, (d >= 0) & (d < WINDOW))
    o_w = _dot(p.astype(BF16), vw)

    gt = gt_ref[...]
    for g in range(gqa):
        r0 = g * tq
        out = (gt[:, g:g + 1] * o_c[r0:r0 + tq]
               + gt[:, gqa + g:gqa + g + 1] * o_s[r0:r0 + tq]
               + gt[:, 2 * gqa + g:2 * gqa + g + 1] * o_w[r0:r0 + tq])
        o_ref[:, g * HEAD_DIM:(g + 1) * HEAD_DIM] = out.astype(o_ref.dtype)


def _overlap_matrix(nc, nb, rows, cols):
    i = np.arange(nc)[:, None]
    j = np.arange(nb)[None, :]
    lo = np.maximum(i * STRIDE, j * L_SLC)
    hi = np.minimum(i * STRIDE + L_CMP, (j + 1) * L_SLC)
    ov = np.zeros((rows, cols), np.float32)
    ov[:nc, :nb] = np.maximum(hi - lo, 0) / STRIDE
    return ov


def _attn_prompt(qkv, kcmp, vcmp, gates_h, *, batch, seq, n_heads, n_kv, tq=128, name):
    gqa = n_heads // n_kv
    tq = _tile(seq, tq, 8)
    nq = seq // tq
    n_chunks = seq // STRIDE
    n_cmp = n_chunks - 1
    n_blocks = -(-seq // L_SLC)
    lanes = -(-n_blocks // 128) * 128
    ov = jnp.asarray(_overlap_matrix(n_cmp, n_blocks, n_chunks, lanes), BF16)
    ex = np.zeros((lanes, seq), np.float32)
    ex[np.arange(seq) // L_SLC, np.arange(seq)] = 1.0
    ex = jnp.asarray(ex, BF16)
    win_keys = min(WINDOW + tq, seq)
    ks_col, vs_col = n_heads + 2 * n_kv, n_heads + 3 * n_kv
    kw_col, vw_col = n_heads + 4 * n_kv, n_heads + 5 * n_kv
    kern = functools.partial(_attn_p_kernel, tq=tq, seq=seq, gqa=gqa, n_blocks=n_blocks,
                             n_cmp=n_cmp, win_keys=win_keys)

    def kv_spec(col):
        return pl.BlockSpec((seq, HEAD_DIM), lambda b, h, i: (b, col + h))

    cmp_spec = pl.BlockSpec((None, None, n_chunks, HEAD_DIM), lambda b, h, i: (b, h, 0, 0))
    return pl.pallas_call(
        kern,
        out_shape=jax.ShapeDtypeStruct((batch * seq, n_heads * HEAD_DIM), BF16),
        grid=(batch, n_kv, nq),
        in_specs=[pl.BlockSpec((tq, gqa * HEAD_DIM), lambda b, h, i: (b * nq + i, h)),
                  cmp_spec, cmp_spec,
                  kv_spec(ks_col), kv_spec(vs_col), kv_spec(kw_col), kv_spec(vw_col),
                  pl.BlockSpec((None, tq, 3 * gqa), lambda b, h, i: (h, b * nq + i, 0)),
                  pl.BlockSpec(ov.shape, lambda b, h, i: (0, 0)),
                  pl.BlockSpec(ex.shape, lambda b, h, i: (0, 0))],
        out_specs=pl.BlockSpec((tq, gqa * HEAD_DIM), lambda b, h, i: (b * nq + i, h)),
        compiler_params=_cparams("parallel", "parallel", "arbitrary"),
        name=name,
    )(qkv, kcmp, vcmp, qkv, qkv, qkv, qkv, gates_h, ov, ex)


def _cmp_s_kernel(pt_ref, page_ref, q_ref, w1_ref, w2_ref, pe_ref, ovt_ref, oc_ref, sel_ref,
                  x_scr, ab_scr, *, n_kv, n_pages, pages_per_group, n_cmp, n_blocks, dec_seq, gqa,
                  past):
    del pt_ref
    p = pl.program_id(1)
    n_kh = 2 * n_kv
    cpp = page_ref.shape[0]
    tok_width = page_ref.shape[1] // STRIDE
    group_chunks = cpp * pages_per_group
    c0 = pl.multiple_of(lax.rem(p, pages_per_group) * cpp, 8)
    for s in range(STRIDE):
        for kh in range(n_kh):
            col = s * tok_width + kh * HEAD_DIM
            x_scr[kh, pl.ds(c0, cpp), s * HEAD_DIM:(s + 1) * HEAD_DIM] = page_ref[:, col:col + HEAD_DIM]

    @pl.when(lax.rem(p, pages_per_group) == pages_per_group - 1)
    def _():
        g0 = pl.multiple_of((p // pages_per_group) * group_chunks, 8)
        for kh in range(n_kh):
            kind = kh // n_kv
            ab_scr[kh, pl.ds(g0, group_chunks), :] = _dot(x_scr[kh].astype(BF16), w1_ref[kind])

    @pl.when(p == n_pages - 1)
    def _():
        n_chunks = ab_scr.shape[1]
        comp = []
        for kh in range(n_kh):
            kind = kh // n_kv
            hpe = _dot(pe_ref[kind], w1_ref[kind])
            hpe = hpe[0:1, 0:HEAD_DIM] + hpe[8:9, HEAD_DIM:2 * HEAD_DIM]
            ab = ab_scr[kh]
            h = ab[:, 0:HEAD_DIM] + pltpu.roll(ab[:, HEAD_DIM:2 * HEAD_DIM], n_chunks - 1, 0) + hpe
            comp.append(_dot(_gelu(h).astype(BF16), w2_ref[kind]).astype(BF16))
        qrows = lax.broadcasted_iota(jnp.int32, (1, HEAD_DIM), 1)
        pos = past + lax.rem(qrows, dec_seq)
        for h in range(n_kv):
            kc, vc = comp[h], comp[n_kv + h]
            st = _dot_nt(kc, q_ref[h].astype(BF16)) * SCALE
            n = lax.broadcasted_iota(jnp.int32, st.shape, 0)
            mask = (n * STRIDE + (L_CMP - 1) <= pos) & (n < n_cmp)
            sm = jnp.where(mask, st, NEG_BIG)
            mx = jnp.max(sm, axis=0, keepdims=True)
            e = jnp.where(mask, jnp.exp(sm - mx), 0.0)
            pt = e / jnp.maximum(jnp.sum(e, axis=0, keepdims=True), 1e-30)
            oc_ref[h] = _dot(pt.T.astype(BF16), vc)
            pg = pt
            for g in range(1, gqa):
                pg = pg + pltpu.roll(pt, HEAD_DIM - g * dec_seq, 1)
            hi, lo = _split_hi_lo(pg)
            imp_t = _dot(ovt_ref[...], hi) + _dot(ovt_ref[...], lo)
            imp = imp_t.T
            tpos = past + lax.broadcasted_iota(jnp.int32, (imp.shape[0], 1), 0)
            score = _block_scores(imp, tpos, n_blocks)
            score_t = score.T
            nb_pad = score.shape[1]
            ii = lax.broadcasted_iota(jnp.int32, (nb_pad, nb_pad), 0)
            jj = lax.broadcasted_iota(jnp.int32, (nb_pad, nb_pad), 1)
            for t in range(dec_seq):
                col = score_t[:, t:t + 1]
                rowv = score[t:t + 1, :]
                beats = (col > rowv) | ((col == rowv) & (ii < jj))
                rank = jnp.sum(jnp.where(beats, 1.0, 0.0), axis=0, keepdims=True)
                sel_ref[h, t:t + 1, :] = jnp.where(rank < float(min(N_SEL, n_blocks)), 1.0, 0.0)
            sel_ref[h, dec_seq:, :] = jnp.zeros((sel_ref.shape[1] - dec_seq, nb_pad), F32)


def _cmp_sample(page_table, cache_l, q_pad, w1cat, w2, pe, *, layer, n_kv, gqa, dec_seq, name):
    batch, n_pages = page_table.shape
    page = cache_l.shape[2]
    past = n_pages * page
    cpp = page // STRIDE
    n_chunks = past // STRIDE
    assert dec_seq < STRIDE and n_chunks % 8 == 0
    n_cmp = (past + dec_seq) // STRIDE - 1
    n_blocks = -(-(past + dec_seq) // L_SLC)
    nb_pad = -(-n_blocks // 128) * 128
    ppg = max(1, min(n_pages, 128 // cpp))
    assert n_pages % ppg == 0
    n_kh = 2 * n_kv
    chunk_rows = cache_l.reshape(cache_l.shape[0], cache_l.shape[1], cpp, STRIDE * cache_l.shape[3])
    ovt = jnp.asarray(_overlap_matrix(n_cmp, n_blocks, n_chunks, nb_pad).T.copy(), BF16)
    kern = functools.partial(_cmp_s_kernel, n_kv=n_kv, n_pages=n_pages, pages_per_group=ppg,
                             n_cmp=n_cmp, n_blocks=n_blocks, dec_seq=dec_seq, gqa=gqa, past=past)
    grid_spec = pltpu.PrefetchScalarGridSpec(
        num_scalar_prefetch=1,
        grid=(batch, n_pages),
        in_specs=[pl.BlockSpec((None, None, cpp, chunk_rows.shape[3]),
                               lambda b, p, pt: (layer, pt[b, p], 0, 0)),
                  pl.BlockSpec((None, n_kv, 128, HEAD_DIM), lambda b, p, pt: (b, 0, 0, 0)),
                  pl.BlockSpec(w1cat.shape, lambda b, p, pt: (0, 0, 0)),
                  pl.BlockSpec(w2.shape, lambda b, p, pt: (0, 0, 0)),
                  pl.BlockSpec(pe.shape, lambda b, p, pt: (0, 0, 0)),
                  pl.BlockSpec(ovt.shape, lambda b, p, pt: (0, 0))],
        out_specs=(pl.BlockSpec((None, n_kv, 128, HEAD_DIM), lambda b, p, pt: (b, 0, 0, 0)),
                   pl.BlockSpec((None, n_kv, 8, nb_pad), lambda b, p, pt: (b, 0, 0, 0))),
        scratch_shapes=[pltpu.VMEM((n_kh, ppg * cpp, STRIDE * HEAD_DIM), F32),
                        pltpu.VMEM((n_kh, n_chunks, 2 * HEAD_DIM), F32)],
    )
    return pl.pallas_call(
        kern,
        out_shape=(jax.ShapeDtypeStruct((batch, n_kv, 128, HEAD_DIM), F32),
                   jax.ShapeDtypeStruct((batch, n_kv, 8, nb_pad), F32)),
        grid_spec=grid_spec,
        compiler_params=_cparams("parallel", "arbitrary"),
        name=name,
    )(page_table, chunk_rows, q_pad, w1cat, w2, pe, ovt)


def _slc_s_kernel(pt_ref, page_ref, q_ref, sel_ref, kn_ref, vn_ref, kb_ref, vb_ref, kwn_ref,
                  vwn_ref, oc_ref, gt_ref, o_ref, qbd_scr, m_scr, l_scr, acc_scr,
                  *, n_kv, n_pages, rows_per_head, dec_seq, past, w_buf):
    del pt_ref
    p = pl.program_id(1)
    page = page_ref.shape[0]
    width = n_kv * HEAD_DIM
    rows = n_kv * rows_per_head

    @pl.when(p == 0)
    def _():
        qbd_scr[...] = jnp.zeros(qbd_scr.shape, qbd_scr.dtype)
        for h in range(n_kv):
            qbd_scr[h * rows_per_head:(h + 1) * rows_per_head, h * HEAD_DIM:(h + 1) * HEAD_DIM] = (
                q_ref[h, 0:rows_per_head, :].astype(qbd_scr.dtype))
        m_scr[...] = jnp.full(m_scr.shape, NEG_BIG, F32)
        l_scr[...] = jnp.zeros(l_scr.shape, F32)
        acc_scr[...] = jnp.zeros(acc_scr.shape, F32)

    qbd = qbd_scr[...].astype(BF16)

    def online_update(s, mask, v):
        sm = jnp.where(mask, s, NEG_BIG)
        m_old = m_scr[...]
        m_new = jnp.maximum(m_old, jnp.max(sm, axis=-1, keepdims=True))
        alpha = jnp.exp(m_old - m_new)
        e = jnp.where(mask, jnp.exp(sm - m_new), 0.0)
        l_scr[...] = alpha * l_scr[...] + jnp.sum(e, axis=-1, keepdims=True)
        acc_scr[...] = alpha * acc_scr[...] + _dot(e.astype(BF16), v)
        m_scr[...] = m_new

    s = _dot_nt(qbd, page_ref[:, 0:width].astype(BF16)) * SCALE
    tok = lax.broadcasted_iota(jnp.int32, s.shape, 1)
    blocks_per_page = page // L_SLC
    sel = sel_ref[...]
    mask = jnp.zeros(s.shape, jnp.bool_)
    for c in range(blocks_per_page):
        mask = mask | ((tok // L_SLC == c) & (sel[:, c:c + 1] > 0.5))
    online_update(s, mask, page_ref[:, width:2 * width].astype(BF16))

    @pl.when(p == n_pages - 1)
    def _():
        r = lax.broadcasted_iota(jnp.int32, (rows, 1), 0)
        t = lax.rem(r, dec_seq)
        s = _dot_nt(qbd, kn_ref[...].astype(BF16)) * SCALE
        j = lax.broadcasted_iota(jnp.int32, s.shape, 1)
        online_update(s, (j <= t) & (j < dec_seq), vn_ref[...].astype(BF16))
        o_s = acc_scr[...] / jnp.maximum(l_scr[...], 1e-30)

        sb = _dot_nt(qbd, kb_ref[...].astype(BF16)) * SCALE
        sn = _dot_nt(qbd, kwn_ref[...].astype(BF16)) * SCALE
        ib = lax.broadcasted_iota(jnp.int32, sb.shape, 1)
        kpos = past - w_buf + ib
        d = (past + t) - kpos
        mb = (d >= 0) & (d < WINDOW) & (kpos >= 0)
        jn = lax.broadcasted_iota(jnp.int32, sn.shape, 1)
        mn = (jn <= t) & (jn < dec_seq) & (t - jn < WINDOW)
        smb = jnp.where(mb, sb, NEG_BIG)
        smn = jnp.where(mn, sn, NEG_BIG)
        mx = jnp.maximum(jnp.max(smb, axis=-1, keepdims=True), jnp.max(smn, axis=-1, keepdims=True))
        eb = jnp.where(mb, jnp.exp(smb - mx), 0.0)
        en = jnp.where(mn, jnp.exp(smn - mx), 0.0)
        den = jnp.maximum(jnp.sum(eb, axis=-1, keepdims=True) + jnp.sum(en, axis=-1, keepdims=True),
                          1e-30)
        o_w = (_dot((eb / den).astype(BF16), vb_ref[...].astype(BF16))
               + _dot((en / den).astype(BF16), vwn_ref[...].astype(BF16)))

        for h in range(n_kv):
            r0 = h * rows_per_head
            c0 = h * HEAD_DIM
            gt = gt_ref[h]
            o_ref[h] = (gt[:, 0:1] * oc_ref[h, 0:rows_per_head, :]
                        + gt[:, 1:2] * o_s[r0:r0 + rows_per_head, c0:c0 + HEAD_DIM]
                        + gt[:, 2:3] * o_w[r0:r0 + rows_per_head, c0:c0 + HEAD_DIM])


def _slc_sample(page_table, cache_l, q_pad, sel_pages, k_new, v_new, win_l, kw_new, vw_new, o_c,
                gates, *, layer, n_kv, gqa, dec_seq, name):
    batch, n_pages = page_table.shape
    page = cache_l.shape[2]
    past = n_pages * page
    width = n_kv * HEAD_DIM
    rph = gqa * dec_seq
    rows = n_kv * rph
    w_buf = win_l.shape[2]
    n_new = k_new.shape[1]
    kern = functools.partial(_slc_s_kernel, n_kv=n_kv, n_pages=n_pages, rows_per_head=rph,
                             dec_seq=dec_seq, past=past, w_buf=w_buf)

    def new_spec():
        return pl.BlockSpec((None, n_new, width), lambda b, p, pt: (b, 0, 0))

    grid_spec = pltpu.PrefetchScalarGridSpec(
        num_scalar_prefetch=1,
        grid=(batch, n_pages),
        in_specs=[pl.BlockSpec((None, None, page, 2 * width), lambda b, p, pt: (layer, pt[b, p], 0, 1)),
                  pl.BlockSpec((None, n_kv, 128, HEAD_DIM), lambda b, p, pt: (b, 0, 0, 0)),
                  pl.BlockSpec((None, None, rows, sel_pages.shape[3]), lambda b, p, pt: (b, p, 0, 0)),
                  new_spec(), new_spec(),
                  pl.BlockSpec((None, None, w_buf, width), lambda b, p, pt: (layer, b, 0, 0)),
                  pl.BlockSpec((None, None, w_buf, width), lambda b, p, pt: (layer, b, 0, 1)),
                  new_spec(), new_spec(),
                  pl.BlockSpec((None, n_kv, 128, HEAD_DIM), lambda b, p, pt: (b, 0, 0, 0)),
                  pl.BlockSpec((None, n_kv, rph, 8), lambda b, p, pt: (b, 0, 0, 0))],
        out_specs=pl.BlockSpec((None, n_kv, rph, HEAD_DIM), lambda b, p, pt: (b, 0, 0, 0)),
        scratch_shapes=[pltpu.VMEM((rows, width), F32),
                        pltpu.VMEM((rows, 1), F32),
                        pltpu.VMEM((rows, 1), F32),
                        pltpu.VMEM((rows, width), F32)],
    )
    return pl.pallas_call(
        kern,
        out_shape=jax.ShapeDtypeStruct((batch, n_kv, rph, HEAD_DIM), F32),
        grid_spec=grid_spec,
        compiler_params=_cparams("parallel", "arbitrary"),
        name=name,
    )(page_table, cache_l, q_pad, sel_pages, k_new, v_new, win_l, win_l, kw_new, vw_new, o_c, gates)


def _rope_tables(pos):
    inv = ROPE_THETA ** (-jnp.arange(ROT_HALF, dtype=F32) * 2.0 / ROT_DIM)
    ang = pos.astype(F32)[:, None] * inv[None, :]
    cos, sin = jnp.cos(ang), jnp.sin(ang)
    rest = HEAD_DIM - ROT_DIM
    cos_t = jnp.concatenate([cos, cos, jnp.ones((pos.shape[0], rest), F32)], axis=1)
    sin_t = jnp.concatenate([-sin, sin, jnp.zeros((pos.shape[0], rest), F32)], axis=1)
    return cos_t, sin_t


def _prep_layer(l, w_in, sgu_w, sgu_b, sgu_g, cmp_pe, cmp_w1, cmp_w2, w_o, ln_g, ln_b,
                ffn_w_in, ffn_conv_w, ffn_conv_b, ffn_w_down, dims):
    d_a, d_b, d_kv, n_heads, d_ff = dims
    base = 2 * d_a
    n_qkv = d_b + 6 * d_kv
    w = w_in[l]
    wl = {}
    wl["w_uv"] = w[:, :base].astype(BF16)
    wl["w_qkv"] = w[:, base:base + n_qkv].astype(BF16)
    n_gate = 3 * n_heads
    wl["w_gate"] = jnp.pad(w[:, base + n_qkv:], ((0, 0), (0, 128 - n_gate))).astype(BF16)
    wl["sgu_w"] = sgu_w[l]
    wl["sgu_bt"] = sgu_b[l].T
    wl["sgu_g"] = sgu_g[l]
    half = STRIDE * HEAD_DIM
    w1 = cmp_w1[l].reshape(2, 2 * half, HEAD_DIM)
    wl["cmp_w1"] = w1.astype(BF16)
    wl["cmp_w1cat"] = jnp.concatenate([w1[:, :half], w1[:, half:]], axis=2).astype(BF16)
    wl["cmp_w2"] = cmp_w2[l].astype(BF16)
    pe = cmp_pe[l].reshape(2, 1, 2 * half)
    wl["cmp_pe"] = jnp.broadcast_to(pe, (2, 16, 2 * half)).astype(BF16)
    pe2 = cmp_pe[l].reshape(2, 2, 1, half)
    wl["cmp_pecat"] = jnp.concatenate([jnp.broadcast_to(pe2[:, 0], (2, 8, half)),
                                       jnp.broadcast_to(pe2[:, 1], (2, 8, half))], axis=1).astype(BF16)
    wl["w_o_a"] = w_o[l, :d_a].astype(BF16)
    wl["w_o_b"] = w_o[l, d_a:].astype(BF16)
    wl["ln_g"], wl["ln_b"] = ln_g[l], ln_b[l]
    wl["f_gate"] = ffn_w_in[l, :, :d_ff].astype(BF16)
    wl["f_up"] = ffn_w_in[l, :, d_ff:].astype(BF16)
    wl["f_cw"], wl["f_cb"] = ffn_conv_w[l], ffn_conv_b[l]
    wl["f_down"] = ffn_w_down[l].astype(BF16)
    return wl


def _project(x_bf, wl, cos_t, sin_t, dims, tag):
    d_a, d_b, d_kv, n_heads, d_ff = dims
    uv = _matmul([x_bf], [wl["w_uv"]], epilogue="gelu", name=f"proj_uv_{tag}")
    qkv = _rope_matmul(x_bf, wl["w_qkv"], cos_t, sin_t, d_b=d_b, d_kv=d_kv, name=f"proj_qkv_{tag}")
    gates = _matmul([x_bf], [wl["w_gate"]], epilogue="sigmoid", tn=128, name=f"proj_gate_{tag}")
    return uv, qkv, gates


def _mix_and_ffn_tail(x, a_out, b_out, wl, alpha, tag):
    mix = _matmul([a_out, b_out], [wl["w_o_a"], wl["w_o_b"]], name=f"w_o_{tag}")
    return _add_ln(x, mix, wl["ln_g"][0], wl["ln_b"][0], alpha=alpha, name=f"ln1_{tag}")


def _layer_prompt(x, wl, cos_t, sin_t, dims, batch, seq, n_kv, alpha):
    d_a, d_b, d_kv, n_heads, d_ff = dims
    gqa = n_heads // n_kv
    m = batch * seq
    uv, qkv, gates = _project(x.astype(BF16), wl, cos_t, sin_t, dims, "p")
    a_out, _ = _sgu(uv, wl["sgu_w"], wl["sgu_bt"], wl["sgu_g"], rows=CHUNK, name="sgu_p")
    kcmp, vcmp = _compress_prompt(qkv, wl["cmp_w1"], wl["cmp_w2"], wl["cmp_pe"], batch=batch, seq=seq,
                                  n_heads=n_heads, n_kv=n_kv, name="compress_p")
    gates_h = gates[:, :3 * n_heads].reshape(m, 3, n_kv, gqa).transpose(2, 0, 1, 3).reshape(n_kv, m, 3 * gqa)
    b_out = _attn_prompt(qkv, kcmp, vcmp, gates_h, batch=batch, seq=seq, n_heads=n_heads, n_kv=n_kv,
                         name="nsa_p")
    x1, x1b = _mix_and_ffn_tail(x, a_out, b_out, wl, alpha, "p")
    state0 = jnp.zeros((batch, CONV_W - 1, d_ff), F32)
    act, conv_new = _ffn_in_seq(x1b, wl["f_gate"], wl["f_up"], wl["f_cw"], wl["f_cb"], state0,
                                seq_len=seq, name="ffn_in_p")
    f = _matmul([act], [wl["f_down"]], tm=256, tn=512, name="ffn_down_p")
    x2, _ = _add_ln(x1, f, wl["ln_g"][1], wl["ln_b"][1], alpha=alpha, name="ln2_p")
    new_nsa = qkv[:, d_b:d_b + 4 * d_kv].reshape(batch, seq, 4, n_kv, HEAD_DIM)
    new_win = qkv[:, d_b + 4 * d_kv:].reshape(batch, seq, 2, n_kv, HEAD_DIM)
    return x2, new_nsa, new_win[:, -min(WINDOW, seq):], conv_new


def _layer_sample(x, wl, cos_t, sin_t, dims, batch, dec_seq, n_kv, alpha, layer, page_table,
                  cache_view, win_view, conv_state):
    d_a, d_b, d_kv, n_heads, d_ff = dims
    gqa = n_heads // n_kv
    m = batch * dec_seq
    rph = gqa * dec_seq
    uv, qkv, gates = _project(x.astype(BF16), wl, cos_t, sin_t, dims, "s")
    w_small = wl["sgu_w"][:, :dec_seq, :dec_seq]
    eye = jnp.eye(batch, dtype=F32)
    w_bd = jnp.einsum("ab,gts->gatbs", eye, w_small).reshape(-1, m, m)
    bt_bd = jnp.tile(wl["sgu_bt"][:dec_seq], (batch, 1))
    a_out, v_rows = _sgu(uv, w_bd, bt_bd, wl["sgu_g"], rows=m, name="sgu_s")

    q = qkv[:, :d_b].reshape(batch, dec_seq, n_kv, gqa, HEAD_DIM).transpose(0, 2, 3, 1, 4)
    q_pad = jnp.pad(q.reshape(batch, n_kv, rph, HEAD_DIM), ((0, 0), (0, 0), (0, 128 - rph), (0, 0)))
    o_c, sel = _cmp_sample(page_table, cache_view, q_pad, wl["cmp_w1cat"], wl["cmp_w2"], wl["cmp_pecat"],
                           layer=layer, n_kv=n_kv, gqa=gqa, dec_seq=dec_seq, name="cmp_s")
    n_pages = page_table.shape[1]
    bpp = cache_view.shape[2] // L_SLC
    sel_pages = sel[:, :, :dec_seq, :n_pages * bpp].reshape(batch, n_kv, 1, dec_seq, n_pages, bpp)
    sel_pages = jnp.broadcast_to(sel_pages, (batch, n_kv, gqa, dec_seq, n_pages, bpp))
    sel_pages = sel_pages.transpose(0, 4, 1, 2, 3, 5).reshape(batch, n_pages, n_kv * rph, bpp)

    def new_rows(col):
        rows = qkv[:, d_b + col * d_kv:d_b + (col + 1) * d_kv].reshape(batch, dec_seq, d_kv)
        return jnp.pad(rows, ((0, 0), (0, 128 - dec_seq), (0, 0)))

    gates_s = gates[:, :3 * n_heads].reshape(batch, dec_seq, 3, n_kv, gqa).transpose(0, 3, 4, 1, 2)
    gates_s = jnp.pad(gates_s.reshape(batch, n_kv, rph, 3), ((0, 0), (0, 0), (0, 0), (0, 5)))
    b_rows = _slc_sample(page_table, cache_view, q_pad, sel_pages, new_rows(2), new_rows(3), win_view,
                         new_rows(4), new_rows(5), o_c, gates_s, layer=layer, n_kv=n_kv, gqa=gqa,
                         dec_seq=dec_seq, name="slc_s")
    b_out = b_rows.reshape(batch, n_kv, gqa, dec_seq, HEAD_DIM).transpose(0, 3, 1, 2, 4)
    b_out = b_out.reshape(m, d_b).astype(BF16)

    x1, x1b = _mix_and_ffn_tail(x, a_out, b_out, wl, alpha, "s")
    st = conv_state
    zero = jnp.zeros((batch, dec_seq - 1, d_ff), F32)
    h1 = jnp.concatenate([st[:, 1:2], zero], axis=1).reshape(m, d_ff)
    h2 = jnp.concatenate([st, zero[:, 1:]], axis=1).reshape(m, d_ff)
    act, gate = _ffn_in_short(x1b, wl["f_gate"], wl["f_up"], wl["f_cw"], wl["f_cb"], h1, h2,
                              seq_len=dec_seq, name="ffn_in_s")
    f = _matmul([act], [wl["f_down"]], tn=512, name="ffn_down_s")
    x2, _ = _add_ln(x1, f, wl["ln_g"][1], wl["ln_b"][1], alpha=alpha, name="ln2_s")
    new_nsa = qkv[:, d_b:d_b + 4 * d_kv].reshape(batch, dec_seq, 4, n_kv, HEAD_DIM)
    new_win = qkv[:, d_b + 4 * d_kv:].reshape(batch, dec_seq, 2, n_kv, HEAD_DIM)
    conv_new = gate.reshape(batch, dec_seq, d_ff)[:, dec_seq - (CONV_W - 1):]
    return x2, new_nsa, new_win, v_rows.reshape(batch, dec_seq, d_a), conv_new


def kernel(x_prompt, x_sample, cache_nsa_kv, cache_win_kv, state_ffn_conv, page_table, w_in, sgu_w,
           sgu_b, sgu_g, cmp_pe, cmp_w1, cmp_w2, w_o, ln_g, ln_b, ffn_w_in, ffn_conv_w, ffn_conv_b,
           ffn_w_down):
    bp, seq, d_model = x_prompt.shape
    bs, dec_seq, _ = x_sample.shape
    depth = w_in.shape[0]
    n_kv = cache_nsa_kv.shape[4]
    page = cache_nsa_kv.shape[2]
    past = page_table.shape[1] * page
    d_a = d_model // 2
    d_b = d_model - d_a
    n_heads = d_b // HEAD_DIM
    d_kv = n_kv * HEAD_DIM
    d_ff = ffn_conv_w.shape[-1]
    dims = (d_a, d_b, d_kv, n_heads, d_ff)
    alpha = (2 * depth) ** 0.25
    assert dec_seq >= CONV_W - 1 and seq % CHUNK == 0

    cos_p, sin_p = _rope_tables(jnp.tile(jnp.arange(seq, dtype=jnp.int32), bp))
    cos_s, sin_s = _rope_tables(jnp.tile(past + jnp.arange(dec_seq, dtype=jnp.int32), bs))
    cache_view = cache_nsa_kv.reshape(depth, cache_nsa_kv.shape[1], page, 4 * d_kv)
    win_view = cache_win_kv.reshape(depth, bs, cache_win_kv.shape[2], 2 * d_kv)

    xp = x_prompt.reshape(bp * seq, d_model)
    xs = x_sample.reshape(bs * dec_seq, d_model)
    outs = [[] for _ in range(7)]
    for l in range(depth):
        wl = _prep_layer(l, w_in, sgu_w, sgu_b, sgu_g, cmp_pe, cmp_w1, cmp_w2, w_o, ln_g, ln_b,
                         ffn_w_in, ffn_conv_w, ffn_conv_b, ffn_w_down, dims)
        xp, nsa_p, win_p, conv_p = _layer_prompt(xp, wl, cos_p, sin_p, dims, bp, seq, n_kv, alpha)
        xs, nsa_s, win_s, v_s, conv_s = _layer_sample(
            xs, wl, cos_s, sin_s, dims, bs, dec_seq, n_kv, alpha, l, page_table, cache_view, win_view,
            state_ffn_conv[l])
        for acc, val in zip(outs, (nsa_p, nsa_s, win_p, win_s, v_s, conv_p, conv_s)):
            acc.append(val)
    return (xp.reshape(bp, seq, d_model), xs.reshape(bs, dec_seq, d_model),
            *[jnp.stack(o) for o in outs])
```

```python
import functools
import math

import jax
import jax.numpy as jnp
import numpy as np
from jax import lax
from jax.experimental import pallas as pl
from jax.experimental.pallas import tpu as pltpu

HEAD_DIM = 128
CHUNK = 128
STRIDE = 16
L_CMP = 2 * STRIDE
L_SLC = 64
N_SEL = 16
WINDOW = 512
ROT_DIM = HEAD_DIM // 4
ROT_HALF = ROT_DIM // 2
ROPE_THETA = 500000.0
CONV_W = 3
LN_EPS = 1e-5
SCALE = HEAD_DIM ** -0.5
LOGIT_SCALE = SCALE * math.log2(math.e)
SEL_FORCE = 1e9
NEG_BIG = -3.0e38
VMEM_LIMIT = 56 * 1024 * 1024

F32 = jnp.float32
BF16 = jnp.bfloat16


def _cparams(*sem):
    return pltpu.CompilerParams(dimension_semantics=sem, vmem_limit_bytes=VMEM_LIMIT)


def _tile(n, pref, unit=128):
    if n <= pref:
        return n
    t = (pref // unit) * unit
    while t > unit and n % t:
        t -= unit
    assert n % t == 0, (n, pref, unit)
    return t


def _gelu(x):
    return jax.nn.gelu(x, approximate=True)


def _dot(a, b):
    return jnp.dot(a, b, preferred_element_type=F32)


def _dot_nt(a, b):
    return lax.dot_general(a, b, (((1,), (1,)), ((), ())), preferred_element_type=F32)


def _split_hi_lo(x):
    hi = x.astype(BF16)
    lo = (x - hi.astype(F32)).astype(BF16)
    return hi, lo


def _masked_softmax2(s, mask, axis=-1):
    sm = jnp.where(mask, s, NEG_BIG)
    m = jnp.max(sm, axis=axis, keepdims=True)
    p = jnp.where(mask, jnp.exp2(sm - m), 0.0)
    return p / jnp.maximum(jnp.sum(p, axis=axis, keepdims=True), 1e-30)


def _mm_kernel(*refs, n_lhs, epilogue):
    acc = _dot(refs[0][...], refs[n_lhs][...])
    for k in range(1, n_lhs):
        acc = acc + _dot(refs[k][...], refs[n_lhs + k][...])
    o_ref = refs[2 * n_lhs]
    if epilogue == "gelu":
        acc = _gelu(acc)
    elif epilogue == "sigmoid":
        acc = jax.nn.sigmoid(acc)
    o_ref[...] = acc.astype(o_ref.dtype)


def _matmul(xs, ws, *, epilogue="none", out_dtype=F32, tm=512, tn=512, name):
    m = xs[0].shape[0]
    n = ws[0].shape[1]
    tm = _tile(m, tm, 8)
    tn = _tile(n, tn)
    n_lhs = len(xs)
    in_specs = [pl.BlockSpec((tm, x.shape[1]), lambda i, j: (i, 0)) for x in xs]
    in_specs += [pl.BlockSpec((w.shape[0], tn), lambda i, j: (0, j)) for w in ws]
    return pl.pallas_call(
        functools.partial(_mm_kernel, n_lhs=n_lhs, epilogue=epilogue),
        out_shape=jax.ShapeDtypeStruct((m, n), out_dtype),
        grid=(m // tm, n // tn),
        in_specs=in_specs,
        out_specs=pl.BlockSpec((tm, tn), lambda i, j: (i, j)),
        compiler_params=_cparams("parallel", "arbitrary"),
        name=name,
    )(*xs, *ws)


def _rope_mm_kernel(x_ref, w_ref, cos_ref, sin_ref, o_ref, *, n_q_tiles, heads_per_tile):
    j = pl.program_id(1)
    acc = _dot(x_ref[...], w_ref[...])
    rot = jnp.logical_or(j < n_q_tiles, lax.rem(j - n_q_tiles, 2) == 0)
    cosv = jnp.where(rot, cos_ref[...], 1.0)
    sinv = jnp.where(rot, sin_ref[...], 0.0)
    lane = lax.broadcasted_iota(jnp.int32, cosv.shape, 1)
    for h in range(heads_per_tile):
        hs = acc[:, h * HEAD_DIM:(h + 1) * HEAD_DIM]
        partner = jnp.where(lane < ROT_HALF,
                            pltpu.roll(hs, HEAD_DIM - ROT_HALF, 1),
                            pltpu.roll(hs, ROT_HALF, 1))
        o_ref[:, h * HEAD_DIM:(h + 1) * HEAD_DIM] = hs * cosv + partner * sinv


def _rope_matmul(x, w, cos_t, sin_t, *, d_b, d_kv, tm=1024, name):
    m, k = x.shape
    n = w.shape[1]
    tm = _tile(m, tm, 8)
    tn = d_kv
    assert d_b % tn == 0
    kern = functools.partial(_rope_mm_kernel, n_q_tiles=d_b // tn, heads_per_tile=tn // HEAD_DIM)
    return pl.pallas_call(
        kern,
        out_shape=jax.ShapeDtypeStruct((m, n), F32),
        grid=(m // tm, n // tn),
        in_specs=[pl.BlockSpec((tm, k), lambda i, j: (i, 0)),
                  pl.BlockSpec((k, tn), lambda i, j: (0, j)),
                  pl.BlockSpec((tm, HEAD_DIM), lambda i, j: (i, 0)),
                  pl.BlockSpec((tm, HEAD_DIM), lambda i, j: (i, 0))],
        out_specs=pl.BlockSpec((tm, tn), lambda i, j: (i, j)),
        compiler_params=_cparams("parallel", "arbitrary"),
        name=name,
    )(x, w, cos_t, sin_t)


def _add_ln_kernel(x_ref, r_ref, g_ref, b_ref, y_ref, yb_ref, *, alpha):
    z = alpha * x_ref[...] + r_ref[...]
    mu = jnp.mean(z, axis=-1, keepdims=True)
    zc = z - mu
    var = jnp.mean(zc * zc, axis=-1, keepdims=True)
    y = zc * lax.rsqrt(var + LN_EPS) * g_ref[...] + b_ref[...]
    y_ref[...] = y
    yb_ref[...] = y.astype(BF16)


def _add_ln(x, r, g, b, *, alpha, name):
    m, d = x.shape
    tr = _tile(m, 256, 8)
    return pl.pallas_call(
        functools.partial(_add_ln_kernel, alpha=alpha),
        out_shape=(jax.ShapeDtypeStruct((m, d), F32), jax.ShapeDtypeStruct((m, d), BF16)),
        grid=(m // tr,),
        in_specs=[pl.BlockSpec((tr, d), lambda i: (i, 0)),
                  pl.BlockSpec((tr, d), lambda i: (i, 0)),
                  pl.BlockSpec((1, d), lambda i: (0, 0)),
                  pl.BlockSpec((1, d), lambda i: (0, 0))],
        out_specs=(pl.BlockSpec((tr, d), lambda i: (i, 0)),
                   pl.BlockSpec((tr, d), lambda i: (i, 0))),
        compiler_params=_cparams("parallel"),
        name=name,
    )(x, r, g.reshape(1, d), b.reshape(1, d))


def _sgu_kernel(uv_ref, w_ref, bt_ref, g_ref, a_ref, vn_ref, *, d_a, n_groups):
    rows = w_ref.shape[1]
    r = lax.broadcasted_iota(jnp.int32, (rows, rows), 0)
    c = lax.broadcasted_iota(jnp.int32, (rows, rows), 1)
    causal = r >= c
    for g in range(n_groups):
        lo = g * HEAD_DIM
        v = uv_ref[:, d_a + lo:d_a + lo + HEAD_DIM]
        mu = jnp.mean(v, axis=-1, keepdims=True)
        vc = v - mu
        var = jnp.mean(vc * vc, axis=-1, keepdims=True)
        vn = vc * lax.rsqrt(var + LN_EPS) * g_ref[:, lo:lo + HEAD_DIM]
        vn_ref[:, lo:lo + HEAD_DIM] = vn
        w = jnp.where(causal, w_ref[g], 0.0).astype(BF16)
        mixed = _dot(w, vn.astype(BF16)) + bt_ref[:, g:g + 1]
        a_ref[:, lo:lo + HEAD_DIM] = (uv_ref[:, lo:lo + HEAD_DIM] * mixed).astype(a_ref.dtype)


def _sgu(uv, w, bt, gain, *, rows, name):
    m = uv.shape[0]
    d_a = uv.shape[1] // 2
    n_groups = d_a // HEAD_DIM
    return pl.pallas_call(
        functools.partial(_sgu_kernel, d_a=d_a, n_groups=n_groups),
        out_shape=(jax.ShapeDtypeStruct((m, d_a), BF16), jax.ShapeDtypeStruct((m, d_a), F32)),
        grid=(m // rows,),
        in_specs=[pl.BlockSpec((rows, 2 * d_a), lambda i: (i, 0)),
                  pl.BlockSpec((n_groups, rows, rows), lambda i: (0, 0, 0)),
                  pl.BlockSpec((rows, n_groups), lambda i: (0, 0)),
                  pl.BlockSpec((1, d_a), lambda i: (0, 0))],
        out_specs=(pl.BlockSpec((rows, d_a), lambda i: (i, 0)),
                   pl.BlockSpec((rows, d_a), lambda i: (i, 0))),
        compiler_params=_cparams("parallel"),
        name=name,
    )(uv, w, bt, gain.reshape(1, d_a))


def _conv_act(gate, g1, g2, up, cw_ref, cb_ref):
    c = cb_ref[...] + g2 * cw_ref[0:1, :] + g1 * cw_ref[1:2, :] + gate * cw_ref[2:3, :]
    return _gelu(c) * up


def _ffn_in_seq_kernel(x_ref, wg_ref, wu_ref, cw_ref, cb_ref, st_ref, a_ref, cn_ref, carry_ref,
                       *, tiles_per_seq):
    i = pl.program_id(1)
    x = x_ref[...]
    gate = _dot(x, wg_ref[...])
    up = _dot(x, wu_ref[...])
    tm = gate.shape[0]

    @pl.when(lax.rem(i, tiles_per_seq) == 0)
    def _():
        carry_ref[0:2, :] = st_ref[...]

    prev2 = carry_ref[0:1, :]
    prev1 = carry_ref[1:2, :]
    row = lax.broadcasted_iota(jnp.int32, gate.shape, 0)
    g1 = jnp.where(row == 0, prev1, pltpu.roll(gate, 1, 0))
    g2 = jnp.where(row == 0, prev2, jnp.where(row == 1, prev1, pltpu.roll(gate, 2, 0)))
    a_ref[...] = _conv_act(gate, g1, g2, up, cw_ref, cb_ref).astype(a_ref.dtype)
    tail = gate[tm - 2:tm, :]
    carry_ref[0:2, :] = tail
    cn_ref[...] = tail


def _ffn_in_seq(x, wg, wu, cw, cb, state, *, seq_len, tm=1024, tn=256, name):
    m, k = x.shape
    d_ff = wg.shape[1]
    tm = _tile(seq_len, tm, 8)
    tn = _tile(d_ff, tn)
    tps = seq_len // tm
    n_seq = m // seq_len
    return pl.pallas_call(
        functools.partial(_ffn_in_seq_kernel, tiles_per_seq=tps),
        out_shape=(jax.ShapeDtypeStruct((m, d_ff), BF16),
                   jax.ShapeDtypeStruct((n_seq, CONV_W - 1, d_ff), F32)),
        grid=(d_ff // tn, m // tm),
        in_specs=[pl.BlockSpec((tm, k), lambda j, i: (i, 0)),
                  pl.BlockSpec((k, tn), lambda j, i: (0, j)),
                  pl.BlockSpec((k, tn), lambda j, i: (0, j)),
                  pl.BlockSpec((CONV_W, tn), lambda j, i: (0, j)),
                  pl.BlockSpec((1, tn), lambda j, i: (0, j)),
                  pl.BlockSpec((None, CONV_W - 1, tn), lambda j, i: (i // tps, 0, j))],
        out_specs=(pl.BlockSpec((tm, tn), lambda j, i: (i, j)),
                   pl.BlockSpec((None, CONV_W - 1, tn), lambda j, i: (i // tps, 0, j))),
        scratch_shapes=[pltpu.VMEM((8, tn), F32)],
        compiler_params=_cparams("arbitrary", "arbitrary"),
        name=name,
    )(x, wg, wu, cw, cb.reshape(1, d_ff), state)


def _ffn_in_short_kernel(x_ref, wg_ref, wu_ref, cw_ref, cb_ref, h1_ref, h2_ref, a_ref, gate_ref,
                         *, seq_len):
    x = x_ref[...]
    gate = _dot(x, wg_ref[...])
    up = _dot(x, wu_ref[...])
    t = lax.rem(lax.broadcasted_iota(jnp.int32, gate.shape, 0), seq_len)
    g1 = jnp.where(t >= 1, pltpu.roll(gate, 1, 0), h1_ref[...])
    g2 = jnp.where(t >= 2, pltpu.roll(gate, 2, 0), h2_ref[...])
    a_ref[...] = _conv_act(gate, g1, g2, up, cw_ref, cb_ref).astype(a_ref.dtype)
    gate_ref[...] = gate


def _ffn_in_short(x, wg, wu, cw, cb, h1, h2, *, seq_len, tn=256, name):
    m, k = x.shape
    d_ff = wg.shape[1]
    tn = _tile(d_ff, tn)
    return pl.pallas_call(
        functools.partial(_ffn_in_short_kernel, seq_len=seq_len),
        out_shape=(jax.ShapeDtypeStruct((m, d_ff), BF16), jax.ShapeDtypeStruct((m, d_ff), F32)),
        grid=(d_ff // tn,),
        in_specs=[pl.BlockSpec((m, k), lambda j: (0, 0)),
                  pl.BlockSpec((k, tn), lambda j: (0, j)),
                  pl.BlockSpec((k, tn), lambda j: (0, j)),
                  pl.BlockSpec((CONV_W, tn), lambda j: (0, j)),
                  pl.BlockSpec((1, tn), lambda j: (0, j)),
                  pl.BlockSpec((m, tn), lambda j: (0, j)),
                  pl.BlockSpec((m, tn), lambda j: (0, j))],
        out_specs=(pl.BlockSpec((m, tn), lambda j: (0, j)),
                   pl.BlockSpec((m, tn), lambda j: (0, j))),
        compiler_params=_cparams("parallel"),
        name=name,
    )(x, wg, wu, cw, cb.reshape(1, d_ff), h1, h2)


def _compress_rows(xa, w1a, w1b, w2, hpe):
    a = _dot(xa, w1a)
    b = _dot(xa, w1b)
    n = a.shape[0]
    h = a + pltpu.roll(b, n - 1, 0) + hpe
    return _dot(_gelu(h).astype(BF16), w2)


def _pos_embed_term(pe_ref, w1_ref, kind):
    return _dot(pe_ref[kind], w1_ref[kind])[0:1, :]


def _compress_p_kernel(k_ref, v_ref, w1_ref, w2_ref, pe_ref, kc_ref, vc_ref, *, n_chunks):
    half = STRIDE * HEAD_DIM
    for kind, (src, dst) in enumerate(((k_ref, kc_ref), (v_ref, vc_ref))):
        xa = jnp.concatenate(
            [src[pl.ds(s, n_chunks, stride=STRIDE), :] for s in range(STRIDE)], axis=1).astype(BF16)
        hpe = _pos_embed_term(pe_ref, w1_ref, kind)
        out = _compress_rows(xa, w1_ref[kind, 0:half, :], w1_ref[kind, half:2 * half, :],
                             w2_ref[kind], hpe)
        dst[...] = out.astype(dst.dtype)


def _compress_prompt(qkv, w1, w2, pe, *, batch, seq, n_heads, n_kv, name):
    n_chunks = seq // STRIDE
    kcol = n_heads
    vcol = n_heads + n_kv
    out = jax.ShapeDtypeStruct((batch, n_kv, n_chunks, HEAD_DIM), BF16)
    ospec = pl.BlockSpec((None, None, n_chunks, HEAD_DIM), lambda b, h: (b, h, 0, 0))
    return pl.pallas_call(
        functools.partial(_compress_p_kernel, n_chunks=n_chunks),
        out_shape=(out, out),
        grid=(batch, n_kv),
        in_specs=[pl.BlockSpec((seq, HEAD_DIM), lambda b, h: (b, kcol + h)),
                  pl.BlockSpec((seq, HEAD_DIM), lambda b, h: (b, vcol + h)),
                  pl.BlockSpec(w1.shape, lambda b, h: (0, 0, 0)),
                  pl.BlockSpec(w2.shape, lambda b, h: (0, 0, 0)),
                  pl.BlockSpec(pe.shape, lambda b, h: (0, 0, 0))],
        out_specs=(ospec, ospec),
        compiler_params=_cparams("parallel", "parallel"),
        name=name,
    )(qkv, qkv, w1, w2, pe)


def _block_scores(imp, pos, n_blocks):
    j = lax.broadcasted_iota(jnp.int32, imp.shape, 1)
    cur = pos // L_SLC
    forced = (j == 0) | (j == cur) | (j == cur - 1)
    valid = j * L_SLC <= pos
    score = jnp.where(valid, jnp.where(forced, SEL_FORCE, imp), -SEL_FORCE)
    return jnp.where(j < n_blocks, score, NEG_BIG)


def _overlap_matrix(nc, nb, rows, cols):
    i = np.arange(nc)[:, None]
    j = np.arange(nb)[None, :]
    lo = np.maximum(i * STRIDE, j * L_SLC)
    hi = np.minimum(i * STRIDE + L_CMP, (j + 1) * L_SLC)
    ov = np.zeros((rows, cols), np.float32)
    ov[:nc, :nb] = np.maximum(hi - lo, 0) / STRIDE
    return ov


def _block_to_key_matrix(n_groups, lanes, keys_per_group):
    ex = np.zeros((n_groups, lanes, keys_per_group), np.float32)
    for c in range(n_groups):
        k = np.arange(keys_per_group)
        ex[c, (c * keys_per_group + k) // L_SLC, k] = 1.0
    return ex


def _attn_p_kernel(q_ref, kc_ref, vc_ref, ks_ref, vs_ref, kw_ref, vw_ref, gt_ref, ov_ref, ex_ref,
                   o_ref, m_scr, l_scr, acc_scr, *, tq, tk, gqa, n_blocks, n_cmp, win_keys):
    qi = pl.program_id(2)
    t0 = qi * tq
    q = q_ref[...] * LOGIT_SCALE
    qs = jnp.concatenate([q[:, g * HEAD_DIM:(g + 1) * HEAD_DIM] for g in range(gqa)],
                         axis=0).astype(BF16)
    pos_t = t0 + lax.broadcasted_iota(jnp.int32, (tq, 1), 0)
    pos = jnp.concatenate([pos_t] * gqa, axis=0)

    s = _dot_nt(qs, kc_ref[...])
    n = lax.broadcasted_iota(jnp.int32, s.shape, 1)
    p = _masked_softmax2(s, (n * STRIDE + (L_CMP - 1) <= pos) & (n < n_cmp))
    o_c = _dot(p.astype(BF16), vc_ref[...])
    p_grp = p[0:tq]
    for g in range(1, gqa):
        p_grp = p_grp + p[g * tq:(g + 1) * tq]

    hi, lo = _split_hi_lo(p_grp)
    imp = _dot(hi, ov_ref[...]) + _dot(lo, ov_ref[...])
    score = _block_scores(imp, pos_t, n_blocks)
    lane = lax.broadcasted_iota(jnp.int32, score.shape, 1)
    rank = jnp.zeros(score.shape, F32)
    for i in range(n_blocks):
        ci = score[:, i:i + 1]
        beats = (ci > score) | ((ci == score) & (lane > i))
        rank = rank + jnp.where(beats, 1.0, 0.0)
    sel = jnp.where(rank < float(min(N_SEL, n_blocks)), 1.0, 0.0).astype(BF16)

    m_scr[...] = jnp.full(m_scr.shape, NEG_BIG, F32)
    l_scr[...] = jnp.zeros(l_scr.shape, F32)
    acc_scr[...] = jnp.zeros(acc_scr.shape, F32)

    def key_tile(c, carry):
        k0 = pl.multiple_of(c * tk, tk)
        s = _dot_nt(qs, ks_ref[pl.ds(k0, tk), :].astype(BF16))
        sel_keys = _dot(sel, ex_ref[c])
        kpos = k0 + lax.broadcasted_iota(jnp.int32, sel_keys.shape, 1)
        bias = jnp.where(sel_keys > jnp.where(kpos <= pos_t, 0.5, 2.0), 0.0, NEG_BIG)
        sm = s + jnp.concatenate([bias] * gqa, axis=0)
        m_old = m_scr[...]
        m_new = jnp.maximum(m_old, jnp.max(sm, axis=-1, keepdims=True))
        alpha = jnp.exp2(m_old - m_new)
        e = jnp.exp2(sm - m_new)
        l_scr[...] = alpha * l_scr[...] + jnp.sum(e, axis=-1, keepdims=True)
        acc_scr[...] = alpha * acc_scr[...] + _dot(e.astype(BF16),
                                                   vs_ref[pl.ds(k0, tk), :].astype(BF16))
        m_scr[...] = m_new
        return carry

    lax.fori_loop(0, (t0 + tq - 1) // tk + 1, key_tile, 0)
    o_s = acc_scr[...] / jnp.maximum(l_scr[...], 1e-30)

    start = pl.multiple_of(jnp.maximum(t0 + tq - win_keys, 0), 128)
    s = _dot_nt(qs, kw_ref[pl.ds(start, win_keys), :].astype(BF16))
    d = pos_t - (start + lax.broadcasted_iota(jnp.int32, (tq, win_keys), 1))
    bias = jnp.where((d >= 0) & (d < WINDOW), 0.0, NEG_BIG)
    sm = s + jnp.concatenate([bias] * gqa, axis=0)
    e = jnp.exp2(sm - jnp.max(sm, axis=-1, keepdims=True))
    o_w = (_dot(e.astype(BF16), vw_ref[pl.ds(start, win_keys), :].astype(BF16))
           / jnp.sum(e, axis=-1, keepdims=True))

    gt = gt_ref[...]
    for g in range(gqa):
        r0 = g * tq
        out = (gt[:, g:g + 1] * o_c[r0:r0 + tq]
               + gt[:, gqa + g:gqa + g + 1] * o_s[r0:r0 + tq]
               + gt[:, 2 * gqa + g:2 * gqa + g + 1] * o_w[r0:r0 + tq])
        o_ref[:, g * HEAD_DIM:(g + 1) * HEAD_DIM] = out.astype(o_ref.dtype)


def _attn_prompt(qkv, kcmp, vcmp, gates_h, *, batch, seq, n_heads, n_kv, tq=128, tk=512, name):
    gqa = n_heads // n_kv
    tq = _tile(seq, tq, 8)
    tk = _tile(seq, tk)
    nq = seq // tq
    n_chunks = seq // STRIDE
    n_cmp = n_chunks - 1
    n_blocks = -(-seq // L_SLC)
    lanes = -(-n_blocks // 128) * 128
    ov = jnp.asarray(_overlap_matrix(n_cmp, n_blocks, n_chunks, lanes), BF16)
    ex = jnp.asarray(_block_to_key_matrix(seq // tk, lanes, tk), BF16)
    win_keys = min(WINDOW + tq, seq)
    ks_col, vs_col = n_heads + 2 * n_kv, n_heads + 3 * n_kv
    kw_col, vw_col = n_heads + 4 * n_kv, n_heads + 5 * n_kv
    rows = gqa * tq
    kern = functools.partial(_attn_p_kernel, tq=tq, tk=tk, gqa=gqa, n_blocks=n_blocks,
                             n_cmp=n_cmp, win_keys=win_keys)

    def kv_spec(col):
        return pl.BlockSpec((seq, HEAD_DIM), lambda b, h, i: (b, col + h))

    cmp_spec = pl.BlockSpec((None, None, n_chunks, HEAD_DIM), lambda b, h, i: (b, h, 0, 0))
    return pl.pallas_call(
        kern,
        out_shape=jax.ShapeDtypeStruct((batch * seq, n_heads * HEAD_DIM), BF16),
        grid=(batch, n_kv, nq),
        in_specs=[pl.BlockSpec((tq, gqa * HEAD_DIM), lambda b, h, i: (b * nq + i, h)),
                  cmp_spec, cmp_spec,
                  kv_spec(ks_col), kv_spec(vs_col), kv_spec(kw_col), kv_spec(vw_col),
                  pl.BlockSpec((None, tq, 3 * gqa), lambda b, h, i: (h, b * nq + i, 0)),
                  pl.BlockSpec(ov.shape, lambda b, h, i: (0, 0)),
                  pl.BlockSpec(ex.shape, lambda b, h, i: (0, 0, 0))],
        out_specs=pl.BlockSpec((tq, gqa * HEAD_DIM), lambda b, h, i: (b * nq + i, h)),
        scratch_shapes=[pltpu.VMEM((rows, 1), F32),
                        pltpu.VMEM((rows, 1), F32),
                        pltpu.VMEM((rows, HEAD_DIM), F32)],
        compiler_params=_cparams("parallel", "parallel", "arbitrary"),
        name=name,
    )(qkv, kcmp, vcmp, qkv, qkv, qkv, qkv, gates_h, ov, ex)


def _head_slabs(rows_ref, first, n_heads):
    return jnp.concatenate([rows_ref[:, first + h, :] for h in range(n_heads)], axis=1)


def _cmp_s_kernel(pt_ref, *refs, n_kv, n_steps, pages_per_step, steps_per_group, n_cmp, n_blocks,
                  dec_seq, gqa, past):
    del pt_ref
    page_refs = refs[:pages_per_step]
    q_ref, w1_ref, w2_ref, pe_ref, ovt_ref, oc_ref, sel_ref, x_scr, ab_scr = refs[pages_per_step:]
    p = pl.program_id(1)
    n_kh = 2 * n_kv
    page = page_refs[0].shape[0]
    cpp = page // STRIDE
    step_chunks = cpp * pages_per_step
    group_chunks = step_chunks * steps_per_group
    c0 = pl.multiple_of(lax.rem(p, steps_per_group) * step_chunks, 8)
    for k, page_ref in enumerate(page_refs):
        for s in range(STRIDE):
            for kh in range(n_kh):
                x_scr[kh, pl.ds(c0 + k * cpp, cpp), s * HEAD_DIM:(s + 1) * HEAD_DIM] = (
                    page_ref[pl.ds(s, cpp, stride=STRIDE), kh, :])

    @pl.when(lax.rem(p, steps_per_group) == steps_per_group - 1)
    def _():
        g0 = pl.multiple_of((p // steps_per_group) * group_chunks, 8)
        for kh in range(n_kh):
            kind = kh // n_kv
            ab_scr[kh, pl.ds(g0, group_chunks), :] = _dot(x_scr[kh].astype(BF16), w1_ref[kind])

    @pl.when(p == n_steps - 1)
    def _():
        n_chunks = ab_scr.shape[1]
        comp = []
        for kh in range(n_kh):
            kind = kh // n_kv
            hpe = _dot(pe_ref[kind], w1_ref[kind])
            hpe = hpe[0:1, 0:HEAD_DIM] + hpe[8:9, HEAD_DIM:2 * HEAD_DIM]
            ab = ab_scr[kh]
            h = ab[:, 0:HEAD_DIM] + pltpu.roll(ab[:, HEAD_DIM:2 * HEAD_DIM], n_chunks - 1, 0) + hpe
            comp.append(_dot(_gelu(h).astype(BF16), w2_ref[kind]).astype(BF16))
        qrows = lax.broadcasted_iota(jnp.int32, (1, HEAD_DIM), 1)
        pos = past + lax.rem(qrows, dec_seq)
        for h in range(n_kv):
            kc, vc = comp[h], comp[n_kv + h]
            st = _dot_nt(kc, (q_ref[h] * LOGIT_SCALE).astype(BF16))
            n = lax.broadcasted_iota(jnp.int32, st.shape, 0)
            pt = _masked_softmax2(st, (n * STRIDE + (L_CMP - 1) <= pos) & (n < n_cmp), axis=0)
            oc_ref[h] = _dot(pt.T.astype(BF16), vc)
            pg = pt
            for g in range(1, gqa):
                pg = pg + pltpu.roll(pt, HEAD_DIM - g * dec_seq, 1)
            hi, lo = _split_hi_lo(pg)
            imp_t = _dot(ovt_ref[...], hi) + _dot(ovt_ref[...], lo)
            imp = imp_t.T
            tpos = past + lax.broadcasted_iota(jnp.int32, (imp.shape[0], 1), 0)
            score = _block_scores(imp, tpos, n_blocks)
            score_t = score.T
            nb_pad = score.shape[1]
            ii = lax.broadcasted_iota(jnp.int32, (nb_pad, nb_pad), 0)
            jj = lax.broadcasted_iota(jnp.int32, (nb_pad, nb_pad), 1)
            for t in range(dec_seq):
                col = score_t[:, t:t + 1]
                rowv = score[t:t + 1, :]
                beats = (col > rowv) | ((col == rowv) & (ii < jj))
                rank = jnp.sum(jnp.where(beats, 1.0, 0.0), axis=0, keepdims=True)
                sel_ref[h, t:t + 1, :] = jnp.where(rank < float(min(N_SEL, n_blocks)), 1.0, 0.0)
            sel_ref[h, dec_seq:, :] = jnp.zeros((sel_ref.shape[1] - dec_seq, nb_pad), F32)


def _page_specs(page, rows, row_block, layer, pages_per_step):
    def spec(k):
        return pl.BlockSpec((None, None, page, rows, HEAD_DIM),
                            lambda b, p, pt: (layer, pt[b, p * pages_per_step + k], 0, row_block, 0))
    return [spec(k) for k in range(pages_per_step)]


def _cmp_sample(page_table, cache_rows, q_pad, w1cat, w2, pe, *, layer, n_kv, gqa, dec_seq, name):
    batch, n_pages = page_table.shape
    page = cache_rows.shape[2]
    past = n_pages * page
    cpp = page // STRIDE
    n_chunks = past // STRIDE
    assert dec_seq < STRIDE and n_chunks % 8 == 0
    n_cmp = (past + dec_seq) // STRIDE - 1
    n_blocks = -(-(past + dec_seq) // L_SLC)
    nb_pad = -(-n_blocks // 128) * 128
    pps = _tile(n_pages, 4, 1)
    n_steps = n_pages // pps
    spg = _tile(n_steps, max(1, 128 // (cpp * pps)), 1)
    n_kh = 2 * n_kv
    ovt = jnp.asarray(_overlap_matrix(n_cmp, n_blocks, n_chunks, nb_pad).T.copy(), BF16)
    kern = functools.partial(_cmp_s_kernel, n_kv=n_kv, n_steps=n_steps, pages_per_step=pps,
                             steps_per_group=spg, n_cmp=n_cmp, n_blocks=n_blocks, dec_seq=dec_seq,
                             gqa=gqa, past=past)
    grid_spec = pltpu.PrefetchScalarGridSpec(
        num_scalar_prefetch=1,
        grid=(batch, n_steps),
        in_specs=_page_specs(page, n_kh, 0, layer, pps) + [
            pl.BlockSpec((None, n_kv, 128, HEAD_DIM), lambda b, p, pt: (b, 0, 0, 0)),
            pl.BlockSpec(w1cat.shape, lambda b, p, pt: (0, 0, 0)),
            pl.BlockSpec(w2.shape, lambda b, p, pt: (0, 0, 0)),
            pl.BlockSpec(pe.shape, lambda b, p, pt: (0, 0, 0)),
            pl.BlockSpec(ovt.shape, lambda b, p, pt: (0, 0))],
        out_specs=(pl.BlockSpec((None, n_kv, 128, HEAD_DIM), lambda b, p, pt: (b, 0, 0, 0)),
                   pl.BlockSpec((None, n_kv, 8, nb_pad), lambda b, p, pt: (b, 0, 0, 0))),
        scratch_shapes=[pltpu.VMEM((n_kh, spg * pps * cpp, STRIDE * HEAD_DIM), F32),
                        pltpu.VMEM((n_kh, n_chunks, 2 * HEAD_DIM), F32)],
    )
    return pl.pallas_call(
        kern,
        out_shape=(jax.ShapeDtypeStruct((batch, n_kv, 128, HEAD_DIM), F32),
                   jax.ShapeDtypeStruct((batch, n_kv, 8, nb_pad), F32)),
        grid_spec=grid_spec,
        compiler_params=_cparams("parallel", "arbitrary"),
        name=name,
    )(page_table, *([cache_rows] * pps), q_pad, w1cat, w2, pe, ovt)


def _slc_s_kernel(pt_ref, *refs, n_kv, n_steps, pages_per_step, rows_per_head, dec_seq, past, w_buf):
    del pt_ref
    page_refs = refs[:pages_per_step]
    (q_ref, sel_ref, ex_ref, kn_ref, vn_ref, win_ref, kwn_ref, vwn_ref, oc_ref, gt_ref, o_ref,
     qbd_scr, m_scr, l_scr, acc_scr) = refs[pages_per_step:]
    p = pl.program_id(1)
    page = page_refs[0].shape[0]
    n_kh = 2 * n_kv
    rows = n_kv * rows_per_head

    @pl.when(p == 0)
    def _():
        qbd_scr[...] = jnp.zeros(qbd_scr.shape, qbd_scr.dtype)
        for h in range(n_kv):
            qbd_scr[h * rows_per_head:(h + 1) * rows_per_head, h * HEAD_DIM:(h + 1) * HEAD_DIM] = (
                q_ref[h, 0:rows_per_head, :] * LOGIT_SCALE)
        m_scr[...] = jnp.full(m_scr.shape, NEG_BIG, F32)
        l_scr[...] = jnp.zeros(l_scr.shape, F32)
        acc_scr[...] = jnp.zeros(acc_scr.shape, F32)

    qbd = qbd_scr[...].astype(BF16)

    def online_update(s, mask, v):
        sm = jnp.where(mask, s, NEG_BIG)
        m_old = m_scr[...]
        m_new = jnp.maximum(m_old, jnp.max(sm, axis=-1, keepdims=True))
        alpha = jnp.exp2(m_old - m_new)
        e = jnp.where(mask, jnp.exp2(sm - m_new), 0.0)
        l_scr[...] = alpha * l_scr[...] + jnp.sum(e, axis=-1, keepdims=True)
        acc_scr[...] = alpha * acc_scr[...] + _dot(e.astype(BF16), v)
        m_scr[...] = m_new

    ks, vs = [], []
    for page_ref in page_refs:
        ks.append(_head_slabs(page_ref, 0, n_kv))
        vs.append(_head_slabs(page_ref, n_kv, n_kv))
    k_all = jnp.concatenate(ks, axis=0).astype(BF16)
    v_all = jnp.concatenate(vs, axis=0).astype(BF16)
    s = _dot_nt(qbd, k_all)
    sel_keys = _dot(sel_ref[...].astype(BF16), ex_ref[...])
    online_update(s, sel_keys > 0.5, v_all)

    @pl.when(p == n_steps - 1)
    def _():
        r = lax.broadcasted_iota(jnp.int32, (rows, 1), 0)
        t = lax.rem(r, dec_seq)
        s = _dot_nt(qbd, kn_ref[...].astype(BF16))
        j = lax.broadcasted_iota(jnp.int32, s.shape, 1)
        online_update(s, (j <= t) & (j < dec_seq), vn_ref[...].astype(BF16))
        o_s = acc_scr[...] / jnp.maximum(l_scr[...], 1e-30)

        sb = _dot_nt(qbd, _head_slabs(win_ref, 0, n_kv).astype(BF16))
        sn = _dot_nt(qbd, kwn_ref[...].astype(BF16))
        ib = lax.broadcasted_iota(jnp.int32, sb.shape, 1)
        kpos = past - w_buf + ib
        d = (past + t) - kpos
        mb = (d >= 0) & (d < WINDOW) & (kpos >= 0)
        jn = lax.broadcasted_iota(jnp.int32, sn.shape, 1)
        mn = (jn <= t) & (jn < dec_seq) & (t - jn < WINDOW)
        smb = jnp.where(mb, sb, NEG_BIG)
        smn = jnp.where(mn, sn, NEG_BIG)
        mx = jnp.maximum(jnp.max(smb, axis=-1, keepdims=True), jnp.max(smn, axis=-1, keepdims=True))
        eb = jnp.where(mb, jnp.exp2(smb - mx), 0.0)
        en = jnp.where(mn, jnp.exp2(smn - mx), 0.0)
        den = jnp.maximum(jnp.sum(eb, axis=-1, keepdims=True) + jnp.sum(en, axis=-1, keepdims=True),
                          1e-30)
        o_w = (_dot((eb / den).astype(BF16), _head_slabs(win_ref, n_kv, n_kv).astype(BF16))
               + _dot((en / den).astype(BF16), vwn_ref[...].astype(BF16)))

        for h in range(n_kv):
            r0 = h * rows_per_head
            c0 = h * HEAD_DIM
            gt = gt_ref[h]
            o_ref[h] = (gt[:, 0:1] * oc_ref[h, 0:rows_per_head, :]
                        + gt[:, 1:2] * o_s[r0:r0 + rows_per_head, c0:c0 + HEAD_DIM]
                        + gt[:, 2:3] * o_w[r0:r0 + rows_per_head, c0:c0 + HEAD_DIM])


def _slc_sample(page_table, cache_rows, q_pad, sel_steps, k_new, v_new, win_rows, kw_new, vw_new, o_c,
                gates, *, layer, n_kv, gqa, dec_seq, pages_per_step, name):
    batch, n_pages = page_table.shape
    page = cache_rows.shape[2]
    past = n_pages * page
    width = n_kv * HEAD_DIM
    n_kh = 2 * n_kv
    rph = gqa * dec_seq
    rows = n_kv * rph
    w_buf = win_rows.shape[2]
    n_new = k_new.shape[1]
    pps = pages_per_step
    n_steps = n_pages // pps
    ex = jnp.asarray(_block_to_key_matrix(1, sel_steps.shape[3], pps * page)[0], BF16)
    kern = functools.partial(_slc_s_kernel, n_kv=n_kv, n_steps=n_steps, pages_per_step=pps,
                             rows_per_head=rph, dec_seq=dec_seq, past=past, w_buf=w_buf)

    def new_spec():
        return pl.BlockSpec((None, n_new, width), lambda b, p, pt: (b, 0, 0))

    grid_spec = pltpu.PrefetchScalarGridSpec(
        num_scalar_prefetch=1,
        grid=(batch, n_steps),
        in_specs=_page_specs(page, n_kh, 1, layer, pps) + [
            pl.BlockSpec((None, n_kv, 128, HEAD_DIM), lambda b, p, pt: (b, 0, 0, 0)),
            pl.BlockSpec((None, None, rows, sel_steps.shape[3]), lambda b, p, pt: (b, p, 0, 0)),
            pl.BlockSpec(ex.shape, lambda b, p, pt: (0, 0)),
            new_spec(), new_spec(),
            pl.BlockSpec((None, None, w_buf, n_kh, HEAD_DIM), lambda b, p, pt: (layer, b, 0, 0, 0)),
            new_spec(), new_spec(),
            pl.BlockSpec((None, n_kv, 128, HEAD_DIM), lambda b, p, pt: (b, 0, 0, 0)),
            pl.BlockSpec((None, n_kv, rph, 8), lambda b, p, pt: (b, 0, 0, 0))],
        out_specs=pl.BlockSpec((None, n_kv, rph, HEAD_DIM), lambda b, p, pt: (b, 0, 0, 0)),
        scratch_shapes=[pltpu.VMEM((rows, width), F32),
                        pltpu.VMEM((rows, 1), F32),
                        pltpu.VMEM((rows, 1), F32),
                        pltpu.VMEM((rows, width), F32)],
    )
    return pl.pallas_call(
        kern,
        out_shape=jax.ShapeDtypeStruct((batch, n_kv, rph, HEAD_DIM), F32),
        grid_spec=grid_spec,
        compiler_params=_cparams("parallel", "arbitrary"),
        name=name,
    )(page_table, *([cache_rows] * pps), q_pad, sel_steps, ex, k_new, v_new, win_rows, kw_new,
      vw_new, o_c, gates)


def _rope_tables(pos):
    inv = ROPE_THETA ** (-jnp.arange(ROT_HALF, dtype=F32) * 2.0 / ROT_DIM)
    ang = pos.astype(F32)[:, None] * inv[None, :]
    cos, sin = jnp.cos(ang), jnp.sin(ang)
    rest = HEAD_DIM - ROT_DIM
    cos_t = jnp.concatenate([cos, cos, jnp.ones((pos.shape[0], rest), F32)], axis=1)
    sin_t = jnp.concatenate([-sin, sin, jnp.zeros((pos.shape[0], rest), F32)], axis=1)
    return cos_t, sin_t


def _prep_layer(l, w_in, sgu_w, sgu_b, sgu_g, cmp_pe, cmp_w1, cmp_w2, w_o, ln_g, ln_b,
                ffn_w_in, ffn_conv_w, ffn_conv_b, ffn_w_down, dims):
    d_a, d_b, d_kv, n_heads, d_ff = dims
    base = 2 * d_a
    n_qkv = d_b + 6 * d_kv
    w = w_in[l]
    wl = {}
    wl["w_uv"] = w[:, :base].astype(BF16)
    wl["w_qkv"] = w[:, base:base + n_qkv].astype(BF16)
    n_gate = 3 * n_heads
    wl["w_gate"] = jnp.pad(w[:, base + n_qkv:], ((0, 0), (0, 128 - n_gate))).astype(BF16)
    wl["sgu_w"] = sgu_w[l]
    wl["sgu_bt"] = sgu_b[l].T
    wl["sgu_g"] = sgu_g[l]
    half = STRIDE * HEAD_DIM
    w1 = cmp_w1[l].reshape(2, 2 * half, HEAD_DIM)
    wl["cmp_w1"] = w1.astype(BF16)
    wl["cmp_w1cat"] = jnp.concatenate([w1[:, :half], w1[:, half:]], axis=2).astype(BF16)
    wl["cmp_w2"] = cmp_w2[l].astype(BF16)
    pe = cmp_pe[l].reshape(2, 1, 2 * half)
    wl["cmp_pe"] = jnp.broadcast_to(pe, (2, 16, 2 * half)).astype(BF16)
    pe2 = cmp_pe[l].reshape(2, 2, 1, half)
    wl["cmp_pecat"] = jnp.concatenate([jnp.broadcast_to(pe2[:, 0], (2, 8, half)),
                                       jnp.broadcast_to(pe2[:, 1], (2, 8, half))], axis=1).astype(BF16)
    wl["w_o_a"] = w_o[l, :d_a].astype(BF16)
    wl["w_o_b"] = w_o[l, d_a:].astype(BF16)
    wl["ln_g"], wl["ln_b"] = ln_g[l], ln_b[l]
    wl["f_gate"] = ffn_w_in[l, :, :d_ff].astype(BF16)
    wl["f_up"] = ffn_w_in[l, :, d_ff:].astype(BF16)
    wl["f_cw"], wl["f_cb"] = ffn_conv_w[l], ffn_conv_b[l]
    wl["f_down"] = ffn_w_down[l].astype(BF16)
    return wl


def _project(x_bf, wl, cos_t, sin_t, dims, tag):
    d_a, d_b, d_kv, n_heads, d_ff = dims
    uv = _matmul([x_bf], [wl["w_uv"]], epilogue="gelu", tm=1024, tn=1024, name=f"proj_uv_{tag}")
    qkv = _rope_matmul(x_bf, wl["w_qkv"], cos_t, sin_t, d_b=d_b, d_kv=d_kv, name=f"proj_qkv_{tag}")
    gates = _matmul([x_bf], [wl["w_gate"]], epilogue="sigmoid", tm=1024, tn=128, name=f"proj_gate_{tag}")
    return uv, qkv, gates


def _mix_and_norm(x, a_out, b_out, wl, alpha, tag):
    mix = _matmul([a_out, b_out], [wl["w_o_a"], wl["w_o_b"]], tm=1024, tn=1024, name=f"w_o_{tag}")
    return _add_ln(x, mix, wl["ln_g"][0], wl["ln_b"][0], alpha=alpha, name=f"ln1_{tag}")


def _layer_prompt(x, wl, cos_t, sin_t, dims, batch, seq, n_kv, alpha):
    d_a, d_b, d_kv, n_heads, d_ff = dims
    gqa = n_heads // n_kv
    m = batch * seq
    uv, qkv, gates = _project(x.astype(BF16), wl, cos_t, sin_t, dims, "p")
    a_out, _ = _sgu(uv, wl["sgu_w"], wl["sgu_bt"], wl["sgu_g"], rows=CHUNK, name="sgu_p")
    kcmp, vcmp = _compress_prompt(qkv, wl["cmp_w1"], wl["cmp_w2"], wl["cmp_pe"], batch=batch, seq=seq,
                                  n_heads=n_heads, n_kv=n_kv, name="compress_p")
    gates_h = gates[:, :3 * n_heads].reshape(m, 3, n_kv, gqa).transpose(2, 0, 1, 3).reshape(n_kv, m, 3 * gqa)
    b_out = _attn_prompt(qkv, kcmp, vcmp, gates_h, batch=batch, seq=seq, n_heads=n_heads, n_kv=n_kv,
                         name="nsa_p")
    x1, x1b = _mix_and_norm(x, a_out, b_out, wl, alpha, "p")
    state0 = jnp.zeros((batch, CONV_W - 1, d_ff), F32)
    act, conv_new = _ffn_in_seq(x1b, wl["f_gate"], wl["f_up"], wl["f_cw"], wl["f_cb"], state0,
                                seq_len=seq, name="ffn_in_p")
    f = _matmul([act], [wl["f_down"]], tm=512, tn=256, name="ffn_down_p")
    x2, _ = _add_ln(x1, f, wl["ln_g"][1], wl["ln_b"][1], alpha=alpha, name="ln2_p")
    new_nsa = qkv[:, d_b:d_b + 4 * d_kv].reshape(batch, seq, 4, n_kv, HEAD_DIM)
    new_win = qkv[:, d_b + 4 * d_kv:].reshape(batch, seq, 2, n_kv, HEAD_DIM)
    return x2, new_nsa, new_win[:, -min(WINDOW, seq):], conv_new


def _layer_sample(x, wl, cos_t, sin_t, dims, batch, dec_seq, n_kv, alpha, layer, page_table,
                  cache_rows, win_rows, conv_state):
    d_a, d_b, d_kv, n_heads, d_ff = dims
    gqa = n_heads // n_kv
    m = batch * dec_seq
    rph = gqa * dec_seq
    uv, qkv, gates = _project(x.astype(BF16), wl, cos_t, sin_t, dims, "s")
    w_small = wl["sgu_w"][:, :dec_seq, :dec_seq]
    eye = jnp.eye(batch, dtype=F32)
    w_bd = jnp.einsum("ab,gts->gatbs", eye, w_small).reshape(-1, m, m)
    bt_bd = jnp.tile(wl["sgu_bt"][:dec_seq], (batch, 1))
    a_out, v_rows = _sgu(uv, w_bd, bt_bd, wl["sgu_g"], rows=m, name="sgu_s")

    q = qkv[:, :d_b].reshape(batch, dec_seq, n_kv, gqa, HEAD_DIM).transpose(0, 2, 3, 1, 4)
    q_pad = jnp.pad(q.reshape(batch, n_kv, rph, HEAD_DIM), ((0, 0), (0, 0), (0, 128 - rph), (0, 0)))
    o_c, sel = _cmp_sample(page_table, cache_rows, q_pad, wl["cmp_w1cat"], wl["cmp_w2"], wl["cmp_pecat"],
                           layer=layer, n_kv=n_kv, gqa=gqa, dec_seq=dec_seq, name="cmp_s")
    n_pages = page_table.shape[1]
    bpp = cache_rows.shape[2] // L_SLC
    pps = _tile(n_pages, 4, 1)
    n_steps = n_pages // pps
    sel_steps = sel[:, :, :dec_seq, :n_pages * bpp].reshape(batch, n_kv, 1, dec_seq, n_steps, pps * bpp)
    sel_steps = jnp.broadcast_to(sel_steps, (batch, n_kv, gqa, dec_seq, n_steps, pps * bpp))
    sel_steps = sel_steps.transpose(0, 4, 1, 2, 3, 5).reshape(batch, n_steps, n_kv * rph, pps * bpp)
    sel_steps = jnp.pad(sel_steps, ((0, 0), (0, 0), (0, 0), (0, 128 - pps * bpp)))

    def new_rows(col):
        rows = qkv[:, d_b + col * d_kv:d_b + (col + 1) * d_kv].reshape(batch, dec_seq, d_kv)
        return jnp.pad(rows, ((0, 0), (0, 128 - dec_seq), (0, 0)))

    gates_s = gates[:, :3 * n_heads].reshape(batch, dec_seq, 3, n_kv, gqa).transpose(0, 3, 4, 1, 2)
    gates_s = jnp.pad(gates_s.reshape(batch, n_kv, rph, 3), ((0, 0), (0, 0), (0, 0), (0, 5)))
    b_rows = _slc_sample(page_table, cache_rows, q_pad, sel_steps, new_rows(2), new_rows(3), win_rows,
                         new_rows(4), new_rows(5), o_c, gates_s, layer=layer, n_kv=n_kv, gqa=gqa,
                         dec_seq=dec_seq, pages_per_step=pps, name="slc_s")
    b_out = b_rows.reshape(batch, n_kv, gqa, dec_seq, HEAD_DIM).transpose(0, 3, 1, 2, 4)
    b_out = b_out.reshape(m, d_b).astype(BF16)

    x1, x1b = _mix_and_norm(x, a_out, b_out, wl, alpha, "s")
    st = conv_state
    zero = jnp.zeros((batch, dec_seq - 1, d_ff), F32)
    h1 = jnp.concatenate([st[:, 1:2], zero], axis=1).reshape(m, d_ff)
    h2 = jnp.concatenate([st, zero[:, 1:]], axis=1).reshape(m, d_ff)
    act, gate = _ffn_in_short(x1b, wl["f_gate"], wl["f_up"], wl["f_cw"], wl["f_cb"], h1, h2,
                              seq_len=dec_seq, name="ffn_in_s")
    f = _matmul([act], [wl["f_down"]], tn=512, name="ffn_down_s")
    x2, _ = _add_ln(x1, f, wl["ln_g"][1], wl["ln_b"][1], alpha=alpha, name="ln2_s")
    new_nsa = qkv[:, d_b:d_b + 4 * d_kv].reshape(batch, dec_seq, 4, n_kv, HEAD_DIM)
    new_win = qkv[:, d_b + 4 * d_kv:].reshape(batch, dec_seq, 2, n_kv, HEAD_DIM)
    conv_new = gate.reshape(batch, dec_seq, d_ff)[:, dec_seq - (CONV_W - 1):]
    return x2, new_nsa, new_win, v_rows.reshape(batch, dec_seq, d_a), conv_new


def kernel(x_prompt, x_sample, cache_nsa_kv, cache_win_kv, state_ffn_conv, page_table, w_in, sgu_w,
           sgu_b, sgu_g, cmp_pe, cmp_w1, cmp_w2, w_o, ln_g, ln_b, ffn_w_in, ffn_conv_w, ffn_conv_b,
           ffn_w_down):
    bp, seq, d_model = x_prompt.shape
    bs, dec_seq, _ = x_sample.shape
    depth = w_in.shape[0]
    n_kv = cache_nsa_kv.shape[4]
    page = cache_nsa_kv.shape[2]
    past = page_table.shape[1] * page
    d_a = d_model // 2
    d_b = d_model - d_a
    n_heads = d_b // HEAD_DIM
    d_kv = n_kv * HEAD_DIM
    d_ff = ffn_conv_w.shape[-1]
    dims = (d_a, d_b, d_kv, n_heads, d_ff)
    alpha = (2 * depth) ** 0.25
    assert dec_seq >= CONV_W - 1 and seq % CHUNK == 0

    cos_p, sin_p = _rope_tables(jnp.tile(jnp.arange(seq, dtype=jnp.int32), bp))
    cos_s, sin_s = _rope_tables(jnp.tile(past + jnp.arange(dec_seq, dtype=jnp.int32), bs))
    cache_rows = cache_nsa_kv.reshape(depth, cache_nsa_kv.shape[1], page, 4 * n_kv, HEAD_DIM)
    win_rows = cache_win_kv.reshape(depth, bs, cache_win_kv.shape[2], 2 * n_kv, HEAD_DIM)

    xp = x_prompt.reshape(bp * seq, d_model)
    xs = x_sample.reshape(bs * dec_seq, d_model)
    outs = [[] for _ in range(7)]
    for l in range(depth):
        wl = _prep_layer(l, w_in, sgu_w, sgu_b, sgu_g, cmp_pe, cmp_w1, cmp_w2, w_o, ln_g, ln_b,
                         ffn_w_in, ffn_conv_w, ffn_conv_b, ffn_w_down, dims)
        xp, nsa_p, win_p, conv_p = _layer_prompt(xp, wl, cos_p, sin_p, dims, bp, seq, n_kv, alpha)
        xs, nsa_s, win_s, v_s, conv_s = _layer_sample(
            xs, wl, cos_s, sin_s, dims, bs, dec_seq, n_kv, alpha, l, page_table, cache_rows, win_rows,
            state_ffn_conv[l])
        for acc, val in zip(outs, (nsa_p, nsa_s, win_p, win_s, v_s, conv_p, conv_s)):
            acc.append(val)
    return (xp.reshape(bp, seq, d_model), xs.reshape(bs, dec_seq, d_model),
            *[jnp.stack(o) for o in outs])
```

```python
import functools
import math

import jax
import jax.numpy as jnp
import numpy as np
from jax import lax
from jax.experimental import pallas as pl
from jax.experimental.pallas import tpu as pltpu

HEAD_DIM = 128
CHUNK = 128
STRIDE = 16
L_CMP = 2 * STRIDE
L_SLC = 64
N_SEL = 16
WINDOW = 512
ROT_DIM = HEAD_DIM // 4
ROT_HALF = ROT_DIM // 2
ROPE_THETA = 500000.0
CONV_W = 3
LN_EPS = 1e-5
SCALE = HEAD_DIM ** -0.5
LOGIT_SCALE = SCALE * math.log2(math.e)
SEL_FORCE = 1e9
NEG_BIG = -3.0e38
VMEM_LIMIT = 56 * 1024 * 1024

F32 = jnp.float32
BF16 = jnp.bfloat16


def _cparams(*sem):
    return pltpu.CompilerParams(dimension_semantics=sem, vmem_limit_bytes=VMEM_LIMIT)


def _tile(n, pref, unit=128):
    if n <= pref:
        return n
    t = (pref // unit) * unit
    while t > unit and n % t:
        t -= unit
    assert n % t == 0, (n, pref, unit)
    return t


def _gelu(x):
    return jax.nn.gelu(x, approximate=True)


def _dot(a, b):
    return jnp.dot(a, b, preferred_element_type=F32)


def _dot_nt(a, b):
    return lax.dot_general(a, b, (((1,), (1,)), ((), ())), preferred_element_type=F32)


def _split_hi_lo(x):
    hi = x.astype(BF16)
    lo = (x - hi.astype(F32)).astype(BF16)
    return hi, lo


def _masked_softmax2(s, mask, axis=-1):
    sm = jnp.where(mask, s, NEG_BIG)
    m = jnp.max(sm, axis=axis, keepdims=True)
    p = jnp.where(mask, jnp.exp2(sm - m), 0.0)
    return p / jnp.maximum(jnp.sum(p, axis=axis, keepdims=True), 1e-30)


def _mm_kernel(*refs, n_lhs, epilogue):
    acc = _dot(refs[0][...], refs[n_lhs][...])
    for k in range(1, n_lhs):
        acc = acc + _dot(refs[k][...], refs[n_lhs + k][...])
    o_ref = refs[2 * n_lhs]
    if epilogue == "gelu":
        acc = _gelu(acc)
    elif epilogue == "sigmoid":
        acc = jax.nn.sigmoid(acc)
    o_ref[...] = acc.astype(o_ref.dtype)


def _matmul(xs, w, *, layer, col_start=0, n=None, epilogue="none", out_dtype=F32, tm=512, tn=512,
            name):
    m, kdim = xs[0].shape
    n = w.shape[2] if n is None else n
    tm = _tile(m, tm, 8)
    tn = _tile(n, tn)
    assert col_start % tn == 0 and all(x.shape == (m, kdim) for x in xs)
    c0 = col_start // tn
    n_lhs = len(xs)
    in_specs = [pl.BlockSpec((tm, kdim), lambda i, j: (i, 0)) for _ in xs]
    in_specs += [pl.BlockSpec((None, kdim, tn), lambda i, j, r=r: (layer, r, c0 + j))
                 for r in range(n_lhs)]
    return pl.pallas_call(
        functools.partial(_mm_kernel, n_lhs=n_lhs, epilogue=epilogue),
        out_shape=jax.ShapeDtypeStruct((m, n), out_dtype),
        grid=(m // tm, n // tn),
        in_specs=in_specs,
        out_specs=pl.BlockSpec((tm, tn), lambda i, j: (i, j)),
        compiler_params=_cparams("parallel", "arbitrary"),
        name=name,
    )(*xs, *([w] * n_lhs))


def _rope_mm_kernel(x_ref, w_ref, cos_ref, sin_ref, o_ref, *cache_refs, heads_per_tile, alternate,
                    n_nsa_kinds):
    j = pl.program_id(1)
    acc = _dot(x_ref[...], w_ref[...])
    cosv, sinv = cos_ref[...], sin_ref[...]
    if alternate:
        rot = lax.rem(j, 2) == 0
        cosv = jnp.where(rot, cosv, 1.0)
        sinv = jnp.where(rot, sinv, 0.0)
    lane = lax.broadcasted_iota(jnp.int32, cosv.shape, 1)
    heads = []
    for h in range(heads_per_tile):
        hs = acc[:, h * HEAD_DIM:(h + 1) * HEAD_DIM]
        partner = jnp.where(lane < ROT_HALF,
                            pltpu.roll(hs, HEAD_DIM - ROT_HALF, 1),
                            pltpu.roll(hs, ROT_HALF, 1))
        heads.append(hs * cosv + partner * sinv)
        o_ref[:, h * HEAD_DIM:(h + 1) * HEAD_DIM] = heads[h]
    if cache_refs:
        nsa_ref, win_ref = cache_refs

        @pl.when(j < n_nsa_kinds)
        def _():
            for h in range(heads_per_tile):
                nsa_ref[:, h, :] = heads[h]

        @pl.when(j >= n_nsa_kinds)
        def _():
            for h in range(heads_per_tile):
                win_ref[:, h, :] = heads[h]


def _rope_matmul(x, w, cos_t, sin_t, *, layer, col_start, n, tn, alternate, n_kv=None, tm=1024,
                 name):
    m, k = x.shape
    tm = _tile(m, tm, 8)
    assert col_start % tn == 0 and n % tn == 0
    c0 = col_start // tn
    hpt = tn // HEAD_DIM
    out_shape = [jax.ShapeDtypeStruct((m, n), F32)]
    out_specs = [pl.BlockSpec((tm, tn), lambda i, j: (i, j))]
    n_nsa = 4
    if alternate:
        assert hpt == n_kv and n == 6 * tn
        out_shape += [jax.ShapeDtypeStruct((m, n_nsa, n_kv, HEAD_DIM), F32),
                      jax.ShapeDtypeStruct((m, 2, n_kv, HEAD_DIM), F32)]
        out_specs += [pl.BlockSpec((tm, None, n_kv, HEAD_DIM),
                                   lambda i, j: (i, jnp.minimum(j, n_nsa - 1), 0, 0)),
                      pl.BlockSpec((tm, None, n_kv, HEAD_DIM),
                                   lambda i, j: (i, jnp.maximum(j - n_nsa, 0), 0, 0))]
    kern = functools.partial(_rope_mm_kernel, heads_per_tile=hpt, alternate=alternate,
                             n_nsa_kinds=n_nsa)
    return pl.pallas_call(
        kern,
        out_shape=tuple(out_shape),
        grid=(m // tm, n // tn),
        in_specs=[pl.BlockSpec((tm, k), lambda i, j: (i, 0)),
                  pl.BlockSpec((None, k, tn), lambda i, j: (layer, 0, c0 + j)),
                  pl.BlockSpec((tm, HEAD_DIM), lambda i, j: (i, 0)),
                  pl.BlockSpec((tm, HEAD_DIM), lambda i, j: (i, 0))],
        out_specs=tuple(out_specs),
        compiler_params=_cparams("arbitrary", "arbitrary"),
        name=name,
    )(x, w, cos_t, sin_t)


def _add_ln_kernel(x_ref, r_ref, g_ref, b_ref, y_ref, yb_ref, *, alpha):
    z = alpha * x_ref[...] + r_ref[...]
    mu = jnp.mean(z, axis=-1, keepdims=True)
    zc = z - mu
    var = jnp.mean(zc * zc, axis=-1, keepdims=True)
    y = zc * lax.rsqrt(var + LN_EPS) * g_ref[...] + b_ref[...]
    y_ref[...] = y
    yb_ref[...] = y.astype(BF16)


def _add_ln(x, r, g, b, *, alpha, name):
    m, d = x.shape
    tr = _tile(m, 256, 8)
    return pl.pallas_call(
        functools.partial(_add_ln_kernel, alpha=alpha),
        out_shape=(jax.ShapeDtypeStruct((m, d), F32), jax.ShapeDtypeStruct((m, d), BF16)),
        grid=(m // tr,),
        in_specs=[pl.BlockSpec((tr, d), lambda i: (i, 0)),
                  pl.BlockSpec((tr, d), lambda i: (i, 0)),
                  pl.BlockSpec((1, d), lambda i: (0, 0)),
                  pl.BlockSpec((1, d), lambda i: (0, 0))],
        out_specs=(pl.BlockSpec((tr, d), lambda i: (i, 0)),
                   pl.BlockSpec((tr, d), lambda i: (i, 0))),
        compiler_params=_cparams("parallel"),
        name=name,
    )(x, r, g.reshape(1, d), b.reshape(1, d))


def _sgu_kernel(uv_ref, w_ref, bt_ref, g_ref, a_ref, vn_ref, *, d_a, n_groups):
    rows = w_ref.shape[1]
    r = lax.broadcasted_iota(jnp.int32, (rows, rows), 0)
    c = lax.broadcasted_iota(jnp.int32, (rows, rows), 1)
    causal = r >= c
    for g in range(n_groups):
        lo = g * HEAD_DIM
        v = uv_ref[:, d_a + lo:d_a + lo + HEAD_DIM]
        mu = jnp.mean(v, axis=-1, keepdims=True)
        vc = v - mu
        var = jnp.mean(vc * vc, axis=-1, keepdims=True)
        vn = vc * lax.rsqrt(var + LN_EPS) * g_ref[:, lo:lo + HEAD_DIM]
        vn_ref[:, lo:lo + HEAD_DIM] = vn
        w = jnp.where(causal, w_ref[g], 0.0).astype(BF16)
        mixed = _dot(w, vn.astype(BF16)) + bt_ref[:, g:g + 1]
        a_ref[:, lo:lo + HEAD_DIM] = (uv_ref[:, lo:lo + HEAD_DIM] * mixed).astype(a_ref.dtype)


def _sgu(uv, w, bt, gain, *, rows, name):
    m = uv.shape[0]
    d_a = uv.shape[1] // 2
    n_groups = d_a // HEAD_DIM
    return pl.pallas_call(
        functools.partial(_sgu_kernel, d_a=d_a, n_groups=n_groups),
        out_shape=(jax.ShapeDtypeStruct((m, d_a), BF16), jax.ShapeDtypeStruct((m, d_a), F32)),
        grid=(m // rows,),
        in_specs=[pl.BlockSpec((rows, 2 * d_a), lambda i: (i, 0)),
                  pl.BlockSpec((n_groups, rows, rows), lambda i: (0, 0, 0)),
                  pl.BlockSpec((rows, n_groups), lambda i: (0, 0)),
                  pl.BlockSpec((1, d_a), lambda i: (0, 0))],
        out_specs=(pl.BlockSpec((rows, d_a), lambda i: (i, 0)),
                   pl.BlockSpec((rows, d_a), lambda i: (i, 0))),
        compiler_params=_cparams("parallel"),
        name=name,
    )(uv, w, bt, gain.reshape(1, d_a))


def _conv_act(gate, g1, g2, up, cw_ref, cb_ref):
    c = cb_ref[...] + g2 * cw_ref[0:1, :] + g1 * cw_ref[1:2, :] + gate * cw_ref[2:3, :]
    return _gelu(c) * up


def _ffn_in_seq_kernel(x_ref, *refs, tiles_per_seq, n_sub):
    wg, wu, cw, cb, st = (refs[k * n_sub:(k + 1) * n_sub] for k in range(5))
    a_ref, cn_ref, carry_ref = refs[5 * n_sub:]
    i = pl.program_id(1)
    x = x_ref[...]
    tm = x.shape[0]
    tn = wg[0].shape[1]
    first = lax.rem(i, tiles_per_seq) == 0
    row = lax.broadcasted_iota(jnp.int32, (tm, tn), 0)
    for k in range(n_sub):
        cols = slice(k * tn, (k + 1) * tn)
        gate = _dot(x, wg[k][...])
        up = _dot(x, wu[k][...])
        prev2 = jnp.where(first, st[k][0:1, :], carry_ref[0:1, cols])
        prev1 = jnp.where(first, st[k][1:2, :], carry_ref[1:2, cols])
        g1 = jnp.where(row == 0, prev1, pltpu.roll(gate, 1, 0))
        g2 = jnp.where(row == 0, prev2, jnp.where(row == 1, prev1, pltpu.roll(gate, 2, 0)))
        a_ref[:, cols] = _conv_act(gate, g1, g2, up, cw[k], cb[k]).astype(a_ref.dtype)
        tail = gate[tm - 2:tm, :]
        carry_ref[0:2, cols] = tail
        cn_ref[:, cols] = tail


def _ffn_in_seq(x, w, cw, cb, state, *, layer, seq_len, tm=1024, tn=256, n_sub=2, name):
    m, k = x.shape
    d_ff = cw.shape[2]
    tm = _tile(seq_len, tm, 8)
    tn = _tile(d_ff, tn)
    n_tiles = d_ff // tn
    tps = seq_len // tm
    n_seq = m // seq_len

    def col(j, s):
        return jnp.minimum(j * n_sub + s, n_tiles - 1)

    subs = range(n_sub)
    in_specs = [pl.BlockSpec((tm, k), lambda j, i: (i, 0))]
    in_specs += [pl.BlockSpec((None, k, tn), lambda j, i, s=s: (layer, 0, col(j, s))) for s in subs]
    in_specs += [pl.BlockSpec((None, k, tn), lambda j, i, s=s: (layer, 0, n_tiles + col(j, s)))
                 for s in subs]
    in_specs += [pl.BlockSpec((None, CONV_W, tn), lambda j, i, s=s: (layer, 0, col(j, s))) for s in subs]
    in_specs += [pl.BlockSpec((None, 1, tn), lambda j, i, s=s: (layer, 0, col(j, s))) for s in subs]
    in_specs += [pl.BlockSpec((None, CONV_W - 1, tn), lambda j, i, s=s: (i // tps, 0, col(j, s)))
                 for s in subs]
    return pl.pallas_call(
        functools.partial(_ffn_in_seq_kernel, tiles_per_seq=tps, n_sub=n_sub),
        out_shape=(jax.ShapeDtypeStruct((m, d_ff), BF16),
                   jax.ShapeDtypeStruct((n_seq, CONV_W - 1, d_ff), F32)),
        grid=(pl.cdiv(n_tiles, n_sub), m // tm),
        in_specs=in_specs,
        out_specs=(pl.BlockSpec((tm, n_sub * tn), lambda j, i: (i, j)),
                   pl.BlockSpec((None, CONV_W - 1, n_sub * tn), lambda j, i: (i // tps, 0, j))),
        scratch_shapes=[pltpu.VMEM((8, n_sub * tn), F32)],
        compiler_params=_cparams("arbitrary", "arbitrary"),
        name=name,
    )(x, *([w] * (2 * n_sub)), *([cw] * n_sub), *([cb] * n_sub), *([state] * n_sub))


def _ffn_in_short_kernel(x_ref, wg_ref, wu_ref, cw_ref, cb_ref, h1_ref, h2_ref, a_ref, gate_ref,
                         *, seq_len):
    x = x_ref[...]
    gate = _dot(x, wg_ref[...])
    up = _dot(x, wu_ref[...])
    t = lax.rem(lax.broadcasted_iota(jnp.int32, gate.shape, 0), seq_len)
    g1 = jnp.where(t >= 1, pltpu.roll(gate, 1, 0), h1_ref[...])
    g2 = jnp.where(t >= 2, pltpu.roll(gate, 2, 0), h2_ref[...])
    a_ref[...] = _conv_act(gate, g1, g2, up, cw_ref, cb_ref).astype(a_ref.dtype)
    gate_ref[...] = gate


def _ffn_in_short(x, w, cw, cb, h1, h2, *, layer, seq_len, tn=256, name):
    m, k = x.shape
    d_ff = cw.shape[2]
    tn = _tile(d_ff, tn)
    n_tiles = d_ff // tn
    return pl.pallas_call(
        functools.partial(_ffn_in_short_kernel, seq_len=seq_len),
        out_shape=(jax.ShapeDtypeStruct((m, d_ff), BF16), jax.ShapeDtypeStruct((m, d_ff), F32)),
        grid=(n_tiles,),
        in_specs=[pl.BlockSpec((m, k), lambda j: (0, 0)),
                  pl.BlockSpec((None, k, tn), lambda j: (layer, 0, j)),
                  pl.BlockSpec((None, k, tn), lambda j: (layer, 0, n_tiles + j)),
                  pl.BlockSpec((None, CONV_W, tn), lambda j: (layer, 0, j)),
                  pl.BlockSpec((None, 1, tn), lambda j: (layer, 0, j)),
                  pl.BlockSpec((m, tn), lambda j: (0, j)),
                  pl.BlockSpec((m, tn), lambda j: (0, j))],
        out_specs=(pl.BlockSpec((m, tn), lambda j: (0, j)),
                   pl.BlockSpec((m, tn), lambda j: (0, j))),
        compiler_params=_cparams("parallel"),
        name=name,
    )(x, w, w, cw, cb, h1, h2)


def _compress_rows(xa, w1a, w1b, w2, hpe):
    a = _dot(xa, w1a)
    b = _dot(xa, w1b)
    n = a.shape[0]
    h = a + pltpu.roll(b, n - 1, 0) + hpe
    return _dot(_gelu(h).astype(BF16), w2)


def _pos_embed_term(pe_ref, w1_ref, kind):
    return _dot(pe_ref[kind], w1_ref[kind])[0:1, :]


def _compress_p_kernel(k_ref, v_ref, w1_ref, w2_ref, pe_ref, kc_ref, vc_ref, *, n_chunks):
    half = STRIDE * HEAD_DIM
    for kind, (src, dst) in enumerate(((k_ref, kc_ref), (v_ref, vc_ref))):
        xa = jnp.concatenate(
            [src[pl.ds(s, n_chunks, stride=STRIDE), :] for s in range(STRIDE)], axis=1).astype(BF16)
        hpe = _pos_embed_term(pe_ref, w1_ref, kind)
        out = _compress_rows(xa, w1_ref[kind, 0:half, :], w1_ref[kind, half:2 * half, :],
                             w2_ref[kind], hpe)
        dst[...] = out.astype(dst.dtype)


def _compress_prompt(kv, w1, w2, pe, *, batch, seq, n_kv, name):
    n_chunks = seq // STRIDE
    out = jax.ShapeDtypeStruct((batch, n_kv, n_chunks, HEAD_DIM), BF16)
    ospec = pl.BlockSpec((None, None, n_chunks, HEAD_DIM), lambda b, h: (b, h, 0, 0))
    return pl.pallas_call(
        functools.partial(_compress_p_kernel, n_chunks=n_chunks),
        out_shape=(out, out),
        grid=(batch, n_kv),
        in_specs=[pl.BlockSpec((seq, HEAD_DIM), lambda b, h: (b, h)),
                  pl.BlockSpec((seq, HEAD_DIM), lambda b, h: (b, n_kv + h)),
                  pl.BlockSpec(w1.shape, lambda b, h: (0, 0, 0)),
                  pl.BlockSpec(w2.shape, lambda b, h: (0, 0, 0)),
                  pl.BlockSpec(pe.shape, lambda b, h: (0, 0, 0))],
        out_specs=(ospec, ospec),
        compiler_params=_cparams("parallel", "parallel"),
        name=name,
    )(kv, kv, w1, w2, pe)


def _block_scores(imp, pos, n_blocks, block_axis):
    j = lax.broadcasted_iota(jnp.int32, imp.shape, block_axis)
    cur = pos // L_SLC
    forced = (j == 0) | (j == cur) | (j == cur - 1)
    valid = j * L_SLC <= pos
    score = jnp.where(valid, jnp.where(forced, SEL_FORCE, imp), -SEL_FORCE)
    return jnp.where(j < n_blocks, score, NEG_BIG)


def _overlap_matrix(nc, nb, rows, cols):
    i = np.arange(nc)[:, None]
    j = np.arange(nb)[None, :]
    lo = np.maximum(i * STRIDE, j * L_SLC)
    hi = np.minimum(i * STRIDE + L_CMP, (j + 1) * L_SLC)
    ov = np.zeros((rows, cols), np.float32)
    ov[:nc, :nb] = np.maximum(hi - lo, 0) / STRIDE
    return ov


def _block_to_key_matrix(n_groups, lanes, keys_per_group):
    ex = np.zeros((n_groups, lanes, keys_per_group), np.float32)
    for c in range(n_groups):
        k = np.arange(keys_per_group)
        ex[c, (c * keys_per_group + k) // L_SLC, k] = 1.0
    return ex


def _attn_p_kernel(q_ref, kc_ref, vc_ref, ks_ref, vs_ref, kw_ref, vw_ref, gt_ref, ovt_ref, ex_ref,
                   o_ref, m_scr, l_scr, acc_scr, *, tq, tk, gqa, n_blocks, n_cmp, win_keys):
    qi = pl.program_id(2)
    t0 = qi * tq
    q = q_ref[...] * LOGIT_SCALE
    qs = jnp.concatenate([q[:, g * HEAD_DIM:(g + 1) * HEAD_DIM] for g in range(gqa)],
                         axis=0).astype(BF16)
    pos_t = t0 + lax.broadcasted_iota(jnp.int32, (tq, 1), 0)

    def tile_rows(x):
        return jnp.concatenate([x] * gqa, axis=0)

    s = _dot_nt(qs, kc_ref[...])
    n = lax.broadcasted_iota(jnp.int32, (tq, s.shape[1]), 1)
    last_end = jnp.minimum(pos_t, (n_cmp - 1) * STRIDE + (L_CMP - 1))
    sm = s + tile_rows(jnp.where(n * STRIDE + (L_CMP - 1) <= last_end, 0.0, NEG_BIG))
    m = jnp.max(sm, axis=-1, keepdims=True)
    e = jnp.exp2(sm - m)
    norm = jnp.where(m > 0.5 * NEG_BIG, 1.0 / jnp.maximum(jnp.sum(e, axis=-1, keepdims=True), 1e-30), 0.0)
    p = e * norm
    o_c = _dot(p.astype(BF16), vc_ref[...])
    p_grp = p[0:tq]
    for g in range(1, gqa):
        p_grp = p_grp + p[g * tq:(g + 1) * tq]

    hi, lo = _split_hi_lo(p_grp)
    imp_t = _dot_nt(ovt_ref[...], hi) + _dot_nt(ovt_ref[...], lo)
    pos_row = t0 + lax.broadcasted_iota(jnp.int32, (1, tq), 1)
    score = _block_scores(imp_t, pos_row, n_blocks, 0)
    blk = lax.broadcasted_iota(jnp.int32, score.shape, 0)
    rank = jnp.zeros(score.shape, F32)
    for i in range(n_blocks):
        ci = score[i:i + 1, :]
        rank = rank + jnp.where((ci > score) | ((ci == score) & (blk > i)), 1.0, 0.0)
    sel_t = jnp.where(rank < float(min(N_SEL, n_blocks)), 1.0, 0.0)
    lanes = ex_ref.shape[1]
    sel_t = jnp.concatenate([sel_t, jnp.zeros((lanes - sel_t.shape[0], tq), F32)], axis=0)
    sel = sel_t.T.astype(BF16)

    m_scr[...] = jnp.full(m_scr.shape, NEG_BIG, F32)
    l_scr[...] = jnp.zeros(l_scr.shape, F32)
    acc_scr[...] = jnp.zeros(acc_scr.shape, F32)

    def key_tile(c, carry):
        k0 = pl.multiple_of(c * tk, tk)
        s = _dot_nt(qs, ks_ref[pl.ds(k0, tk), :].astype(BF16))
        sel_keys = _dot(sel, ex_ref[c])
        kpos = k0 + lax.broadcasted_iota(jnp.int32, sel_keys.shape, 1)
        bias = jnp.where(sel_keys > jnp.where(kpos <= pos_t, 0.5, 2.0), 0.0, NEG_BIG)
        sm = s + tile_rows(bias)
        m_old = m_scr[...]
        m_new = jnp.maximum(m_old, jnp.max(sm, axis=-1, keepdims=True))
        alpha = jnp.exp2(m_old - m_new)
        e = jnp.exp2(sm - m_new)
        l_scr[...] = alpha * l_scr[...] + jnp.sum(e, axis=-1, keepdims=True)
        acc_scr[...] = alpha * acc_scr[...] + _dot(e.astype(BF16),
                                                   vs_ref[pl.ds(k0, tk), :].astype(BF16))
        m_scr[...] = m_new
        return carry

    lax.fori_loop(0, (t0 + tq - 1) // tk + 1, key_tile, 0)
    o_s = acc_scr[...] / jnp.maximum(l_scr[...], 1e-30)

    start = pl.multiple_of(jnp.maximum(t0 + tq - win_keys, 0), 128)
    s = _dot_nt(qs, kw_ref[pl.ds(start, win_keys), :].astype(BF16))
    d = pos_t - (start + lax.broadcasted_iota(jnp.int32, (tq, win_keys), 1))
    bias = jnp.where(d >= 0, jnp.where(d < WINDOW, 0.0, NEG_BIG), NEG_BIG)
    sm = s + tile_rows(bias)
    e = jnp.exp2(sm - jnp.max(sm, axis=-1, keepdims=True))
    o_w = (_dot(e.astype(BF16), vw_ref[pl.ds(start, win_keys), :].astype(BF16))
           / jnp.sum(e, axis=-1, keepdims=True))

    gt = gt_ref[...]
    for g in range(gqa):
        r0 = g * tq
        out = (gt[:, g:g + 1] * o_c[r0:r0 + tq]
               + gt[:, gqa + g:gqa + g + 1] * o_s[r0:r0 + tq]
               + gt[:, 2 * gqa + g:2 * gqa + g + 1] * o_w[r0:r0 + tq])
        o_ref[:, g * HEAD_DIM:(g + 1) * HEAD_DIM] = out.astype(o_ref.dtype)


def _attn_prompt(q2d, kv, kcmp, vcmp, gates_h, *, batch, seq, n_heads, n_kv, tq=256, tk=512, name):
    gqa = n_heads // n_kv
    tq = _tile(seq, tq, 8)
    tk = _tile(seq, tk)
    nq = seq // tq
    n_chunks = seq // STRIDE
    n_cmp = n_chunks - 1
    n_blocks = -(-seq // L_SLC)
    lanes = -(-n_blocks // 128) * 128
    block_rows = -(-n_blocks // 8) * 8
    ovt = jnp.asarray(_overlap_matrix(n_cmp, n_blocks, n_chunks, block_rows).T.copy(), BF16)
    ex = jnp.asarray(_block_to_key_matrix(seq // tk, lanes, tk), BF16)
    win_keys = min(WINDOW + tq, seq)
    rows = gqa * tq
    kern = functools.partial(_attn_p_kernel, tq=tq, tk=tk, gqa=gqa, n_blocks=n_blocks,
                             n_cmp=n_cmp, win_keys=win_keys)

    def kv_spec(kind):
        return pl.BlockSpec((seq, HEAD_DIM), lambda b, h, i: (b, kind * n_kv + h))

    cmp_spec = pl.BlockSpec((None, None, n_chunks, HEAD_DIM), lambda b, h, i: (b, h, 0, 0))
    return pl.pallas_call(
        kern,
        out_shape=jax.ShapeDtypeStruct((batch * seq, n_heads * HEAD_DIM), BF16),
        grid=(batch, n_kv, nq),
        in_specs=[pl.BlockSpec((tq, gqa * HEAD_DIM), lambda b, h, i: (b * nq + i, h)),
                  cmp_spec, cmp_spec,
                  kv_spec(2), kv_spec(3), kv_spec(4), kv_spec(5),
                  pl.BlockSpec((None, tq, 3 * gqa), lambda b, h, i: (h, b * nq + i, 0)),
                  pl.BlockSpec(ovt.shape, lambda b, h, i: (0, 0)),
                  pl.BlockSpec(ex.shape, lambda b, h, i: (0, 0, 0))],
        out_specs=pl.BlockSpec((tq, gqa * HEAD_DIM), lambda b, h, i: (b * nq + i, h)),
        scratch_shapes=[pltpu.VMEM((rows, 1), F32),
                        pltpu.VMEM((rows, 1), F32),
                        pltpu.VMEM((rows, HEAD_DIM), F32)],
        compiler_params=_cparams("parallel", "parallel", "arbitrary"),
        name=name,
    )(q2d, kcmp, vcmp, kv, kv, kv, kv, gates_h, ovt, ex)


def _head_slabs(rows_ref, first, n_heads):
    return jnp.concatenate([rows_ref[:, first + h, :] for h in range(n_heads)], axis=1)


def _cmp_s_kernel(pt_ref, *refs, n_kv, n_steps, pages_per_step, steps_per_group, n_cmp, n_blocks,
                  dec_seq, gqa, past):
    del pt_ref
    page_refs = refs[:pages_per_step]
    q_ref, w1_ref, w2_ref, pe_ref, ovt_ref, oc_ref, sel_ref, x_scr, ab_scr = refs[pages_per_step:]
    p = pl.program_id(1)
    n_kh = 2 * n_kv
    page = page_refs[0].shape[0]
    cpp = page // STRIDE
    step_chunks = cpp * pages_per_step
    group_chunks = step_chunks * steps_per_group
    c0 = pl.multiple_of(lax.rem(p, steps_per_group) * step_chunks, 8)
    for k, page_ref in enumerate(page_refs):
        for s in range(STRIDE):
            for kh in range(n_kh):
                x_scr[kh, pl.ds(c0 + k * cpp, cpp), s * HEAD_DIM:(s + 1) * HEAD_DIM] = (
                    page_ref[pl.ds(s, cpp, stride=STRIDE), kh, :])

    @pl.when(lax.rem(p, steps_per_group) == steps_per_group - 1)
    def _():
        g0 = pl.multiple_of((p // steps_per_group) * group_chunks, 8)
        for kh in range(n_kh):
            kind = kh // n_kv
            ab_scr[kh, pl.ds(g0, group_chunks), :] = _dot(x_scr[kh].astype(BF16), w1_ref[kind])

    @pl.when(p == n_steps - 1)
    def _():
        n_chunks = ab_scr.shape[1]
        comp = []
        for kh in range(n_kh):
            kind = kh // n_kv
            hpe = _dot(pe_ref[kind], w1_ref[kind])
            hpe = hpe[0:1, 0:HEAD_DIM] + hpe[8:9, HEAD_DIM:2 * HEAD_DIM]
            ab = ab_scr[kh]
            h = ab[:, 0:HEAD_DIM] + pltpu.roll(ab[:, HEAD_DIM:2 * HEAD_DIM], n_chunks - 1, 0) + hpe
            comp.append(_dot(_gelu(h).astype(BF16), w2_ref[kind]).astype(BF16))
        qrows = lax.broadcasted_iota(jnp.int32, (1, HEAD_DIM), 1)
        pos = past + lax.rem(qrows, dec_seq)
        for h in range(n_kv):
            kc, vc = comp[h], comp[n_kv + h]
            st = _dot_nt(kc, (q_ref[h] * LOGIT_SCALE).astype(BF16))
            n = lax.broadcasted_iota(jnp.int32, st.shape, 0)
            pt = _masked_softmax2(st, (n * STRIDE + (L_CMP - 1) <= pos) & (n < n_cmp), axis=0)
            oc_ref[h] = _dot(pt.T.astype(BF16), vc)
            pg = pt
            for g in range(1, gqa):
                pg = pg + pltpu.roll(pt, HEAD_DIM - g * dec_seq, 1)
            hi, lo = _split_hi_lo(pg)
            imp_t = _dot(ovt_ref[...], hi) + _dot(ovt_ref[...], lo)
            imp = imp_t.T
            tpos = past + lax.broadcasted_iota(jnp.int32, (imp.shape[0], 1), 0)
            score = _block_scores(imp, tpos, n_blocks, 1)
            score_t = score.T
            nb_pad = score.shape[1]
            ii = lax.broadcasted_iota(jnp.int32, (nb_pad, nb_pad), 0)
            jj = lax.broadcasted_iota(jnp.int32, (nb_pad, nb_pad), 1)
            for t in range(dec_seq):
                col = score_t[:, t:t + 1]
                rowv = score[t:t + 1, :]
                beats = (col > rowv) | ((col == rowv) & (ii < jj))
                rank = jnp.sum(jnp.where(beats, 1.0, 0.0), axis=0, keepdims=True)
                sel_ref[h, t:t + 1, :] = jnp.where(rank < float(min(N_SEL, n_blocks)), 1.0, 0.0)
            sel_ref[h, dec_seq:, :] = jnp.zeros((sel_ref.shape[1] - dec_seq, nb_pad), F32)


def _page_specs(page, rows, row_block, layer, pages_per_step):
    def spec(k):
        return pl.BlockSpec((None, None, page, rows, HEAD_DIM),
                            lambda b, p, pt: (layer, pt[b, p * pages_per_step + k], 0, row_block, 0))
    return [spec(k) for k in range(pages_per_step)]


def _cmp_sample(page_table, cache_rows, q_pad, w1cat, w2, pe, *, layer, n_kv, gqa, dec_seq, name):
    batch, n_pages = page_table.shape
    page = cache_rows.shape[2]
    past = n_pages * page
    cpp = page // STRIDE
    n_chunks = past // STRIDE
    assert dec_seq < STRIDE and n_chunks % 8 == 0
    n_cmp = (past + dec_seq) // STRIDE - 1
    n_blocks = -(-(past + dec_seq) // L_SLC)
    nb_pad = -(-n_blocks // 128) * 128
    pps = _tile(n_pages, 4, 1)
    n_steps = n_pages // pps
    spg = _tile(n_steps, max(1, 128 // (cpp * pps)), 1)
    n_kh = 2 * n_kv
    ovt = jnp.asarray(_overlap_matrix(n_cmp, n_blocks, n_chunks, nb_pad).T.copy(), BF16)
    kern = functools.partial(_cmp_s_kernel, n_kv=n_kv, n_steps=n_steps, pages_per_step=pps,
                             steps_per_group=spg, n_cmp=n_cmp, n_blocks=n_blocks, dec_seq=dec_seq,
                             gqa=gqa, past=past)
    grid_spec = pltpu.PrefetchScalarGridSpec(
        num_scalar_prefetch=1,
        grid=(batch, n_steps),
        in_specs=_page_specs(page, n_kh, 0, layer, pps) + [
            pl.BlockSpec((None, n_kv, 128, HEAD_DIM), lambda b, p, pt: (b, 0, 0, 0)),
            pl.BlockSpec(w1cat.shape, lambda b, p, pt: (0, 0, 0)),
            pl.BlockSpec(w2.shape, lambda b, p, pt: (0, 0, 0)),
            pl.BlockSpec(pe.shape, lambda b, p, pt: (0, 0, 0)),
            pl.BlockSpec(ovt.shape, lambda b, p, pt: (0, 0))],
        out_specs=(pl.BlockSpec((None, n_kv, 128, HEAD_DIM), lambda b, p, pt: (b, 0, 0, 0)),
                   pl.BlockSpec((None, n_kv, 8, nb_pad), lambda b, p, pt: (b, 0, 0, 0))),
        scratch_shapes=[pltpu.VMEM((n_kh, spg * pps * cpp, STRIDE * HEAD_DIM), F32),
                        pltpu.VMEM((n_kh, n_chunks, 2 * HEAD_DIM), F32)],
    )
    return pl.pallas_call(
        kern,
        out_shape=(jax.ShapeDtypeStruct((batch, n_kv, 128, HEAD_DIM), F32),
                   jax.ShapeDtypeStruct((batch, n_kv, 8, nb_pad), F32)),
        grid_spec=grid_spec,
        compiler_params=_cparams("parallel", "arbitrary"),
        name=name,
    )(page_table, *([cache_rows] * pps), q_pad, w1cat, w2, pe, ovt)


def _slc_s_kernel(pt_ref, *refs, n_kv, n_steps, pages_per_step, rows_per_head, dec_seq, past, w_buf):
    del pt_ref
    page_refs = refs[:pages_per_step]
    (q_ref, sel_ref, ex_ref, kn_ref, vn_ref, win_ref, kwn_ref, vwn_ref, oc_ref, gt_ref, o_ref,
     qbd_scr, m_scr, l_scr, acc_scr) = refs[pages_per_step:]
    p = pl.program_id(1)
    rows = n_kv * rows_per_head

    @pl.when(p == 0)
    def _():
        qbd_scr[...] = jnp.zeros(qbd_scr.shape, qbd_scr.dtype)
        for h in range(n_kv):
            qbd_scr[h * rows_per_head:(h + 1) * rows_per_head, h * HEAD_DIM:(h + 1) * HEAD_DIM] = (
                q_ref[h, 0:rows_per_head, :] * LOGIT_SCALE)
        m_scr[...] = jnp.full(m_scr.shape, NEG_BIG, F32)
        l_scr[...] = jnp.zeros(l_scr.shape, F32)
        acc_scr[...] = jnp.zeros(acc_scr.shape, F32)

    qbd = qbd_scr[...].astype(BF16)

    def online_update(s, mask, v):
        sm = jnp.where(mask, s, NEG_BIG)
        m_old = m_scr[...]
        m_new = jnp.maximum(m_old, jnp.max(sm, axis=-1, keepdims=True))
        alpha = jnp.exp2(m_old - m_new)
        e = jnp.where(mask, jnp.exp2(sm - m_new), 0.0)
        l_scr[...] = alpha * l_scr[...] + jnp.sum(e, axis=-1, keepdims=True)
        acc_scr[...] = alpha * acc_scr[...] + _dot(e.astype(BF16), v)
        m_scr[...] = m_new

    ks, vs = [], []
    for page_ref in page_refs:
        ks.append(_head_slabs(page_ref, 0, n_kv))
        vs.append(_head_slabs(page_ref, n_kv, n_kv))
    k_all = jnp.concatenate(ks, axis=0).astype(BF16)
    v_all = jnp.concatenate(vs, axis=0).astype(BF16)
    s = _dot_nt(qbd, k_all)
    sel_keys = _dot(sel_ref[...].astype(BF16), ex_ref[...])
    online_update(s, sel_keys > 0.5, v_all)

    @pl.when(p == n_steps - 1)
    def _():
        r = lax.broadcasted_iota(jnp.int32, (rows, 1), 0)
        t = lax.rem(r, dec_seq)
        s = _dot_nt(qbd, kn_ref[...].astype(BF16))
        j = lax.broadcasted_iota(jnp.int32, s.shape, 1)
        online_update(s, (j <= t) & (j < dec_seq), vn_ref[...].astype(BF16))
        o_s = acc_scr[...] / jnp.maximum(l_scr[...], 1e-30)

        sb = _dot_nt(qbd, _head_slabs(win_ref, 0, n_kv).astype(BF16))
        sn = _dot_nt(qbd, kwn_ref[...].astype(BF16))
        ib = lax.broadcasted_iota(jnp.int32, sb.shape, 1)
        kpos = past - w_buf + ib
        d = (past + t) - kpos
        mb = (d >= 0) & (d < WINDOW) & (kpos >= 0)
        jn = lax.broadcasted_iota(jnp.int32, sn.shape, 1)
        mn = (jn <= t) & (jn < dec_seq) & (t - jn < WINDOW)
        smb = jnp.where(mb, sb, NEG_BIG)
        smn = jnp.where(mn, sn, NEG_BIG)
        mx = jnp.maximum(jnp.max(smb, axis=-1, keepdims=True), jnp.max(smn, axis=-1, keepdims=True))
        eb = jnp.where(mb, jnp.exp2(smb - mx), 0.0)
        en = jnp.where(mn, jnp.exp2(smn - mx), 0.0)
        den = jnp.maximum(jnp.sum(eb, axis=-1, keepdims=True) + jnp.sum(en, axis=-1, keepdims=True),
                          1e-30)
        o_w = (_dot((eb / den).astype(BF16), _head_slabs(win_ref, n_kv, n_kv).astype(BF16))
               + _dot((en / den).astype(BF16), vwn_ref[...].astype(BF16)))

        for h in range(n_kv):
            r0 = h * rows_per_head
            c0 = h * HEAD_DIM
            gt = gt_ref[h]
            o_ref[h] = (gt[:, 0:1] * oc_ref[h, 0:rows_per_head, :]
                        + gt[:, 1:2] * o_s[r0:r0 + rows_per_head, c0:c0 + HEAD_DIM]
                        + gt[:, 2:3] * o_w[r0:r0 + rows_per_head, c0:c0 + HEAD_DIM])


def _slc_sample(page_table, cache_rows, q_pad, sel_steps, k_new, v_new, win_rows, kw_new, vw_new, o_c,
                gates, *, layer, n_kv, gqa, dec_seq, pages_per_step, name):
    batch, n_pages = page_table.shape
    page = cache_rows.shape[2]
    past = n_pages * page
    width = n_kv * HEAD_DIM
    n_kh = 2 * n_kv
    rph = gqa * dec_seq
    rows = n_kv * rph
    w_buf = win_rows.shape[2]
    n_new = k_new.shape[1]
    pps = pages_per_step
    n_steps = n_pages // pps
    ex = jnp.asarray(_block_to_key_matrix(1, sel_steps.shape[3], pps * page)[0], BF16)
    kern = functools.partial(_slc_s_kernel, n_kv=n_kv, n_steps=n_steps, pages_per_step=pps,
                             rows_per_head=rph, dec_seq=dec_seq, past=past, w_buf=w_buf)

    def new_spec():
        return pl.BlockSpec((None, n_new, width), lambda b, p, pt: (b, 0, 0))

    grid_spec = pltpu.PrefetchScalarGridSpec(
        num_scalar_prefetch=1,
        grid=(batch, n_steps),
        in_specs=_page_specs(page, n_kh, 1, layer, pps) + [
            pl.BlockSpec((None, n_kv, 128, HEAD_DIM), lambda b, p, pt: (b, 0, 0, 0)),
            pl.BlockSpec((None, None, rows, sel_steps.shape[3]), lambda b, p, pt: (b, p, 0, 0)),
            pl.BlockSpec(ex.shape, lambda b, p, pt: (0, 0)),
            new_spec(), new_spec(),
            pl.BlockSpec((None, None, w_buf, n_kh, HEAD_DIM), lambda b, p, pt: (layer, b, 0, 0, 0)),
            new_spec(), new_spec(),
            pl.BlockSpec((None, n_kv, 128, HEAD_DIM), lambda b, p, pt: (b, 0, 0, 0)),
            pl.BlockSpec((None, n_kv, rph, 8), lambda b, p, pt: (b, 0, 0, 0))],
        out_specs=pl.BlockSpec((None, n_kv, rph, HEAD_DIM), lambda b, p, pt: (b, 0, 0, 0)),
        scratch_shapes=[pltpu.VMEM((rows, width), F32),
                        pltpu.VMEM((rows, 1), F32),
                        pltpu.VMEM((rows, 1), F32),
                        pltpu.VMEM((rows, width), F32)],
    )
    return pl.pallas_call(
        kern,
        out_shape=jax.ShapeDtypeStruct((batch, n_kv, rph, HEAD_DIM), F32),
        grid_spec=grid_spec,
        compiler_params=_cparams("parallel", "arbitrary"),
        name=name,
    )(page_table, *([cache_rows] * pps), q_pad, sel_steps, ex, k_new, v_new, win_rows, kw_new,
      vw_new, o_c, gates)


def _rope_tables(pos):
    inv = ROPE_THETA ** (-jnp.arange(ROT_HALF, dtype=F32) * 2.0 / ROT_DIM)
    ang = pos.astype(F32)[:, None] * inv[None, :]
    cos, sin = jnp.cos(ang), jnp.sin(ang)
    rest = HEAD_DIM - ROT_DIM
    cos_t = jnp.concatenate([cos, cos, jnp.ones((pos.shape[0], rest), F32)], axis=1)
    sin_t = jnp.concatenate([-sin, sin, jnp.zeros((pos.shape[0], rest), F32)], axis=1)
    return cos_t, sin_t


def _prep_weights(w_in, w_o, ffn_w_in, ffn_w_down, ffn_conv_b, n_gate):
    w_gate = jnp.pad(w_in[:, :, w_in.shape[2] - n_gate:], ((0, 0), (0, 0), (0, 128 - n_gate)))
    return dict(w_in=w_in.astype(BF16), w_gate=w_gate.astype(BF16), w_o=w_o.astype(BF16),
                f_in=ffn_w_in.astype(BF16),
                f_down=ffn_w_down.astype(BF16),
                f_cb=ffn_conv_b.reshape(ffn_conv_b.shape[0], 1, ffn_conv_b.shape[1]))


def _prep_compress(l, cmp_pe, cmp_w1, cmp_w2):
    half = STRIDE * HEAD_DIM
    w1 = cmp_w1[l].reshape(2, 2 * half, HEAD_DIM)
    wl = {}
    wl["cmp_w1"] = w1.astype(BF16)
    wl["cmp_w1cat"] = jnp.concatenate([w1[:, :half], w1[:, half:]], axis=2).astype(BF16)
    wl["cmp_w2"] = cmp_w2[l].astype(BF16)
    pe = cmp_pe[l].reshape(2, 1, 2 * half)
    wl["cmp_pe"] = jnp.broadcast_to(pe, (2, 16, 2 * half)).astype(BF16)
    pe2 = cmp_pe[l].reshape(2, 2, 1, half)
    wl["cmp_pecat"] = jnp.concatenate([jnp.broadcast_to(pe2[:, 0], (2, 8, half)),
                                       jnp.broadcast_to(pe2[:, 1], (2, 8, half))], axis=1).astype(BF16)
    return wl


def _project(x_bf, ws, l, cos_t, sin_t, dims, tag):
    d_a, d_b, d_kv, n_heads, d_ff = dims
    n_kv = d_kv // HEAD_DIM
    w = ws["w_in"]
    uv = _matmul([x_bf], w, layer=l, col_start=0, n=2 * d_a, epilogue="gelu", tm=1024, tn=1024,
                 name=f"proj_uv_{tag}")
    (q2d,) = _rope_matmul(x_bf, w, cos_t, sin_t, layer=l, col_start=2 * d_a, n=d_b, tn=d_kv,
                          alternate=False, name=f"proj_q_{tag}")
    kv, nsa4, win4 = _rope_matmul(x_bf, w, cos_t, sin_t, layer=l, col_start=2 * d_a + d_b, n=6 * d_kv,
                                  tn=d_kv, alternate=True, n_kv=n_kv, name=f"proj_kv_{tag}")
    gates = _matmul([x_bf], ws["w_gate"], layer=l, epilogue="sigmoid", tm=1024, tn=128,
                    name=f"proj_gate_{tag}")
    return uv, q2d, kv, nsa4, win4, gates


def _mix_and_norm(x, a_out, b_out, ws, l, ln_g, ln_b, alpha, tag):
    mix = _matmul([a_out, b_out], ws["w_o"], layer=l, tm=1024, tn=1024, name=f"w_o_{tag}")
    return _add_ln(x, mix, ln_g[l, 0], ln_b[l, 0], alpha=alpha, name=f"ln1_{tag}")


def _layer_prompt(x, x_bf, ws, wl, l, p, cos_t, sin_t, dims, batch, seq, n_kv, alpha):
    d_a, d_b, d_kv, n_heads, d_ff = dims
    gqa = n_heads // n_kv
    m = batch * seq
    uv, q2d, kv, nsa4, win4, gates = _project(x_bf, ws, l, cos_t, sin_t, dims, "p")
    a_out, _ = _sgu(uv, p["sgu_w"][l], p["sgu_b"][l].T, p["sgu_g"][l], rows=CHUNK, name="sgu_p")
    kcmp, vcmp = _compress_prompt(kv, wl["cmp_w1"], wl["cmp_w2"], wl["cmp_pe"], batch=batch, seq=seq,
                                  n_kv=n_kv, name="compress_p")
    gates_h = gates[:, :3 * n_heads].reshape(m, 3, n_kv, gqa).transpose(2, 0, 1, 3).reshape(n_kv, m, 3 * gqa)
    b_out = _attn_prompt(q2d, kv, kcmp, vcmp, gates_h, batch=batch, seq=seq, n_heads=n_heads, n_kv=n_kv,
                         name="nsa_p")
    x1, x1b = _mix_and_norm(x, a_out, b_out, ws, l, p["ln_g"], p["ln_b"], alpha, "p")
    state0 = jnp.zeros((batch, CONV_W - 1, d_ff), F32)
    act, conv_new = _ffn_in_seq(x1b, ws["f_in"], p["ffn_conv_w"], ws["f_cb"], state0, layer=l,
                                seq_len=seq, name="ffn_in_p")
    f = _matmul([act], ws["f_down"], layer=l, tm=512, tn=256, name="ffn_down_p")
    x2, x2b = _add_ln(x1, f, p["ln_g"][l, 1], p["ln_b"][l, 1], alpha=alpha, name="ln2_p")
    new_nsa = nsa4.reshape(batch, seq, 4, n_kv, HEAD_DIM)
    new_win = win4.reshape(batch, seq, 2, n_kv, HEAD_DIM)[:, -min(WINDOW, seq):]
    return x2, x2b, new_nsa, new_win, conv_new


def _layer_sample(x, x_bf, ws, wl, l, p, cos_t, sin_t, dims, batch, dec_seq, n_kv, alpha, page_table,
                  cache_rows, win_rows, conv_state):
    d_a, d_b, d_kv, n_heads, d_ff = dims
    gqa = n_heads // n_kv
    m = batch * dec_seq
    rph = gqa * dec_seq
    uv, q2d, kv, nsa4, win4, gates = _project(x_bf, ws, l, cos_t, sin_t, dims, "s")
    w_small = p["sgu_w"][l][:, :dec_seq, :dec_seq]
    eye = jnp.eye(batch, dtype=F32)
    w_bd = jnp.einsum("ab,gts->gatbs", eye, w_small).reshape(-1, m, m)
    bt_bd = jnp.tile(p["sgu_b"][l].T[:dec_seq], (batch, 1))
    a_out, v_rows = _sgu(uv, w_bd, bt_bd, p["sgu_g"][l], rows=m, name="sgu_s")

    q = q2d.reshape(batch, dec_seq, n_kv, gqa, HEAD_DIM).transpose(0, 2, 3, 1, 4)
    q_pad = jnp.pad(q.reshape(batch, n_kv, rph, HEAD_DIM), ((0, 0), (0, 0), (0, 128 - rph), (0, 0)))
    o_c, sel = _cmp_sample(page_table, cache_rows, q_pad, wl["cmp_w1cat"], wl["cmp_w2"], wl["cmp_pecat"],
                           layer=l, n_kv=n_kv, gqa=gqa, dec_seq=dec_seq, name="cmp_s")
    n_pages = page_table.shape[1]
    bpp = cache_rows.shape[2] // L_SLC
    pps = _tile(n_pages, 4, 1)
    n_steps = n_pages // pps
    sel_steps = sel[:, :, :dec_seq, :n_pages * bpp].reshape(batch, n_kv, 1, dec_seq, n_steps, pps * bpp)
    sel_steps = jnp.broadcast_to(sel_steps, (batch, n_kv, gqa, dec_seq, n_steps, pps * bpp))
    sel_steps = sel_steps.transpose(0, 4, 1, 2, 3, 5).reshape(batch, n_steps, n_kv * rph, pps * bpp)
    sel_steps = jnp.pad(sel_steps, ((0, 0), (0, 0), (0, 0), (0, 128 - pps * bpp)))

    def new_rows(kind):
        rows = kv[:, kind * d_kv:(kind + 1) * d_kv].reshape(batch, dec_seq, d_kv)
        return jnp.pad(rows, ((0, 0), (0, 128 - dec_seq), (0, 0)))

    gates_s = gates[:, :3 * n_heads].reshape(batch, dec_seq, 3, n_kv, gqa).transpose(0, 3, 4, 1, 2)
    gates_s = jnp.pad(gates_s.reshape(batch, n_kv, rph, 3), ((0, 0), (0, 0), (0, 0), (0, 5)))
    b_rows = _slc_sample(page_table, cache_rows, q_pad, sel_steps, new_rows(2), new_rows(3), win_rows,
                         new_rows(4), new_rows(5), o_c, gates_s, layer=l, n_kv=n_kv, gqa=gqa,
                         dec_seq=dec_seq, pages_per_step=pps, name="slc_s")
    b_out = b_rows.reshape(batch, n_kv, gqa, dec_seq, HEAD_DIM).transpose(0, 3, 1, 2, 4)
    b_out = b_out.reshape(m, d_b).astype(BF16)

    x1, x1b = _mix_and_norm(x, a_out, b_out, ws, l, p["ln_g"], p["ln_b"], alpha, "s")
    st = conv_state
    zero = jnp.zeros((batch, dec_seq - 1, d_ff), F32)
    h1 = jnp.concatenate([st[:, 1:2], zero], axis=1).reshape(m, d_ff)
    h2 = jnp.concatenate([st, zero[:, 1:]], axis=1).reshape(m, d_ff)
    act, gate = _ffn_in_short(x1b, ws["f_in"], p["ffn_conv_w"], ws["f_cb"], h1, h2, layer=l,
                              seq_len=dec_seq, name="ffn_in_s")
    f = _matmul([act], ws["f_down"], layer=l, tn=512, name="ffn_down_s")
    x2, x2b = _add_ln(x1, f, p["ln_g"][l, 1], p["ln_b"][l, 1], alpha=alpha, name="ln2_s")
    new_nsa = nsa4.reshape(batch, dec_seq, 4, n_kv, HEAD_DIM)
    new_win = win4.reshape(batch, dec_seq, 2, n_kv, HEAD_DIM)
    conv_new = gate.reshape(batch, dec_seq, d_ff)[:, dec_seq - (CONV_W - 1):]
    return x2, x2b, new_nsa, new_win, v_rows.reshape(batch, dec_seq, d_a), conv_new


def kernel(x_prompt, x_sample, cache_nsa_kv, cache_win_kv, state_ffn_conv, page_table, w_in, sgu_w,
           sgu_b, sgu_g, cmp_pe, cmp_w1, cmp_w2, w_o, ln_g, ln_b, ffn_w_in, ffn_conv_w, ffn_conv_b,
           ffn_w_down):
    bp, seq, d_model = x_prompt.shape
    bs, dec_seq, _ = x_sample.shape
    depth = w_in.shape[0]
    n_kv = cache_nsa_kv.shape[4]
    page = cache_nsa_kv.shape[2]
    past = page_table.shape[1] * page
    d_a = d_model // 2
    d_b = d_model - d_a
    n_heads = d_b // HEAD_DIM
    d_kv = n_kv * HEAD_DIM
    d_ff = ffn_conv_w.shape[-1]
    dims = (d_a, d_b, d_kv, n_heads, d_ff)
    alpha = (2 * depth) ** 0.25
    assert dec_seq >= CONV_W - 1 and seq % CHUNK == 0

    cos_p, sin_p = _rope_tables(jnp.tile(jnp.arange(seq, dtype=jnp.int32), bp))
    cos_s, sin_s = _rope_tables(jnp.tile(past + jnp.arange(dec_seq, dtype=jnp.int32), bs))
    cache_rows = cache_nsa_kv.reshape(depth, cache_nsa_kv.shape[1], page, 4 * n_kv, HEAD_DIM)
    win_rows = cache_win_kv.reshape(depth, bs, cache_win_kv.shape[2], 2 * n_kv, HEAD_DIM)
    ws = _prep_weights(w_in, w_o, ffn_w_in, ffn_w_down, ffn_conv_b, 3 * n_heads)
    p = dict(sgu_w=sgu_w, sgu_b=sgu_b, sgu_g=sgu_g, ln_g=ln_g, ln_b=ln_b, ffn_conv_w=ffn_conv_w)

    xp = x_prompt.reshape(bp * seq, d_model)
    xs = x_sample.reshape(bs * dec_seq, d_model)
    xp_bf, xs_bf = xp.astype(BF16), xs.astype(BF16)
    outs = [[] for _ in range(7)]
    for l in range(depth):
        wl = _prep_compress(l, cmp_pe, cmp_w1, cmp_w2)
        xp, xp_bf, nsa_p, win_p, conv_p = _layer_prompt(
            xp, xp_bf, ws, wl, l, p, cos_p, sin_p, dims, bp, seq, n_kv, alpha)
        xs, xs_bf, nsa_s, win_s, v_s, conv_s = _layer_sample(
            xs, xs_bf, ws, wl, l, p, cos_s, sin_s, dims, bs, dec_seq, n_kv, alpha, page_table,
            cache_rows, win_rows, state_ffn_conv[l])
        for acc, val in zip(outs, (nsa_p, nsa_s, win_p, win_s, v_s, conv_p, conv_s)):
            acc.append(val)
    return (xp.reshape(bp, seq, d_model), xs.reshape(bs, dec_seq, d_model),
            *[jnp.stack(o) for o in outs])
```

```python
import functools
import math

import jax
import jax.numpy as jnp
import numpy as np
from jax import lax
from jax.experimental import pallas as pl
from jax.experimental.pallas import tpu as pltpu

HEAD_DIM = 128
CHUNK = 128
STRIDE = 16
L_CMP = 2 * STRIDE
L_SLC = 64
N_SEL = 16
WINDOW = 512
ROT_DIM = HEAD_DIM // 4
ROT_HALF = ROT_DIM // 2
ROPE_THETA = 500000.0
CONV_W = 3
LN_EPS = 1e-5
SCALE = HEAD_DIM ** -0.5
LOGIT_SCALE = SCALE * math.log2(math.e)
SEL_FORCE = 1e9
NEG_BIG = -3.0e38
VMEM_LIMIT = 56 * 1024 * 1024

F32 = jnp.float32
BF16 = jnp.bfloat16


def _cparams(*sem):
    return pltpu.CompilerParams(dimension_semantics=sem, vmem_limit_bytes=VMEM_LIMIT)


def _tile(n, pref, unit=128):
    if n <= pref:
        return n
    t = (pref // unit) * unit
    while t > unit and n % t:
        t -= unit
    assert n % t == 0, (n, pref, unit)
    return t


def _gelu(x):
    return jax.nn.gelu(x, approximate=True)


def _dot(a, b):
    return jnp.dot(a, b, preferred_element_type=F32)


def _dot_nt(a, b):
    return lax.dot_general(a, b, (((1,), (1,)), ((), ())), preferred_element_type=F32)


def _split_hi_lo(x):
    hi = x.astype(BF16)
    lo = (x - hi.astype(F32)).astype(BF16)
    return hi, lo


def _masked_softmax2(s, mask, axis=-1):
    sm = jnp.where(mask, s, NEG_BIG)
    m = jnp.max(sm, axis=axis, keepdims=True)
    p = jnp.where(mask, jnp.exp2(sm - m), 0.0)
    return p / jnp.maximum(jnp.sum(p, axis=axis, keepdims=True), 1e-30)


def _mm_kernel(*refs, n_lhs, epilogue):
    acc = _dot(refs[0][...], refs[n_lhs][...])
    for k in range(1, n_lhs):
        acc = acc + _dot(refs[k][...], refs[n_lhs + k][...])
    o_ref = refs[2 * n_lhs]
    if epilogue == "gelu":
        acc = _gelu(acc)
    elif epilogue == "sigmoid":
        acc = jax.nn.sigmoid(acc)
    o_ref[...] = acc.astype(o_ref.dtype)


def _matmul(xs, w, *, layer, col_start=0, n=None, epilogue="none", out_dtype=F32, tm=512, tn=512,
            name):
    m, kdim = xs[0].shape
    n = w.shape[2] if n is None else n
    tm = _tile(m, tm, 8)
    tn = _tile(n, tn)
    assert col_start % tn == 0 and all(x.shape == (m, kdim) for x in xs)
    c0 = col_start // tn
    n_lhs = len(xs)
    in_specs = [pl.BlockSpec((tm, kdim), lambda i, j: (i, 0)) for _ in xs]
    in_specs += [pl.BlockSpec((None, kdim, tn), lambda i, j, r=r: (layer, r, c0 + j))
                 for r in range(n_lhs)]
    return pl.pallas_call(
        functools.partial(_mm_kernel, n_lhs=n_lhs, epilogue=epilogue),
        out_shape=jax.ShapeDtypeStruct((m, n), out_dtype),
        grid=(m // tm, n // tn),
        in_specs=in_specs,
        out_specs=pl.BlockSpec((tm, tn), lambda i, j: (i, j)),
        compiler_params=_cparams("parallel", "arbitrary"),
        name=name,
    )(*xs, *([w] * n_lhs))


def _rope_mm_kernel(x_ref, w_ref, cos_ref, sin_ref, o_ref, *cache_refs, heads_per_tile, alternate,
                    n_nsa_kinds):
    j = pl.program_id(1)
    acc = _dot(x_ref[...], w_ref[...])
    cosv, sinv = cos_ref[...], sin_ref[...]
    if alternate:
        rot = lax.rem(j, 2) == 0
        cosv = jnp.where(rot, cosv, 1.0)
        sinv = jnp.where(rot, sinv, 0.0)
    lane = lax.broadcasted_iota(jnp.int32, cosv.shape, 1)
    heads = []
    for h in range(heads_per_tile):
        hs = acc[:, h * HEAD_DIM:(h + 1) * HEAD_DIM]
        partner = jnp.where(lane < ROT_HALF,
                            pltpu.roll(hs, HEAD_DIM - ROT_HALF, 1),
                            pltpu.roll(hs, ROT_HALF, 1))
        heads.append(hs * cosv + partner * sinv)
        o_ref[:, h * HEAD_DIM:(h + 1) * HEAD_DIM] = heads[h]
    if cache_refs:
        nsa_ref, win_ref = cache_refs

        @pl.when(j < n_nsa_kinds)
        def _():
            for h in range(heads_per_tile):
                nsa_ref[:, h, :] = heads[h]

        @pl.when(j >= n_nsa_kinds)
        def _():
            for h in range(heads_per_tile):
                win_ref[:, h, :] = heads[h]


def _rope_matmul(x, w, cos_t, sin_t, *, layer, col_start, n, tn, alternate, n_kv=None, tm=1024,
                 name):
    m, k = x.shape
    tm = _tile(m, tm, 8)
    assert col_start % tn == 0 and n % tn == 0
    c0 = col_start // tn
    hpt = tn // HEAD_DIM
    out_shape = [jax.ShapeDtypeStruct((m, n), F32)]
    out_specs = [pl.BlockSpec((tm, tn), lambda i, j: (i, j))]
    n_nsa = 4
    if alternate:
        assert hpt == n_kv and n == 6 * tn
        out_shape += [jax.ShapeDtypeStruct((m, n_nsa, n_kv, HEAD_DIM), F32),
                      jax.ShapeDtypeStruct((m, 2, n_kv, HEAD_DIM), F32)]
        out_specs += [pl.BlockSpec((tm, None, n_kv, HEAD_DIM),
                                   lambda i, j: (i, jnp.minimum(j, n_nsa - 1), 0, 0)),
                      pl.BlockSpec((tm, None, n_kv, HEAD_DIM),
                                   lambda i, j: (i, jnp.maximum(j - n_nsa, 0), 0, 0))]
    kern = functools.partial(_rope_mm_kernel, heads_per_tile=hpt, alternate=alternate,
                             n_nsa_kinds=n_nsa)
    return pl.pallas_call(
        kern,
        out_shape=tuple(out_shape),
        grid=(m // tm, n // tn),
        in_specs=[pl.BlockSpec((tm, k), lambda i, j: (i, 0)),
                  pl.BlockSpec((None, k, tn), lambda i, j: (layer, 0, c0 + j)),
                  pl.BlockSpec((tm, HEAD_DIM), lambda i, j: (i, 0)),
                  pl.BlockSpec((tm, HEAD_DIM), lambda i, j: (i, 0))],
        out_specs=tuple(out_specs),
        compiler_params=_cparams("arbitrary", "arbitrary"),
        name=name,
    )(x, w, cos_t, sin_t)


def _add_ln_kernel(x_ref, r_ref, g_ref, b_ref, y_ref, yb_ref, *, alpha):
    z = alpha * x_ref[...] + r_ref[...]
    mu = jnp.mean(z, axis=-1, keepdims=True)
    zc = z - mu
    var = jnp.mean(zc * zc, axis=-1, keepdims=True)
    y = zc * lax.rsqrt(var + LN_EPS) * g_ref[...] + b_ref[...]
    y_ref[...] = y
    yb_ref[...] = y.astype(BF16)


def _add_ln(x, r, g, b, *, alpha, name):
    m, d = x.shape
    tr = _tile(m, 256, 8)
    return pl.pallas_call(
        functools.partial(_add_ln_kernel, alpha=alpha),
        out_shape=(jax.ShapeDtypeStruct((m, d), F32), jax.ShapeDtypeStruct((m, d), BF16)),
        grid=(m // tr,),
        in_specs=[pl.BlockSpec((tr, d), lambda i: (i, 0)),
                  pl.BlockSpec((tr, d), lambda i: (i, 0)),
                  pl.BlockSpec((1, d), lambda i: (0, 0)),
                  pl.BlockSpec((1, d), lambda i: (0, 0))],
        out_specs=(pl.BlockSpec((tr, d), lambda i: (i, 0)),
                   pl.BlockSpec((tr, d), lambda i: (i, 0))),
        compiler_params=_cparams("parallel"),
        name=name,
    )(x, r, g.reshape(1, d), b.reshape(1, d))


def _sgu_kernel(uv_ref, w_ref, bt_ref, g_ref, a_ref, vn_ref, *, d_a, n_groups):
    rows = w_ref.shape[1]
    r = lax.broadcasted_iota(jnp.int32, (rows, rows), 0)
    c = lax.broadcasted_iota(jnp.int32, (rows, rows), 1)
    causal = r >= c
    for g in range(n_groups):
        lo = g * HEAD_DIM
        v = uv_ref[:, d_a + lo:d_a + lo + HEAD_DIM]
        mu = jnp.mean(v, axis=-1, keepdims=True)
        vc = v - mu
        var = jnp.mean(vc * vc, axis=-1, keepdims=True)
        vn = vc * lax.rsqrt(var + LN_EPS) * g_ref[:, lo:lo + HEAD_DIM]
        vn_ref[:, lo:lo + HEAD_DIM] = vn
        w = jnp.where(causal, w_ref[g], 0.0).astype(BF16)
        mixed = _dot(w, vn.astype(BF16)) + bt_ref[:, g:g + 1]
        a_ref[:, lo:lo + HEAD_DIM] = (uv_ref[:, lo:lo + HEAD_DIM] * mixed).astype(a_ref.dtype)


def _sgu(uv, w, bt, gain, *, rows, name):
    m = uv.shape[0]
    d_a = uv.shape[1] // 2
    n_groups = d_a // HEAD_DIM
    return pl.pallas_call(
        functools.partial(_sgu_kernel, d_a=d_a, n_groups=n_groups),
        out_shape=(jax.ShapeDtypeStruct((m, d_a), BF16), jax.ShapeDtypeStruct((m, d_a), F32)),
        grid=(m // rows,),
        in_specs=[pl.BlockSpec((rows, 2 * d_a), lambda i: (i, 0)),
                  pl.BlockSpec((n_groups, rows, rows), lambda i: (0, 0, 0)),
                  pl.BlockSpec((rows, n_groups), lambda i: (0, 0)),
                  pl.BlockSpec((1, d_a), lambda i: (0, 0))],
        out_specs=(pl.BlockSpec((rows, d_a), lambda i: (i, 0)),
                   pl.BlockSpec((rows, d_a), lambda i: (i, 0))),
        compiler_params=_cparams("parallel"),
        name=name,
    )(uv, w, bt, gain.reshape(1, d_a))


def _conv_act(gate, g1, g2, up, cw_ref, cb_ref):
    c = cb_ref[...] + g2 * cw_ref[0:1, :] + g1 * cw_ref[1:2, :] + gate * cw_ref[2:3, :]
    return _gelu(c) * up


def _ffn_in_seq_kernel(x_ref, *refs, tiles_per_seq, n_sub):
    wg, wu, cw, cb, st = (refs[k * n_sub:(k + 1) * n_sub] for k in range(5))
    a_ref, cn_ref, carry_ref = refs[5 * n_sub:]
    i = pl.program_id(1)
    x = x_ref[...]
    tm = x.shape[0]
    tn = wg[0].shape[1]
    first = lax.rem(i, tiles_per_seq) == 0
    row = lax.broadcasted_iota(jnp.int32, (tm, tn), 0)
    for k in range(n_sub):
        cols = slice(k * tn, (k + 1) * tn)
        gate = _dot(x, wg[k][...])
        up = _dot(x, wu[k][...])
        prev2 = jnp.where(first, st[k][0:1, :], carry_ref[0:1, cols])
        prev1 = jnp.where(first, st[k][1:2, :], carry_ref[1:2, cols])
        g1 = jnp.where(row == 0, prev1, pltpu.roll(gate, 1, 0))
        g2 = jnp.where(row == 0, prev2, jnp.where(row == 1, prev1, pltpu.roll(gate, 2, 0)))
        a_ref[:, cols] = _conv_act(gate, g1, g2, up, cw[k], cb[k]).astype(a_ref.dtype)
        tail = gate[tm - 2:tm, :]
        carry_ref[0:2, cols] = tail
        cn_ref[:, cols] = tail


def _ffn_in_seq(x, w, cw, cb, state, *, layer, seq_len, tm=1024, tn=256, n_sub=2, name):
    m, k = x.shape
    d_ff = cw.shape[2]
    tm = _tile(seq_len, tm, 8)
    tn = _tile(d_ff, tn)
    n_tiles = d_ff // tn
    tps = seq_len // tm
    n_seq = m // seq_len

    def col(j, s):
        return jnp.minimum(j * n_sub + s, n_tiles - 1)

    subs = range(n_sub)
    in_specs = [pl.BlockSpec((tm, k), lambda j, i: (i, 0))]
    in_specs += [pl.BlockSpec((None, k, tn), lambda j, i, s=s: (layer, 0, col(j, s))) for s in subs]
    in_specs += [pl.BlockSpec((None, k, tn), lambda j, i, s=s: (layer, 0, n_tiles + col(j, s)))
                 for s in subs]
    in_specs += [pl.BlockSpec((None, CONV_W, tn), lambda j, i, s=s: (layer, 0, col(j, s))) for s in subs]
    in_specs += [pl.BlockSpec((None, 1, tn), lambda j, i, s=s: (layer, 0, col(j, s))) for s in subs]
    in_specs += [pl.BlockSpec((None, CONV_W - 1, tn), lambda j, i, s=s: (i // tps, 0, col(j, s)))
                 for s in subs]
    return pl.pallas_call(
        functools.partial(_ffn_in_seq_kernel, tiles_per_seq=tps, n_sub=n_sub),
        out_shape=(jax.ShapeDtypeStruct((m, d_ff), BF16),
                   jax.ShapeDtypeStruct((n_seq, CONV_W - 1, d_ff), F32)),
        grid=(pl.cdiv(n_tiles, n_sub), m // tm),
        in_specs=in_specs,
        out_specs=(pl.BlockSpec((tm, n_sub * tn), lambda j, i: (i, j)),
                   pl.BlockSpec((None, CONV_W - 1, n_sub * tn), lambda j, i: (i // tps, 0, j))),
        scratch_shapes=[pltpu.VMEM((8, n_sub * tn), F32)],
        compiler_params=_cparams("arbitrary", "arbitrary"),
        name=name,
    )(x, *([w] * (2 * n_sub)), *([cw] * n_sub), *([cb] * n_sub), *([state] * n_sub))


def _ffn_in_short_kernel(x_ref, wg_ref, wu_ref, cw_ref, cb_ref, h1_ref, h2_ref, a_ref, gate_ref,
                         *, seq_len):
    x = x_ref[...]
    gate = _dot(x, wg_ref[...])
    up = _dot(x, wu_ref[...])
    t = lax.rem(lax.broadcasted_iota(jnp.int32, gate.shape, 0), seq_len)
    g1 = jnp.where(t >= 1, pltpu.roll(gate, 1, 0), h1_ref[...])
    g2 = jnp.where(t >= 2, pltpu.roll(gate, 2, 0), h2_ref[...])
    a_ref[...] = _conv_act(gate, g1, g2, up, cw_ref, cb_ref).astype(a_ref.dtype)
    gate_ref[...] = gate


def _ffn_in_short(x, w, cw, cb, h1, h2, *, layer, seq_len, tn=256, name):
    m, k = x.shape
    d_ff = cw.shape[2]
    tn = _tile(d_ff, tn)
    n_tiles = d_ff // tn
    return pl.pallas_call(
        functools.partial(_ffn_in_short_kernel, seq_len=seq_len),
        out_shape=(jax.ShapeDtypeStruct((m, d_ff), BF16), jax.ShapeDtypeStruct((m, d_ff), F32)),
        grid=(n_tiles,),
        in_specs=[pl.BlockSpec((m, k), lambda j: (0, 0)),
                  pl.BlockSpec((None, k, tn), lambda j: (layer, 0, j)),
                  pl.BlockSpec((None, k, tn), lambda j: (layer, 0, n_tiles + j)),
                  pl.BlockSpec((None, CONV_W, tn), lambda j: (layer, 0, j)),
                  pl.BlockSpec((None, 1, tn), lambda j: (layer, 0, j)),
                  pl.BlockSpec((m, tn), lambda j: (0, j)),
                  pl.BlockSpec((m, tn), lambda j: (0, j))],
        out_specs=(pl.BlockSpec((m, tn), lambda j: (0, j)),
                   pl.BlockSpec((m, tn), lambda j: (0, j))),
        compiler_params=_cparams("parallel"),
        name=name,
    )(x, w, w, cw, cb, h1, h2)


def _compress_rows(xa, w1a, w1b, w2, hpe):
    a = _dot(xa, w1a)
    b = _dot(xa, w1b)
    n = a.shape[0]
    h = a + pltpu.roll(b, n - 1, 0) + hpe
    return _dot(_gelu(h).astype(BF16), w2)


def _pos_embed_term(pe_ref, w1_ref, kind):
    return _dot(pe_ref[kind], w1_ref[kind])[0:1, :]


def _compress_p_kernel(k_ref, v_ref, w1_ref, w2_ref, pe_ref, kc_ref, vc_ref, *, n_chunks):
    half = STRIDE * HEAD_DIM
    for kind, (src, dst) in enumerate(((k_ref, kc_ref), (v_ref, vc_ref))):
        xa = jnp.concatenate(
            [src[pl.ds(s, n_chunks, stride=STRIDE), :] for s in range(STRIDE)], axis=1).astype(BF16)
        hpe = _pos_embed_term(pe_ref, w1_ref, kind)
        out = _compress_rows(xa, w1_ref[kind, 0:half, :], w1_ref[kind, half:2 * half, :],
                             w2_ref[kind], hpe)
        dst[...] = out.astype(dst.dtype)


def _compress_prompt(kv, w1, w2, pe, *, batch, seq, n_kv, name):
    n_chunks = seq // STRIDE
    out = jax.ShapeDtypeStruct((batch, n_kv, n_chunks, HEAD_DIM), BF16)
    ospec = pl.BlockSpec((None, None, n_chunks, HEAD_DIM), lambda b, h: (b, h, 0, 0))
    return pl.pallas_call(
        functools.partial(_compress_p_kernel, n_chunks=n_chunks),
        out_shape=(out, out),
        grid=(batch, n_kv),
        in_specs=[pl.BlockSpec((seq, HEAD_DIM), lambda b, h: (b, h)),
                  pl.BlockSpec((seq, HEAD_DIM), lambda b, h: (b, n_kv + h)),
                  pl.BlockSpec(w1.shape, lambda b, h: (0, 0, 0)),
                  pl.BlockSpec(w2.shape, lambda b, h: (0, 0, 0)),
                  pl.BlockSpec(pe.shape, lambda b, h: (0, 0, 0))],
        out_specs=(ospec, ospec),
        compiler_params=_cparams("parallel", "parallel"),
        name=name,
    )(kv, kv, w1, w2, pe)


def _block_scores(imp, pos, n_blocks, block_axis):
    j = lax.broadcasted_iota(jnp.int32, imp.shape, block_axis)
    cur = pos // L_SLC
    forced = (j == 0) | (j == cur) | (j == cur - 1)
    valid = j * L_SLC <= pos
    score = jnp.where(valid, jnp.where(forced, SEL_FORCE, imp), -SEL_FORCE)
    return jnp.where(j < n_blocks, score, NEG_BIG)


def _overlap_matrix(nc, nb, rows, cols):
    i = np.arange(nc)[:, None]
    j = np.arange(nb)[None, :]
    lo = np.maximum(i * STRIDE, j * L_SLC)
    hi = np.minimum(i * STRIDE + L_CMP, (j + 1) * L_SLC)
    ov = np.zeros((rows, cols), np.float32)
    ov[:nc, :nb] = np.maximum(hi - lo, 0) / STRIDE
    return ov


def _block_to_key_matrix(n_groups, lanes, keys_per_group):
    ex = np.zeros((n_groups, lanes, keys_per_group), np.float32)
    for c in range(n_groups):
        k = np.arange(keys_per_group)
        ex[c, (c * keys_per_group + k) // L_SLC, k] = 1.0
    return ex


def _attn_p_kernel(q_ref, kc_ref, vc_ref, ks_ref, vs_ref, kw_ref, vw_ref, gt_ref, ovt_ref, ex_ref,
                   o_ref, m_scr, acc_scr, *, tq, tk, gqa, n_blocks, n_cmp, win_keys):
    qi = pl.program_id(2)
    t0 = qi * tq
    q = q_ref[...] * LOGIT_SCALE
    qs = [q[:, g * HEAD_DIM:(g + 1) * HEAD_DIM].astype(BF16) for g in range(gqa)]
    pos_t = t0 + lax.broadcasted_iota(jnp.int32, (tq, 1), 0)
    ones = jnp.ones((max(tk, win_keys), HEAD_DIM), BF16)

    kc = kc_ref[...]
    vc = vc_ref[...]
    n = lax.broadcasted_iota(jnp.int32, (tq, kc.shape[0]), 1)
    last_end = jnp.minimum(pos_t, (n_cmp - 1) * STRIDE + (L_CMP - 1))
    bias_c = jnp.where(n * STRIDE + (L_CMP - 1) <= last_end, 0.0, NEG_BIG)
    o_c = []
    p_grp = None
    for g in range(gqa):
        sm = _dot_nt(qs[g], kc) + bias_c
        m = jnp.max(sm, axis=-1, keepdims=True)
        e = jnp.exp2(sm - m)
        norm = jnp.where(m > 0.5 * NEG_BIG,
                         1.0 / jnp.maximum(jnp.sum(e, axis=-1, keepdims=True), 1e-30), 0.0)
        p = e * norm
        o_c.append(_dot(p.astype(BF16), vc))
        p_grp = p if p_grp is None else p_grp + p

    hi, lo = _split_hi_lo(p_grp)
    imp_t = _dot_nt(ovt_ref[...], hi) + _dot_nt(ovt_ref[...], lo)
    pos_row = t0 + lax.broadcasted_iota(jnp.int32, (1, tq), 1)
    score = _block_scores(imp_t, pos_row, n_blocks, 0)
    blk = lax.broadcasted_iota(jnp.int32, score.shape, 0)
    rank = jnp.zeros(score.shape, F32)
    for i in range(n_blocks):
        ci = score[i:i + 1, :]
        rank = rank + jnp.where((ci > score) | ((ci == score) & (blk > i)), 1.0, 0.0)
    sel_t = jnp.where(rank < float(min(N_SEL, n_blocks)), 1.0, 0.0)
    lanes = ex_ref.shape[1]
    sel_t = jnp.concatenate([sel_t, jnp.zeros((lanes - sel_t.shape[0], tq), F32)], axis=0)
    sel = sel_t.T.astype(BF16)

    m_scr[...] = jnp.full(m_scr.shape, NEG_BIG, F32)
    acc_scr[...] = jnp.zeros(acc_scr.shape, F32)

    def key_tile(c, carry):
        k0 = pl.multiple_of(c * tk, tk)
        kt = ks_ref[pl.ds(k0, tk), :].astype(BF16)
        vt = jnp.concatenate([vs_ref[pl.ds(k0, tk), :].astype(BF16), ones[0:tk]], axis=1)
        sel_keys = _dot(sel, ex_ref[c])
        kpos = k0 + lax.broadcasted_iota(jnp.int32, sel_keys.shape, 1)
        bias = jnp.where(sel_keys > jnp.where(kpos <= pos_t, 0.5, 2.0), 0.0, NEG_BIG)
        for g in range(gqa):
            rows = slice(g * tq, (g + 1) * tq)
            sm = _dot_nt(qs[g], kt) + bias
            m_old = m_scr[rows]
            m_new = jnp.maximum(m_old, jnp.max(sm, axis=-1, keepdims=True))
            e = jnp.exp2(sm - m_new)
            acc_scr[rows] = jnp.exp2(m_old - m_new) * acc_scr[rows] + _dot(e.astype(BF16), vt)
            m_scr[rows] = m_new
        return carry

    lax.fori_loop(0, (t0 + tq - 1) // tk + 1, key_tile, 0)

    start = pl.multiple_of(jnp.maximum(t0 + tq - win_keys, 0), 128)
    kw = kw_ref[pl.ds(start, win_keys), :].astype(BF16)
    vw = jnp.concatenate([vw_ref[pl.ds(start, win_keys), :].astype(BF16), ones[0:win_keys]], axis=1)
    d = pos_t - (start + lax.broadcasted_iota(jnp.int32, (tq, win_keys), 1))
    bias_w = jnp.where(d >= 0, jnp.where(d < WINDOW, 0.0, NEG_BIG), NEG_BIG)
    gt = gt_ref[...]
    for g in range(gqa):
        sm = _dot_nt(qs[g], kw) + bias_w
        e = jnp.exp2(sm - jnp.max(sm, axis=-1, keepdims=True))
        ow = _dot(e.astype(BF16), vw)
        o_w = ow[:, 0:HEAD_DIM] / ow[:, HEAD_DIM:HEAD_DIM + 1]
        acc = acc_scr[g * tq:(g + 1) * tq]
        o_s = acc[:, 0:HEAD_DIM] / jnp.maximum(acc[:, HEAD_DIM:HEAD_DIM + 1], 1e-30)
        out = (gt[:, g:g + 1] * o_c[g] + gt[:, gqa + g:gqa + g + 1] * o_s
               + gt[:, 2 * gqa + g:2 * gqa + g + 1] * o_w)
        o_ref[:, g * HEAD_DIM:(g + 1) * HEAD_DIM] = out.astype(o_ref.dtype)


def _attn_prompt(q2d, kv, kcmp, vcmp, gates_h, *, batch, seq, n_heads, n_kv, tq=256, tk=512, name):
    gqa = n_heads // n_kv
    tq = _tile(seq, tq, 8)
    tk = _tile(seq, tk)
    nq = seq // tq
    n_chunks = seq // STRIDE
    n_cmp = n_chunks - 1
    n_blocks = -(-seq // L_SLC)
    lanes = -(-n_blocks // 128) * 128
    block_rows = -(-n_blocks // 8) * 8
    ovt = jnp.asarray(_overlap_matrix(n_cmp, n_blocks, n_chunks, block_rows).T.copy(), BF16)
    ex = jnp.asarray(_block_to_key_matrix(seq // tk, lanes, tk), BF16)
    win_keys = min(WINDOW + tq, seq)
    rows = gqa * tq
    kern = functools.partial(_attn_p_kernel, tq=tq, tk=tk, gqa=gqa, n_blocks=n_blocks,
                             n_cmp=n_cmp, win_keys=win_keys)

    def kv_spec(kind):
        return pl.BlockSpec((seq, HEAD_DIM), lambda b, h, i: (b, kind * n_kv + h))

    cmp_spec = pl.BlockSpec((None, None, n_chunks, HEAD_DIM), lambda b, h, i: (b, h, 0, 0))
    return pl.pallas_call(
        kern,
        out_shape=jax.ShapeDtypeStruct((batch * seq, n_heads * HEAD_DIM), BF16),
        grid=(batch, n_kv, nq),
        in_specs=[pl.BlockSpec((tq, gqa * HEAD_DIM), lambda b, h, i: (b * nq + i, h)),
                  cmp_spec, cmp_spec,
                  kv_spec(2), kv_spec(3), kv_spec(4), kv_spec(5),
                  pl.BlockSpec((None, tq, 3 * gqa), lambda b, h, i: (h, b * nq + i, 0)),
                  pl.BlockSpec(ovt.shape, lambda b, h, i: (0, 0)),
                  pl.BlockSpec(ex.shape, lambda b, h, i: (0, 0, 0))],
        out_specs=pl.BlockSpec((tq, gqa * HEAD_DIM), lambda b, h, i: (b * nq + i, h)),
        scratch_shapes=[pltpu.VMEM((rows, 1), F32),
                        pltpu.VMEM((rows, 2 * HEAD_DIM), F32)],
        compiler_params=_cparams("parallel", "parallel", "arbitrary"),
        name=name,
    )(q2d, kcmp, vcmp, kv, kv, kv, kv, gates_h, ovt, ex)


def _head_slabs(rows_ref, first, n_heads):
    return jnp.concatenate([rows_ref[:, first + h, :] for h in range(n_heads)], axis=1)


def _cmp_s_kernel(pt_ref, *refs, n_kv, n_steps, pages_per_step, steps_per_group, n_cmp, n_blocks,
                  dec_seq, gqa, past):
    del pt_ref
    page_refs = refs[:pages_per_step]
    q_ref, w1_ref, w2_ref, pe_ref, ovt_ref, oc_ref, sel_ref, x_scr, ab_scr = refs[pages_per_step:]
    p = pl.program_id(1)
    n_kh = 2 * n_kv
    page = page_refs[0].shape[0]
    cpp = page // STRIDE
    step_chunks = cpp * pages_per_step
    group_chunks = step_chunks * steps_per_group
    c0 = pl.multiple_of(lax.rem(p, steps_per_group) * step_chunks, 8)
    for k, page_ref in enumerate(page_refs):
        for s in range(STRIDE):
            for kh in range(n_kh):
                x_scr[kh, pl.ds(c0 + k * cpp, cpp), s * HEAD_DIM:(s + 1) * HEAD_DIM] = (
                    page_ref[pl.ds(s, cpp, stride=STRIDE), kh, :])

    @pl.when(lax.rem(p, steps_per_group) == steps_per_group - 1)
    def _():
        g0 = pl.multiple_of((p // steps_per_group) * group_chunks, 8)
        for kh in range(n_kh):
            kind = kh // n_kv
            ab_scr[kh, pl.ds(g0, group_chunks), :] = _dot(x_scr[kh].astype(BF16), w1_ref[kind])

    @pl.when(p == n_steps - 1)
    def _():
        n_chunks = ab_scr.shape[1]
        comp = []
        for kh in range(n_kh):
            kind = kh // n_kv
            hpe = _dot(pe_ref[kind], w1_ref[kind])
            hpe = hpe[0:1, 0:HEAD_DIM] + hpe[8:9, HEAD_DIM:2 * HEAD_DIM]
            ab = ab_scr[kh]
            h = ab[:, 0:HEAD_DIM] + pltpu.roll(ab[:, HEAD_DIM:2 * HEAD_DIM], n_chunks - 1, 0) + hpe
            comp.append(_dot(_gelu(h).astype(BF16), w2_ref[kind]).astype(BF16))
        qrows = lax.broadcasted_iota(jnp.int32, (1, HEAD_DIM), 1)
        pos = past + lax.rem(qrows, dec_seq)
        for h in range(n_kv):
            kc, vc = comp[h], comp[n_kv + h]
            st = _dot_nt(kc, (q_ref[h] * LOGIT_SCALE).astype(BF16))
            n = lax.broadcasted_iota(jnp.int32, st.shape, 0)
            pt = _masked_softmax2(st, (n * STRIDE + (L_CMP - 1) <= pos) & (n < n_cmp), axis=0)
            oc_ref[h] = _dot(pt.T.astype(BF16), vc)
            pg = pt
            for g in range(1, gqa):
                pg = pg + pltpu.roll(pt, HEAD_DIM - g * dec_seq, 1)
            hi, lo = _split_hi_lo(pg)
            imp_t = _dot(ovt_ref[...], hi) + _dot(ovt_ref[...], lo)
            imp = imp_t.T
            tpos = past + lax.broadcasted_iota(jnp.int32, (imp.shape[0], 1), 0)
            score = _block_scores(imp, tpos, n_blocks, 1)
            score_t = score.T
            nb_pad = score.shape[1]
            ii = lax.broadcasted_iota(jnp.int32, (nb_pad, nb_pad), 0)
            jj = lax.broadcasted_iota(jnp.int32, (nb_pad, nb_pad), 1)
            for t in range(dec_seq):
                col = score_t[:, t:t + 1]
                rowv = score[t:t + 1, :]
                beats = (col > rowv) | ((col == rowv) & (ii < jj))
                rank = jnp.sum(jnp.where(beats, 1.0, 0.0), axis=0, keepdims=True)
                sel_ref[h, t:t + 1, :] = jnp.where(rank < float(min(N_SEL, n_blocks)), 1.0, 0.0)
            sel_ref[h, dec_seq:, :] = jnp.zeros((sel_ref.shape[1] - dec_seq, nb_pad), F32)


def _page_specs(page, rows, row_block, layer, pages_per_step):
    def spec(k):
        return pl.BlockSpec((None, None, page, rows, HEAD_DIM),
                            lambda b, p, pt: (layer, pt[b, p * pages_per_step + k], 0, row_block, 0))
    return [spec(k) for k in range(pages_per_step)]


def _cmp_sample(page_table, cache_rows, q_pad, w1cat, w2, pe, *, layer, n_kv, gqa, dec_seq, name):
    batch, n_pages = page_table.shape
    page = cache_rows.shape[2]
    past = n_pages * page
    cpp = page // STRIDE
    n_chunks = past // STRIDE
    assert dec_seq < STRIDE and n_chunks % 8 == 0
    n_cmp = (past + dec_seq) // STRIDE - 1
    n_blocks = -(-(past + dec_seq) // L_SLC)
    nb_pad = -(-n_blocks // 128) * 128
    pps = _tile(n_pages, 4, 1)
    n_steps = n_pages // pps
    spg = _tile(n_steps, max(1, 128 // (cpp * pps)), 1)
    n_kh = 2 * n_kv
    ovt = jnp.asarray(_overlap_matrix(n_cmp, n_blocks, n_chunks, nb_pad).T.copy(), BF16)
    kern = functools.partial(_cmp_s_kernel, n_kv=n_kv, n_steps=n_steps, pages_per_step=pps,
                             steps_per_group=spg, n_cmp=n_cmp, n_blocks=n_blocks, dec_seq=dec_seq,
                             gqa=gqa, past=past)
    grid_spec = pltpu.PrefetchScalarGridSpec(
        num_scalar_prefetch=1,
        grid=(batch, n_steps),
        in_specs=_page_specs(page, n_kh, 0, layer, pps) + [
            pl.BlockSpec((None, n_kv, 128, HEAD_DIM), lambda b, p, pt: (b, 0, 0, 0)),
            pl.BlockSpec(w1cat.shape, lambda b, p, pt: (0, 0, 0)),
            pl.BlockSpec(w2.shape, lambda b, p, pt: (0, 0, 0)),
            pl.BlockSpec(pe.shape, lambda b, p, pt: (0, 0, 0)),
            pl.BlockSpec(ovt.shape, lambda b, p, pt: (0, 0))],
        out_specs=(pl.BlockSpec((None, n_kv, 128, HEAD_DIM), lambda b, p, pt: (b, 0, 0, 0)),
                   pl.BlockSpec((None, n_kv, 8, nb_pad), lambda b, p, pt: (b, 0, 0, 0))),
        scratch_shapes=[pltpu.VMEM((n_kh, spg * pps * cpp, STRIDE * HEAD_DIM), F32),
                        pltpu.VMEM((n_kh, n_chunks, 2 * HEAD_DIM), F32)],
    )
    return pl.pallas_call(
        kern,
        out_shape=(jax.ShapeDtypeStruct((batch, n_kv, 128, HEAD_DIM), F32),
                   jax.ShapeDtypeStruct((batch, n_kv, 8, nb_pad), F32)),
        grid_spec=grid_spec,
        compiler_params=_cparams("parallel", "arbitrary"),
        name=name,
    )(page_table, *([cache_rows] * pps), q_pad, w1cat, w2, pe, ovt)


def _slc_s_kernel(pt_ref, *refs, n_kv, n_steps, pages_per_step, rows_per_head, dec_seq, past, w_buf):
    del pt_ref
    page_refs = refs[:pages_per_step]
    (q_ref, sel_ref, ex_ref, kn_ref, vn_ref, win_ref, kwn_ref, vwn_ref, oc_ref, gt_ref, o_ref,
     qbd_scr, m_scr, l_scr, acc_scr) = refs[pages_per_step:]
    p = pl.program_id(1)
    rows = n_kv * rows_per_head

    @pl.when(p == 0)
    def _():
        qbd_scr[...] = jnp.zeros(qbd_scr.shape, qbd_scr.dtype)
        for h in range(n_kv):
            qbd_scr[h * rows_per_head:(h + 1) * rows_per_head, h * HEAD_DIM:(h + 1) * HEAD_DIM] = (
                q_ref[h, 0:rows_per_head, :] * LOGIT_SCALE)
        m_scr[...] = jnp.full(m_scr.shape, NEG_BIG, F32)
        l_scr[...] = jnp.zeros(l_scr.shape, F32)
        acc_scr[...] = jnp.zeros(acc_scr.shape, F32)

    qbd = qbd_scr[...].astype(BF16)

    def online_update(s, mask, v):
        sm = jnp.where(mask, s, NEG_BIG)
        m_old = m_scr[...]
        m_new = jnp.maximum(m_old, jnp.max(sm, axis=-1, keepdims=True))
        alpha = jnp.exp2(m_old - m_new)
        e = jnp.where(mask, jnp.exp2(sm - m_new), 0.0)
        l_scr[...] = alpha * l_scr[...] + jnp.sum(e, axis=-1, keepdims=True)
        acc_scr[...] = alpha * acc_scr[...] + _dot(e.astype(BF16), v)
        m_scr[...] = m_new

    ks, vs = [], []
    for page_ref in page_refs:
        ks.append(_head_slabs(page_ref, 0, n_kv))
        vs.append(_head_slabs(page_ref, n_kv, n_kv))
    k_all = jnp.concatenate(ks, axis=0).astype(BF16)
    v_all = jnp.concatenate(vs, axis=0).astype(BF16)
    s = _dot_nt(qbd, k_all)
    sel_keys = _dot(sel_ref[...].astype(BF16), ex_ref[...])
    online_update(s, sel_keys > 0.5, v_all)

    @pl.when(p == n_steps - 1)
    def _():
        r = lax.broadcasted_iota(jnp.int32, (rows, 1), 0)
        t = lax.rem(r, dec_seq)
        s = _dot_nt(qbd, kn_ref[...].astype(BF16))
        j = lax.broadcasted_iota(jnp.int32, s.shape, 1)
        online_update(s, (j <= t) & (j < dec_seq), vn_ref[...].astype(BF16))
        o_s = acc_scr[...] / jnp.maximum(l_scr[...], 1e-30)

        sb = _dot_nt(qbd, _head_slabs(win_ref, 0, n_kv).astype(BF16))
        sn = _dot_nt(qbd, kwn_ref[...].astype(BF16))
        ib = lax.broadcasted_iota(jnp.int32, sb.shape, 1)
        kpos = past - w_buf + ib
        d = (past + t) - kpos
        mb = (d >= 0) & (d < WINDOW) & (kpos >= 0)
        jn = lax.broadcasted_iota(jnp.int32, sn.shape, 1)
        mn = (jn <= t) & (jn < dec_seq) & (t - jn < WINDOW)
        smb = jnp.where(mb, sb, NEG_BIG)
        smn = jnp.where(mn, sn, NEG_BIG)
        mx = jnp.maximum(jnp.max(smb, axis=-1, keepdims=True), jnp.max(smn, axis=-1, keepdims=True))
        eb = jnp.where(mb, jnp.exp2(smb - mx), 0.0)
        en = jnp.where(mn, jnp.exp2(smn - mx), 0.0)
        den = jnp.maximum(jnp.sum(eb, axis=-1, keepdims=True) + jnp.sum(en, axis=-1, keepdims=True),
                          1e-30)
        o_w = (_dot((eb / den).astype(BF16), _head_slabs(win_ref, n_kv, n_kv).astype(BF16))
               + _dot((en / den).astype(BF16), vwn_ref[...].astype(BF16)))

        for h in range(n_kv):
            r0 = h * rows_per_head
            c0 = h * HEAD_DIM
            gt = gt_ref[h]
            o_ref[h] = (gt[:, 0:1] * oc_ref[h, 0:rows_per_head, :]
                        + gt[:, 1:2] * o_s[r0:r0 + rows_per_head, c0:c0 + HEAD_DIM]
                        + gt[:, 2:3] * o_w[r0:r0 + rows_per_head, c0:c0 + HEAD_DIM])


def _slc_sample(page_table, cache_rows, q_pad, sel_steps, k_new, v_new, win_rows, kw_new, vw_new, o_c,
                gates, *, layer, n_kv, gqa, dec_seq, pages_per_step, name):
    batch, n_pages = page_table.shape
    page = cache_rows.shape[2]
    past = n_pages * page
    width = n_kv * HEAD_DIM
    n_kh = 2 * n_kv
    rph = gqa * dec_seq
    rows = n_kv * rph
    w_buf = win_rows.shape[2]
    n_new = k_new.shape[1]
    pps = pages_per_step
    n_steps = n_pages // pps
    ex = jnp.asarray(_block_to_key_matrix(1, sel_steps.shape[3], pps * page)[0], BF16)
    kern = functools.partial(_slc_s_kernel, n_kv=n_kv, n_steps=n_steps, pages_per_step=pps,
                             rows_per_head=rph, dec_seq=dec_seq, past=past, w_buf=w_buf)

    def new_spec():
        return pl.BlockSpec((None, n_new, width), lambda b, p, pt: (b, 0, 0))

    grid_spec = pltpu.PrefetchScalarGridSpec(
        num_scalar_prefetch=1,
        grid=(batch, n_steps),
        in_specs=_page_specs(page, n_kh, 1, layer, pps) + [
            pl.BlockSpec((None, n_kv, 128, HEAD_DIM), lambda b, p, pt: (b, 0, 0, 0)),
            pl.BlockSpec((None, None, rows, sel_steps.shape[3]), lambda b, p, pt: (b, p, 0, 0)),
            pl.BlockSpec(ex.shape, lambda b, p, pt: (0, 0)),
            new_spec(), new_spec(),
            pl.BlockSpec((None, None, w_buf, n_kh, HEAD_DIM), lambda b, p, pt: (layer, b, 0, 0, 0)),
            new_spec(), new_spec(),
            pl.BlockSpec((None, n_kv, 128, HEAD_DIM), lambda b, p, pt: (b, 0, 0, 0)),
            pl.BlockSpec((None, n_kv, rph, 8), lambda b, p, pt: (b, 0, 0, 0))],
        out_specs=pl.BlockSpec((None, n_kv, rph, HEAD_DIM), lambda b, p, pt: (b, 0, 0, 0)),
        scratch_shapes=[pltpu.VMEM((rows, width), F32),
                        pltpu.VMEM((rows, 1), F32),
                        pltpu.VMEM((rows, 1), F32),
                        pltpu.VMEM((rows, width), F32)],
    )
    return pl.pallas_call(
        kern,
        out_shape=jax.ShapeDtypeStruct((batch, n_kv, rph, HEAD_DIM), F32),
        grid_spec=grid_spec,
        compiler_params=_cparams("parallel", "arbitrary"),
        name=name,
    )(page_table, *([cache_rows] * pps), q_pad, sel_steps, ex, k_new, v_new, win_rows, kw_new,
      vw_new, o_c, gates)


def _rope_tables(pos):
    inv = ROPE_THETA ** (-jnp.arange(ROT_HALF, dtype=F32) * 2.0 / ROT_DIM)
    ang = pos.astype(F32)[:, None] * inv[None, :]
    cos, sin = jnp.cos(ang), jnp.sin(ang)
    rest = HEAD_DIM - ROT_DIM
    cos_t = jnp.concatenate([cos, cos, jnp.ones((pos.shape[0], rest), F32)], axis=1)
    sin_t = jnp.concatenate([-sin, sin, jnp.zeros((pos.shape[0], rest), F32)], axis=1)
    return cos_t, sin_t


def _prep_weights(w_in, w_o, ffn_w_in, ffn_w_down, ffn_conv_b, n_gate):
    w_gate = jnp.pad(w_in[:, :, w_in.shape[2] - n_gate:], ((0, 0), (0, 0), (0, 128 - n_gate)))
    return dict(w_in=w_in.astype(BF16), w_gate=w_gate.astype(BF16), w_o=w_o.astype(BF16),
                f_in=ffn_w_in.astype(BF16),
                f_down=ffn_w_down.astype(BF16),
                f_cb=ffn_conv_b.reshape(ffn_conv_b.shape[0], 1, ffn_conv_b.shape[1]))


def _prep_compress(l, cmp_pe, cmp_w1, cmp_w2):
    half = STRIDE * HEAD_DIM
    w1 = cmp_w1[l].reshape(2, 2 * half, HEAD_DIM)
    wl = {}
    wl["cmp_w1"] = w1.astype(BF16)
    wl["cmp_w1cat"] = jnp.concatenate([w1[:, :half], w1[:, half:]], axis=2).astype(BF16)
    wl["cmp_w2"] = cmp_w2[l].astype(BF16)
    pe = cmp_pe[l].reshape(2, 1, 2 * half)
    wl["cmp_pe"] = jnp.broadcast_to(pe, (2, 16, 2 * half)).astype(BF16)
    pe2 = cmp_pe[l].reshape(2, 2, 1, half)
    wl["cmp_pecat"] = jnp.concatenate([jnp.broadcast_to(pe2[:, 0], (2, 8, half)),
                                       jnp.broadcast_to(pe2[:, 1], (2, 8, half))], axis=1).astype(BF16)
    return wl


def _project(x_bf, ws, l, cos_t, sin_t, dims, tag):
    d_a, d_b, d_kv, n_heads, d_ff = dims
    n_kv = d_kv // HEAD_DIM
    w = ws["w_in"]
    uv = _matmul([x_bf], w, layer=l, col_start=0, n=2 * d_a, epilogue="gelu", tm=1024, tn=1024,
                 name=f"proj_uv_{tag}")
    (q2d,) = _rope_matmul(x_bf, w, cos_t, sin_t, layer=l, col_start=2 * d_a, n=d_b, tn=d_kv,
                          alternate=False, name=f"proj_q_{tag}")
    kv, nsa4, win4 = _rope_matmul(x_bf, w, cos_t, sin_t, layer=l, col_start=2 * d_a + d_b, n=6 * d_kv,
                                  tn=d_kv, alternate=True, n_kv=n_kv, name=f"proj_kv_{tag}")
    gates = _matmul([x_bf], ws["w_gate"], layer=l, epilogue="sigmoid", tm=1024, tn=128,
                    name=f"proj_gate_{tag}")
    return uv, q2d, kv, nsa4, win4, gates


def _mix_and_norm(x, a_out, b_out, ws, l, ln_g, ln_b, alpha, tag):
    mix = _matmul([a_out, b_out], ws["w_o"], layer=l, tm=1024, tn=1024, name=f"w_o_{tag}")
    return _add_ln(x, mix, ln_g[l, 0], ln_b[l, 0], alpha=alpha, name=f"ln1_{tag}")


def _layer_prompt(x, x_bf, ws, wl, l, p, cos_t, sin_t, dims, batch, seq, n_kv, alpha):
    d_a, d_b, d_kv, n_heads, d_ff = dims
    gqa = n_heads // n_kv
    m = batch * seq
    uv, q2d, kv, nsa4, win4, gates = _project(x_bf, ws, l, cos_t, sin_t, dims, "p")
    a_out, _ = _sgu(uv, p["sgu_w"][l], p["sgu_b"][l].T, p["sgu_g"][l], rows=CHUNK, name="sgu_p")
    kcmp, vcmp = _compress_prompt(kv, wl["cmp_w1"], wl["cmp_w2"], wl["cmp_pe"], batch=batch, seq=seq,
                                  n_kv=n_kv, name="compress_p")
    gates_h = gates[:, :3 * n_heads].reshape(m, 3, n_kv, gqa).transpose(2, 0, 1, 3).reshape(n_kv, m, 3 * gqa)
    b_out = _attn_prompt(q2d, kv, kcmp, vcmp, gates_h, batch=batch, seq=seq, n_heads=n_heads, n_kv=n_kv,
                         name="nsa_p")
    x1, x1b = _mix_and_norm(x, a_out, b_out, ws, l, p["ln_g"], p["ln_b"], alpha, "p")
    state0 = jnp.zeros((batch, CONV_W - 1, d_ff), F32)
    act, conv_new = _ffn_in_seq(x1b, ws["f_in"], p["ffn_conv_w"], ws["f_cb"], state0, layer=l,
                                seq_len=seq, name="ffn_in_p")
    f = _matmul([act], ws["f_down"], layer=l, tm=512, tn=512, name="ffn_down_p")
    x2, x2b = _add_ln(x1, f, p["ln_g"][l, 1], p["ln_b"][l, 1], alpha=alpha, name="ln2_p")
    new_nsa = nsa4.reshape(batch, seq, 4, n_kv, HEAD_DIM)
    new_win = win4.reshape(batch, seq, 2, n_kv, HEAD_DIM)[:, -min(WINDOW, seq):]
    return x2, x2b, new_nsa, new_win, conv_new


def _layer_sample(x, x_bf, ws, wl, l, p, cos_t, sin_t, dims, batch, dec_seq, n_kv, alpha, page_table,
                  cache_rows, win_rows, conv_state):
    d_a, d_b, d_kv, n_heads, d_ff = dims
    gqa = n_heads // n_kv
    m = batch * dec_seq
    rph = gqa * dec_seq
    uv, q2d, kv, nsa4, win4, gates = _project(x_bf, ws, l, cos_t, sin_t, dims, "s")
    w_small = p["sgu_w"][l][:, :dec_seq, :dec_seq]
    eye = jnp.eye(batch, dtype=F32)
    w_bd = jnp.einsum("ab,gts->gatbs", eye, w_small).reshape(-1, m, m)
    bt_bd = jnp.tile(p["sgu_b"][l].T[:dec_seq], (batch, 1))
    a_out, v_rows = _sgu(uv, w_bd, bt_bd, p["sgu_g"][l], rows=m, name="sgu_s")

    q = q2d.reshape(batch, dec_seq, n_kv, gqa, HEAD_DIM).transpose(0, 2, 3, 1, 4)
    q_pad = jnp.pad(q.reshape(batch, n_kv, rph, HEAD_DIM), ((0, 0), (0, 0), (0, 128 - rph), (0, 0)))
    o_c, sel = _cmp_sample(page_table, cache_rows, q_pad, wl["cmp_w1cat"], wl["cmp_w2"], wl["cmp_pecat"],
                           layer=l, n_kv=n_kv, gqa=gqa, dec_seq=dec_seq, name="cmp_s")
    n_pages = page_table.shape[1]
    bpp = cache_rows.shape[2] // L_SLC
    pps = _tile(n_pages, 4, 1)
    n_steps = n_pages // pps
    sel_steps = sel[:, :, :dec_seq, :n_pages * bpp].reshape(batch, n_kv, 1, dec_seq, n_steps, pps * bpp)
    sel_steps = jnp.broadcast_to(sel_steps, (batch, n_kv, gqa, dec_seq, n_steps, pps * bpp))
    sel_steps = sel_steps.transpose(0, 4, 1, 2, 3, 5).reshape(batch, n_steps, n_kv * rph, pps * bpp)
    sel_steps = jnp.pad(sel_steps, ((0, 0), (0, 0), (0, 0), (0, 128 - pps * bpp)))

    def new_rows(kind):
        rows = kv[:, kind * d_kv:(kind + 1) * d_kv].reshape(batch, dec_seq, d_kv)
        return jnp.pad(rows, ((0, 0), (0, 128 - dec_seq), (0, 0)))

    gates_s = gates[:, :3 * n_heads].reshape(batch, dec_seq, 3, n_kv, gqa).transpose(0, 3, 4, 1, 2)
    gates_s = jnp.pad(gates_s.reshape(batch, n_kv, rph, 3), ((0, 0), (0, 0), (0, 0), (0, 5)))
    b_rows = _slc_sample(page_table, cache_rows, q_pad, sel_steps, new_rows(2), new_rows(3), win_rows,
                         new_rows(4), new_rows(5), o_c, gates_s, layer=l, n_kv=n_kv, gqa=gqa,
                         dec_seq=dec_seq, pages_per_step=pps, name="slc_s")
    b_out = b_rows.reshape(batch, n_kv, gqa, dec_seq, HEAD_DIM).transpose(0, 3, 1, 2, 4)
    b_out = b_out.reshape(m, d_b).astype(BF16)

    x1, x1b = _mix_and_norm(x, a_out, b_out, ws, l, p["ln_g"], p["ln_b"], alpha, "s")
    st = conv_state
    zero = jnp.zeros((batch, dec_seq - 1, d_ff), F32)
    h1 = jnp.concatenate([st[:, 1:2], zero], axis=1).reshape(m, d_ff)
    h2 = jnp.concatenate([st, zero[:, 1:]], axis=1).reshape(m, d_ff)
    act, gate = _ffn_in_short(x1b, ws["f_in"], p["ffn_conv_w"], ws["f_cb"], h1, h2, layer=l,
                              seq_len=dec_seq, name="ffn_in_s")
    f = _matmul([act], ws["f_down"], layer=l, tn=512, name="ffn_down_s")
    x2, x2b = _add_ln(x1, f, p["ln_g"][l, 1], p["ln_b"][l, 1], alpha=alpha, name="ln2_s")
    new_nsa = nsa4.reshape(batch, dec_seq, 4, n_kv, HEAD_DIM)
    new_win = win4.reshape(batch, dec_seq, 2, n_kv, HEAD_DIM)
    conv_new = gate.reshape(batch, dec_seq, d_ff)[:, dec_seq - (CONV_W - 1):]
    return x2, x2b, new_nsa, new_win, v_rows.reshape(batch, dec_seq, d_a), conv_new


def kernel(x_prompt, x_sample, cache_nsa_kv, cache_win_kv, state_ffn_conv, page_table, w_in, sgu_w,
           sgu_b, sgu_g, cmp_pe, cmp_w1, cmp_w2, w_o, ln_g, ln_b, ffn_w_in, ffn_conv_w, ffn_conv_b,
           ffn_w_down):
    bp, seq, d_model = x_prompt.shape
    bs, dec_seq, _ = x_sample.shape
    depth = w_in.shape[0]
    n_kv = cache_nsa_kv.shape[4]
    page = cache_nsa_kv.shape[2]
    past = page_table.shape[1] * page
    d_a = d_model // 2
    d_b = d_model - d_a
    n_heads = d_b // HEAD_DIM
    d_kv = n_kv * HEAD_DIM
    d_ff = ffn_conv_w.shape[-1]
    dims = (d_a, d_b, d_kv, n_heads, d_ff)
    alpha = (2 * depth) ** 0.25
    assert dec_seq >= CONV_W - 1 and seq % CHUNK == 0

    cos_p, sin_p = _rope_tables(jnp.tile(jnp.arange(seq, dtype=jnp.int32), bp))
    cos_s, sin_s = _rope_tables(jnp.tile(past + jnp.arange(dec_seq, dtype=jnp.int32), bs))
    cache_rows = cache_nsa_kv.reshape(depth, cache_nsa_kv.shape[1], page, 4 * n_kv, HEAD_DIM)
    win_rows = cache_win_kv.reshape(depth, bs, cache_win_kv.shape[2], 2 * n_kv, HEAD_DIM)
    ws = _prep_weights(w_in, w_o, ffn_w_in, ffn_w_down, ffn_conv_b, 3 * n_heads)
    p = dict(sgu_w=sgu_w, sgu_b=sgu_b, sgu_g=sgu_g, ln_g=ln_g, ln_b=ln_b, ffn_conv_w=ffn_conv_w)

    xp = x_prompt.reshape(bp * seq, d_model)
    xs = x_sample.reshape(bs * dec_seq, d_model)
    xp_bf, xs_bf = xp.astype(BF16), xs.astype(BF16)
    outs = [[] for _ in range(7)]
    for l in range(depth):
        wl = _prep_compress(l, cmp_pe, cmp_w1, cmp_w2)
        xp, xp_bf, nsa_p, win_p, conv_p = _layer_prompt(
            xp, xp_bf, ws, wl, l, p, cos_p, sin_p, dims, bp, seq, n_kv, alpha)
        xs, xs_bf, nsa_s, win_s, v_s, conv_s = _layer_sample(
            xs, xs_bf, ws, wl, l, p, cos_s, sin_s, dims, bs, dec_seq, n_kv, alpha, page_table,
            cache_rows, win_rows, state_ffn_conv[l])
        for acc, val in zip(outs, (nsa_p, nsa_s, win_p, win_s, v_s, conv_p, conv_s)):
            acc.append(val)
    return (xp.reshape(bp, seq, d_model), xs.reshape(bs, dec_seq, d_model),
            *[jnp.stack(o) for o in outs])
```

```python
import functools
import math

import jax
import jax.numpy as jnp
import numpy as np
from jax import lax
from jax.experimental import pallas as pl
from jax.experimental.pallas import tpu as pltpu

HEAD_DIM = 128
CHUNK = 128
STRIDE = 16
L_CMP = 2 * STRIDE
L_SLC = 64
N_SEL = 16
WINDOW = 512
ROT_DIM = HEAD_DIM // 4
ROT_HALF = ROT_DIM // 2
ROPE_THETA = 500000.0
CONV_W = 3
LN_EPS = 1e-5
SCALE = HEAD_DIM ** -0.5
LOGIT_SCALE = SCALE * math.log2(math.e)
SEL_FORCE = 1e9
NEG_BIG = -3.0e38
VMEM_LIMIT = 56 * 1024 * 1024

F32 = jnp.float32
BF16 = jnp.bfloat16


def _cparams(*sem):
    return pltpu.CompilerParams(dimension_semantics=sem, vmem_limit_bytes=VMEM_LIMIT)


def _tile(n, pref, unit=128):
    if n <= pref:
        return n
    t = (pref // unit) * unit
    while t > unit and n % t:
        t -= unit
    assert n % t == 0, (n, pref, unit)
    return t


def _gelu(x):
    return jax.nn.gelu(x, approximate=True)


def _dot(a, b):
    return jnp.dot(a, b, preferred_element_type=F32)


def _dot_nt(a, b):
    return lax.dot_general(a, b, (((1,), (1,)), ((), ())), preferred_element_type=F32)


def _split_hi_lo(x):
    hi = x.astype(BF16)
    lo = (x - hi.astype(F32)).astype(BF16)
    return hi, lo


def _masked_softmax2(s, mask, axis=-1):
    sm = jnp.where(mask, s, NEG_BIG)
    m = jnp.max(sm, axis=axis, keepdims=True)
    p = jnp.where(mask, jnp.exp2(sm - m), 0.0)
    return p / jnp.maximum(jnp.sum(p, axis=axis, keepdims=True), 1e-30)


def _mm_kernel(*refs, n_lhs, epilogue, emit, res_scale):
    n_in = 2 * n_lhs + (res_scale is not None)
    ws = [refs[n_lhs + k][...] for k in range(n_lhs)]
    if emit:
        ws = [w.astype(BF16) for w in ws]
        for k in range(n_lhs):
            refs[n_in + 1 + k][...] = ws[k]
    acc = _dot(refs[0][...], ws[0])
    for k in range(1, n_lhs):
        acc = acc + _dot(refs[k][...], ws[k])
    if res_scale is not None:
        acc = res_scale * refs[n_in - 1][...] + acc
    o_ref = refs[n_in]
    if epilogue == "gelu":
        acc = _gelu(acc)
    elif epilogue == "sigmoid":
        acc = jax.nn.sigmoid(acc)
    o_ref[...] = acc.astype(o_ref.dtype)


def _matmul(xs, w, *, layer, col_start=0, n=None, epilogue="none", out_dtype=F32, tm=512, tn=512,
            emit=False, residual=None, res_scale=None, name):
    m, kdim = xs[0].shape
    n = (w[0] if isinstance(w, (list, tuple)) else w).shape[2] if n is None else n
    tm = _tile(m, tm, 8)
    tn = _tile(n, tn)
    assert col_start % tn == 0 and all(x.shape == (m, kdim) for x in xs)
    c0 = col_start // tn
    n_lhs = len(xs)
    in_specs = [pl.BlockSpec((tm, kdim), lambda i, j: (i, 0)) for _ in xs]
    separate = isinstance(w, (list, tuple))
    w_list = list(w) if separate else [w] * n_lhs
    in_specs += [pl.BlockSpec((None, kdim, tn), lambda i, j, r=r: (layer, 0 if separate else r, c0 + j))
                 for r in range(n_lhs)]
    extra = []
    if residual is not None:
        in_specs.append(pl.BlockSpec((tm, tn), lambda i, j: (i, j)))
        extra = [residual]
    out_shape = [jax.ShapeDtypeStruct((m, n), out_dtype)]
    out_specs = [pl.BlockSpec((tm, tn), lambda i, j: (i, j))]
    if emit:
        assert m == tm
        out_shape += [jax.ShapeDtypeStruct((1, kdim, n), BF16)] * n_lhs
        out_specs += [pl.BlockSpec((None, kdim, tn), lambda i, j: (0, 0, j))] * n_lhs
    out = pl.pallas_call(
        functools.partial(_mm_kernel, n_lhs=n_lhs, epilogue=epilogue, emit=emit,
                          res_scale=res_scale if residual is not None else None),
        out_shape=tuple(out_shape),
        grid=(m // tm, n // tn),
        in_specs=in_specs,
        out_specs=tuple(out_specs),
        compiler_params=_cparams("parallel", "arbitrary"),
        name=name,
    )(*xs, *w_list, *extra)
    return out if emit else out[0]


def _rope_mm_kernel(x_ref, w_ref, cos_ref, sin_ref, o_ref, *more_refs, heads_per_tile, alternate,
                    n_nsa_kinds, emit):
    j = pl.program_id(1)
    w = w_ref[...]
    cache_refs = more_refs
    if emit:
        w = w.astype(BF16)
        more_refs[-1][...] = w
        cache_refs = more_refs[:-1]
    acc = _dot(x_ref[...], w)
    cosv, sinv = cos_ref[...], sin_ref[...]
    if alternate:
        rot = lax.rem(j, 2) == 0
        cosv = jnp.where(rot, cosv, 1.0)
        sinv = jnp.where(rot, sinv, 0.0)
    lane = lax.broadcasted_iota(jnp.int32, cosv.shape, 1)
    heads = []
    for h in range(heads_per_tile):
        hs = acc[:, h * HEAD_DIM:(h + 1) * HEAD_DIM]
        partner = jnp.where(lane < ROT_HALF,
                            pltpu.roll(hs, HEAD_DIM - ROT_HALF, 1),
                            pltpu.roll(hs, ROT_HALF, 1))
        heads.append(hs * cosv + partner * sinv)
        o_ref[:, h * HEAD_DIM:(h + 1) * HEAD_DIM] = heads[h]
    if cache_refs:
        nsa_ref, win_ref = cache_refs

        @pl.when(j < n_nsa_kinds)
        def _():
            for h in range(heads_per_tile):
                nsa_ref[:, h, :] = heads[h]

        @pl.when(j >= n_nsa_kinds)
        def _():
            for h in range(heads_per_tile):
                win_ref[:, h, :] = heads[h]


def _rope_matmul(x, w, cos_t, sin_t, *, layer, col_start, n, tn, alternate, n_kv=None, tm=1024,
                 emit=False, name):
    m, k = x.shape
    tm = _tile(m, tm, 8)
    assert col_start % tn == 0 and n % tn == 0
    c0 = col_start // tn
    hpt = tn // HEAD_DIM
    out_shape = [jax.ShapeDtypeStruct((m, n), F32)]
    out_specs = [pl.BlockSpec((tm, tn), lambda i, j: (i, j))]
    n_nsa = 4
    if alternate:
        assert hpt == n_kv and n == 6 * tn
        out_shape += [jax.ShapeDtypeStruct((m, n_nsa, n_kv, HEAD_DIM), F32),
                      jax.ShapeDtypeStruct((m, 2, n_kv, HEAD_DIM), F32)]
        out_specs += [pl.BlockSpec((tm, None, n_kv, HEAD_DIM),
                                   lambda i, j: (i, jnp.minimum(j, n_nsa - 1), 0, 0)),
                      pl.BlockSpec((tm, None, n_kv, HEAD_DIM),
                                   lambda i, j: (i, jnp.maximum(j - n_nsa, 0), 0, 0))]
    if emit:
        assert m == tm
        out_shape += [jax.ShapeDtypeStruct((1, k, n), BF16)]
        out_specs += [pl.BlockSpec((None, k, tn), lambda i, j: (0, 0, j))]
    kern = functools.partial(_rope_mm_kernel, heads_per_tile=hpt, alternate=alternate,
                             n_nsa_kinds=n_nsa, emit=emit)
    return pl.pallas_call(
        kern,
        out_shape=tuple(out_shape),
        grid=(m // tm, n // tn),
        in_specs=[pl.BlockSpec((tm, k), lambda i, j: (i, 0)),
                  pl.BlockSpec((None, k, tn), lambda i, j: (layer, 0, c0 + j)),
                  pl.BlockSpec((tm, HEAD_DIM), lambda i, j: (i, 0)),
                  pl.BlockSpec((tm, HEAD_DIM), lambda i, j: (i, 0))],
        out_specs=tuple(out_specs),
        compiler_params=_cparams("arbitrary", "arbitrary"),
        name=name,
    )(x, w, cos_t, sin_t)


def _ln_kernel(z_ref, g_ref, b_ref, y_ref, yb_ref):
    z = z_ref[...]
    mu = jnp.mean(z, axis=-1, keepdims=True)
    zc = z - mu
    var = jnp.mean(zc * zc, axis=-1, keepdims=True)
    y = zc * lax.rsqrt(var + LN_EPS) * g_ref[...] + b_ref[...]
    y_ref[...] = y
    yb_ref[...] = y.astype(BF16)


def _layer_norm(z, g, b, *, name):
    m, d = z.shape
    tr = _tile(m, 256, 8)
    return pl.pallas_call(
        _ln_kernel,
        out_shape=(jax.ShapeDtypeStruct((m, d), F32), jax.ShapeDtypeStruct((m, d), BF16)),
        grid=(m // tr,),
        in_specs=[pl.BlockSpec((tr, d), lambda i: (i, 0)),
                  pl.BlockSpec((1, d), lambda i: (0, 0)),
                  pl.BlockSpec((1, d), lambda i: (0, 0))],
        out_specs=(pl.BlockSpec((tr, d), lambda i: (i, 0)),
                   pl.BlockSpec((tr, d), lambda i: (i, 0))),
        compiler_params=_cparams("parallel"),
        name=name,
    )(z, g.reshape(1, d), b.reshape(1, d))


def _sgu_kernel(uv_ref, w_ref, bt_ref, g_ref, a_ref, vn_ref, *, d_a, n_groups):
    rows = w_ref.shape[1]
    r = lax.broadcasted_iota(jnp.int32, (rows, rows), 0)
    c = lax.broadcasted_iota(jnp.int32, (rows, rows), 1)
    causal = r >= c
    for g in range(n_groups):
        lo = g * HEAD_DIM
        v = uv_ref[:, d_a + lo:d_a + lo + HEAD_DIM]
        mu = jnp.mean(v, axis=-1, keepdims=True)
        vc = v - mu
        var = jnp.mean(vc * vc, axis=-1, keepdims=True)
        vn = vc * lax.rsqrt(var + LN_EPS) * g_ref[:, lo:lo + HEAD_DIM]
        vn_ref[:, lo:lo + HEAD_DIM] = vn
        w = jnp.where(causal, w_ref[g], 0.0).astype(BF16)
        mixed = _dot(w, vn.astype(BF16)) + bt_ref[:, g:g + 1]
        a_ref[:, lo:lo + HEAD_DIM] = (uv_ref[:, lo:lo + HEAD_DIM] * mixed).astype(a_ref.dtype)


def _sgu(uv, w, bt, gain, *, rows, name):
    m = uv.shape[0]
    d_a = uv.shape[1] // 2
    n_groups = d_a // HEAD_DIM
    return pl.pallas_call(
        functools.partial(_sgu_kernel, d_a=d_a, n_groups=n_groups),
        out_shape=(jax.ShapeDtypeStruct((m, d_a), BF16), jax.ShapeDtypeStruct((m, d_a), F32)),
        grid=(m // rows,),
        in_specs=[pl.BlockSpec((rows, 2 * d_a), lambda i: (i, 0)),
                  pl.BlockSpec((n_groups, rows, rows), lambda i: (0, 0, 0)),
                  pl.BlockSpec((rows, n_groups), lambda i: (0, 0)),
                  pl.BlockSpec((1, d_a), lambda i: (0, 0))],
        out_specs=(pl.BlockSpec((rows, d_a), lambda i: (i, 0)),
                   pl.BlockSpec((rows, d_a), lambda i: (i, 0))),
        compiler_params=_cparams("parallel"),
        name=name,
    )(uv, w, bt, gain.reshape(1, d_a))


def _conv_act(gate, g1, g2, up, cw_ref, cb_ref):
    c = cb_ref[...] + g2 * cw_ref[0:1, :] + g1 * cw_ref[1:2, :] + gate * cw_ref[2:3, :]
    return _gelu(c) * up


def _ffn_in_seq_kernel(x_ref, *refs, tiles_per_seq, n_sub):
    wg, wu, cw, cb, st = (refs[k * n_sub:(k + 1) * n_sub] for k in range(5))
    a_ref, cn_ref, carry_ref = refs[5 * n_sub:]
    i = pl.program_id(1)
    x = x_ref[...]
    tm = x.shape[0]
    tn = wg[0].shape[1]
    first = lax.rem(i, tiles_per_seq) == 0
    row = lax.broadcasted_iota(jnp.int32, (tm, tn), 0)
    for k in range(n_sub):
        cols = slice(k * tn, (k + 1) * tn)
        gate = _dot(x, wg[k][...])
        up = _dot(x, wu[k][...])
        prev2 = jnp.where(first, st[k][0:1, :], carry_ref[0:1, cols])
        prev1 = jnp.where(first, st[k][1:2, :], carry_ref[1:2, cols])
        g1 = jnp.where(row == 0, prev1, pltpu.roll(gate, 1, 0))
        g2 = jnp.where(row == 0, prev2, jnp.where(row == 1, prev1, pltpu.roll(gate, 2, 0)))
        a_ref[:, cols] = _conv_act(gate, g1, g2, up, cw[k], cb[k]).astype(a_ref.dtype)
        tail = gate[tm - 2:tm, :]
        carry_ref[0:2, cols] = tail
        cn_ref[:, cols] = tail


def _ffn_in_seq(x, wg, wu, cw, cb, state, *, layer, seq_len, tm=1024, tn=256, n_sub=2, name):
    m, k = x.shape
    d_ff = cw.shape[2]
    tm = _tile(seq_len, tm, 8)
    tn = _tile(d_ff, tn)
    n_tiles = d_ff // tn
    tps = seq_len // tm
    n_seq = m // seq_len

    def col(j, s):
        return jnp.minimum(j * n_sub + s, n_tiles - 1)

    subs = range(n_sub)
    in_specs = [pl.BlockSpec((tm, k), lambda j, i: (i, 0))]
    in_specs += [pl.BlockSpec((None, k, tn), lambda j, i, s=s: (0, 0, col(j, s))) for s in subs] * 2
    in_specs += [pl.BlockSpec((None, CONV_W, tn), lambda j, i, s=s: (layer, 0, col(j, s))) for s in subs]
    in_specs += [pl.BlockSpec((None, 1, tn), lambda j, i, s=s: (layer, 0, col(j, s))) for s in subs]
    in_specs += [pl.BlockSpec((None, CONV_W - 1, tn), lambda j, i, s=s: (i // tps, 0, col(j, s)))
                 for s in subs]
    return pl.pallas_call(
        functools.partial(_ffn_in_seq_kernel, tiles_per_seq=tps, n_sub=n_sub),
        out_shape=(jax.ShapeDtypeStruct((m, d_ff), BF16),
                   jax.ShapeDtypeStruct((n_seq, CONV_W - 1, d_ff), F32)),
        grid=(pl.cdiv(n_tiles, n_sub), m // tm),
        in_specs=in_specs,
        out_specs=(pl.BlockSpec((tm, n_sub * tn), lambda j, i: (i, j)),
                   pl.BlockSpec((None, CONV_W - 1, n_sub * tn), lambda j, i: (i // tps, 0, j))),
        scratch_shapes=[pltpu.VMEM((8, n_sub * tn), F32)],
        compiler_params=_cparams("arbitrary", "arbitrary"),
        name=name,
    )(x, *([wg] * n_sub), *([wu] * n_sub), *([cw] * n_sub), *([cb] * n_sub), *([state] * n_sub))


def _ffn_in_short_kernel(x_ref, wg_ref, wu_ref, cw_ref, cb_ref, h1_ref, h2_ref, a_ref, gate_ref,
                         wgb_ref, wub_ref, *, seq_len):
    x = x_ref[...]
    wg = wg_ref[...].astype(BF16)
    wu = wu_ref[...].astype(BF16)
    wgb_ref[...] = wg
    wub_ref[...] = wu
    gate = _dot(x, wg)
    up = _dot(x, wu)
    t = lax.rem(lax.broadcasted_iota(jnp.int32, gate.shape, 0), seq_len)
    g1 = jnp.where(t >= 1, pltpu.roll(gate, 1, 0), h1_ref[...])
    g2 = jnp.where(t >= 2, pltpu.roll(gate, 2, 0), h2_ref[...])
    a_ref[...] = _conv_act(gate, g1, g2, up, cw_ref, cb_ref).astype(a_ref.dtype)
    gate_ref[...] = gate


def _ffn_in_short(x, w, cw, cb, h1, h2, *, layer, seq_len, tn=256, name):
    m, k = x.shape
    d_ff = cw.shape[2]
    tn = _tile(d_ff, tn)
    n_tiles = d_ff // tn
    return pl.pallas_call(
        functools.partial(_ffn_in_short_kernel, seq_len=seq_len),
        out_shape=(jax.ShapeDtypeStruct((m, d_ff), BF16), jax.ShapeDtypeStruct((m, d_ff), F32),
                   jax.ShapeDtypeStruct((1, k, d_ff), BF16), jax.ShapeDtypeStruct((1, k, d_ff), BF16)),
        grid=(n_tiles,),
        in_specs=[pl.BlockSpec((m, k), lambda j: (0, 0)),
                  pl.BlockSpec((None, k, tn), lambda j: (layer, 0, j)),
                  pl.BlockSpec((None, k, tn), lambda j: (layer, 0, n_tiles + j)),
                  pl.BlockSpec((None, CONV_W, tn), lambda j: (layer, 0, j)),
                  pl.BlockSpec((None, 1, tn), lambda j: (layer, 0, j)),
                  pl.BlockSpec((m, tn), lambda j: (0, j)),
                  pl.BlockSpec((m, tn), lambda j: (0, j))],
        out_specs=(pl.BlockSpec((m, tn), lambda j: (0, j)),
                   pl.BlockSpec((m, tn), lambda j: (0, j)),
                   pl.BlockSpec((None, k, tn), lambda j: (0, 0, j)),
                   pl.BlockSpec((None, k, tn), lambda j: (0, 0, j))),
        compiler_params=_cparams("parallel"),
        name=name,
    )(x, w, w, cw, cb, h1, h2)


def _compress_rows(xa, w1a, w1b, w2, hpe):
    a = _dot(xa, w1a)
    b = _dot(xa, w1b)
    n = a.shape[0]
    h = a + pltpu.roll(b, n - 1, 0) + hpe
    return _dot(_gelu(h).astype(BF16), w2)


def _pos_embed_term(pe_ref, w1_ref, kind):
    return _dot(pe_ref[kind], w1_ref[kind])[0:1, :]


def _compress_p_kernel(k_ref, v_ref, w1_ref, w2_ref, pe_ref, kc_ref, vc_ref, *, n_chunks):
    half = STRIDE * HEAD_DIM
    for kind, (src, dst) in enumerate(((k_ref, kc_ref), (v_ref, vc_ref))):
        xa = jnp.concatenate(
            [src[pl.ds(s, n_chunks, stride=STRIDE), :] for s in range(STRIDE)], axis=1).astype(BF16)
        hpe = _pos_embed_term(pe_ref, w1_ref, kind)
        out = _compress_rows(xa, w1_ref[kind, 0:half, :], w1_ref[kind, half:2 * half, :],
                             w2_ref[kind], hpe)
        dst[...] = out.astype(dst.dtype)


def _compress_prompt(kv, w1, w2, pe, *, batch, seq, n_kv, name):
    n_chunks = seq // STRIDE
    out = jax.ShapeDtypeStruct((batch, n_kv, n_chunks, HEAD_DIM), BF16)
    ospec = pl.BlockSpec((None, None, n_chunks, HEAD_DIM), lambda b, h: (b, h, 0, 0))
    return pl.pallas_call(
        functools.partial(_compress_p_kernel, n_chunks=n_chunks),
        out_shape=(out, out),
        grid=(batch, n_kv),
        in_specs=[pl.BlockSpec((seq, HEAD_DIM), lambda b, h: (b, h)),
                  pl.BlockSpec((seq, HEAD_DIM), lambda b, h: (b, n_kv + h)),
                  pl.BlockSpec(w1.shape, lambda b, h: (0, 0, 0)),
                  pl.BlockSpec(w2.shape, lambda b, h: (0, 0, 0)),
                  pl.BlockSpec(pe.shape, lambda b, h: (0, 0, 0))],
        out_specs=(ospec, ospec),
        compiler_params=_cparams("parallel", "parallel"),
        name=name,
    )(kv, kv, w1, w2, pe)


def _block_scores(imp, pos, n_blocks, block_axis):
    j = lax.broadcasted_iota(jnp.int32, imp.shape, block_axis)
    cur = pos // L_SLC
    forced = (j == 0) | (j == cur) | (j == cur - 1)
    valid = j * L_SLC <= pos
    score = jnp.where(valid, jnp.where(forced, SEL_FORCE, imp), -SEL_FORCE)
    return jnp.where(j < n_blocks, score, NEG_BIG)


def _overlap_matrix(nc, nb, rows, cols):
    i = np.arange(nc)[:, None]
    j = np.arange(nb)[None, :]
    lo = np.maximum(i * STRIDE, j * L_SLC)
    hi = np.minimum(i * STRIDE + L_CMP, (j + 1) * L_SLC)
    ov = np.zeros((rows, cols), np.float32)
    ov[:nc, :nb] = np.maximum(hi - lo, 0) / STRIDE
    return ov


def _block_to_key_matrix(n_groups, lanes, keys_per_group):
    ex = np.zeros((n_groups, lanes, keys_per_group), np.float32)
    for c in range(n_groups):
        k = np.arange(keys_per_group)
        ex[c, (c * keys_per_group + k) // L_SLC, k] = 1.0
    return ex


def _attn_p_kernel(q_ref, kc_ref, vc_ref, ks_ref, vs_ref, kw_ref, vw_ref, gt_ref, ovt_ref, ex_ref,
                   o_ref, m_scr, acc_scr, *, tq, tk, gqa, n_blocks, n_cmp, win_keys):
    qi = pl.program_id(2)
    t0 = qi * tq
    q = q_ref[...] * LOGIT_SCALE
    qs = [q[:, g * HEAD_DIM:(g + 1) * HEAD_DIM].astype(BF16) for g in range(gqa)]
    pos_t = t0 + lax.broadcasted_iota(jnp.int32, (tq, 1), 0)
    ones = jnp.ones((max(tk, win_keys), HEAD_DIM), BF16)

    kc = kc_ref[...]
    vc = vc_ref[...]
    n = lax.broadcasted_iota(jnp.int32, (tq, kc.shape[0]), 1)
    last_end = jnp.minimum(pos_t, (n_cmp - 1) * STRIDE + (L_CMP - 1))
    bias_c = jnp.where(n * STRIDE + (L_CMP - 1) <= last_end, 0.0, NEG_BIG)
    o_c = []
    p_grp = None
    for g in range(gqa):
        sm = _dot_nt(qs[g], kc) + bias_c
        m = jnp.max(sm, axis=-1, keepdims=True)
        e = jnp.exp2(sm - m)
        norm = jnp.where(m > 0.5 * NEG_BIG,
                         1.0 / jnp.maximum(jnp.sum(e, axis=-1, keepdims=True), 1e-30), 0.0)
        p = e * norm
        o_c.append(_dot(p.astype(BF16), vc))
        p_grp = p if p_grp is None else p_grp + p

    hi, lo = _split_hi_lo(p_grp)
    imp_t = _dot_nt(ovt_ref[...], hi) + _dot_nt(ovt_ref[...], lo)
    pos_row = t0 + lax.broadcasted_iota(jnp.int32, (1, tq), 1)
    score = _block_scores(imp_t, pos_row, n_blocks, 0)
    blk = lax.broadcasted_iota(jnp.int32, score.shape, 0)
    rank = jnp.zeros(score.shape, F32)
    for i in range(n_blocks):
        ci = score[i:i + 1, :]
        rank = rank + jnp.where((ci > score) | ((ci == score) & (blk > i)), 1.0, 0.0)
    sel_t = jnp.where(rank < float(min(N_SEL, n_blocks)), 1.0, 0.0)
    lanes = ex_ref.shape[1]
    sel_t = jnp.concatenate([sel_t, jnp.zeros((lanes - sel_t.shape[0], tq), F32)], axis=0)
    sel = sel_t.T.astype(BF16)

    m_scr[...] = jnp.full(m_scr.shape, NEG_BIG, F32)
    acc_scr[...] = jnp.zeros(acc_scr.shape, F32)

    def key_tile(c, carry):
        k0 = pl.multiple_of(c * tk, tk)
        kt = ks_ref[pl.ds(k0, tk), :].astype(BF16)
        vt = jnp.concatenate([vs_ref[pl.ds(k0, tk), :].astype(BF16), ones[0:tk]], axis=1)
        sel_keys = _dot(sel, ex_ref[c])
        kpos = k0 + lax.broadcasted_iota(jnp.int32, sel_keys.shape, 1)
        bias = jnp.where(sel_keys > jnp.where(kpos <= pos_t, 0.5, 2.0), 0.0, NEG_BIG)
        for g in range(gqa):
            rows = slice(g * tq, (g + 1) * tq)
            sm = _dot_nt(qs[g], kt) + bias
            m_old = m_scr[rows]
            m_new = jnp.maximum(m_old, jnp.max(sm, axis=-1, keepdims=True))
            e = jnp.exp2(sm - m_new)
            acc_scr[rows] = jnp.exp2(m_old - m_new) * acc_scr[rows] + _dot(e.astype(BF16), vt)
            m_scr[rows] = m_new
        return carry

    lax.fori_loop(0, (t0 + tq - 1) // tk + 1, key_tile, 0)

    start = pl.multiple_of(jnp.maximum(t0 + tq - win_keys, 0), 128)
    kw = kw_ref[pl.ds(start, win_keys), :].astype(BF16)
    vw = jnp.concatenate([vw_ref[pl.ds(start, win_keys), :].astype(BF16), ones[0:win_keys]], axis=1)
    d = pos_t - (start + lax.broadcasted_iota(jnp.int32, (tq, win_keys), 1))
    bias_w = jnp.where(d >= 0, jnp.where(d < WINDOW, 0.0, NEG_BIG), NEG_BIG)
    gt = gt_ref[...]
    for g in range(gqa):
        sm = _dot_nt(qs[g], kw) + bias_w
        e = jnp.exp2(sm - jnp.max(sm, axis=-1, keepdims=True))
        ow = _dot(e.astype(BF16), vw)
        o_w = ow[:, 0:HEAD_DIM] / ow[:, HEAD_DIM:HEAD_DIM + 1]
        acc = acc_scr[g * tq:(g + 1) * tq]
        o_s = acc[:, 0:HEAD_DIM] / jnp.maximum(acc[:, HEAD_DIM:HEAD_DIM + 1], 1e-30)
        out = (gt[:, g:g + 1] * o_c[g] + gt[:, gqa + g:gqa + g + 1] * o_s
               + gt[:, 2 * gqa + g:2 * gqa + g + 1] * o_w)
        o_ref[:, g * HEAD_DIM:(g + 1) * HEAD_DIM] = out.astype(o_ref.dtype)


def _attn_prompt(q2d, kv, kcmp, vcmp, gates_h, *, batch, seq, n_heads, n_kv, tq=256, tk=512, name):
    gqa = n_heads // n_kv
    tq = _tile(seq, tq, 8)
    tk = _tile(seq, tk)
    nq = seq // tq
    n_chunks = seq // STRIDE
    n_cmp = n_chunks - 1
    n_blocks = -(-seq // L_SLC)
    lanes = -(-n_blocks // 128) * 128
    block_rows = -(-n_blocks // 8) * 8
    ovt = jnp.asarray(_overlap_matrix(n_cmp, n_blocks, n_chunks, block_rows).T.copy(), BF16)
    ex = jnp.asarray(_block_to_key_matrix(seq // tk, lanes, tk), BF16)
    win_keys = min(WINDOW + tq, seq)
    rows = gqa * tq
    kern = functools.partial(_attn_p_kernel, tq=tq, tk=tk, gqa=gqa, n_blocks=n_blocks,
                             n_cmp=n_cmp, win_keys=win_keys)

    def kv_spec(kind):
        return pl.BlockSpec((seq, HEAD_DIM), lambda b, h, i: (b, kind * n_kv + h))

    cmp_spec = pl.BlockSpec((None, None, n_chunks, HEAD_DIM), lambda b, h, i: (b, h, 0, 0))
    return pl.pallas_call(
        kern,
        out_shape=jax.ShapeDtypeStruct((batch * seq, n_heads * HEAD_DIM), BF16),
        grid=(batch, n_kv, nq),
        in_specs=[pl.BlockSpec((tq, gqa * HEAD_DIM), lambda b, h, i: (b * nq + i, h)),
                  cmp_spec, cmp_spec,
                  kv_spec(2), kv_spec(3), kv_spec(4), kv_spec(5),
                  pl.BlockSpec((None, tq, 3 * gqa), lambda b, h, i: (h, b * nq + i, 0)),
                  pl.BlockSpec(ovt.shape, lambda b, h, i: (0, 0)),
                  pl.BlockSpec(ex.shape, lambda b, h, i: (0, 0, 0))],
        out_specs=pl.BlockSpec((tq, gqa * HEAD_DIM), lambda b, h, i: (b * nq + i, h)),
        scratch_shapes=[pltpu.VMEM((rows, 1), F32),
                        pltpu.VMEM((rows, 2 * HEAD_DIM), F32)],
        compiler_params=_cparams("parallel", "parallel", "arbitrary"),
        name=name,
    )(q2d, kcmp, vcmp, kv, kv, kv, kv, gates_h, ovt, ex)


def _head_slabs(rows_ref, first, n_heads):
    return jnp.concatenate([rows_ref[:, first + h, :] for h in range(n_heads)], axis=1)


def _page_copies(cache_ref, buf_ref, sem_ref, pt_ref, b, step, slot, *, layer, row0, n_rows,
                 pages_per_step):
    out = []
    for k in range(pages_per_step):
        page = pt_ref[b, step * pages_per_step + k]
        for r in range(n_rows):
            out.append(pltpu.make_async_copy(cache_ref.at[layer, page, :, row0 + r, :],
                                             buf_ref.at[slot, k, r], sem_ref.at[slot]))
    return out


def _page_stream(cache_ref, buf_ref, sem_ref, pt_ref, **kw):
    b, p = pl.program_id(0), pl.program_id(1)
    n_b, n_p = pl.num_programs(0), pl.num_programs(1)
    g = b * n_p + p
    slot = lax.rem(g, 2)

    @pl.when(g == 0)
    def _():
        for c in _page_copies(cache_ref, buf_ref, sem_ref, pt_ref, 0, 0, 0, **kw):
            c.start()

    @pl.when(g + 1 < n_b * n_p)
    def _():
        wrap = p + 1 == n_p
        for c in _page_copies(cache_ref, buf_ref, sem_ref, pt_ref, jnp.where(wrap, b + 1, b),
                              jnp.where(wrap, 0, p + 1), 1 - slot, **kw):
            c.start()

    for c in _page_copies(cache_ref, buf_ref, sem_ref, pt_ref, b, p, slot, **kw):
        c.wait()
    return slot


def _cmp_s_kernel(pt_ref, cache_ref, q_ref, w1_ref, w2_ref, pe_ref, ovt_ref, oc_ref, sel_ref,
                  x_scr, ab_scr, buf_ref, sem_ref, *, layer, n_kv, n_steps, pages_per_step,
                  steps_per_group, n_cmp, n_blocks, dec_seq, gqa, past):
    p = pl.program_id(1)
    n_kh = 2 * n_kv
    page = buf_ref.shape[3]
    slot = _page_stream(cache_ref, buf_ref, sem_ref, pt_ref, layer=layer, row0=0, n_rows=n_kh,
                        pages_per_step=pages_per_step)
    cpp = page // STRIDE
    step_chunks = cpp * pages_per_step
    group_chunks = step_chunks * steps_per_group
    c0 = pl.multiple_of(lax.rem(p, steps_per_group) * step_chunks, 8)
    for k in range(pages_per_step):
        for s in range(STRIDE):
            for kh in range(n_kh):
                x_scr[kh, pl.ds(c0 + k * cpp, cpp), s * HEAD_DIM:(s + 1) * HEAD_DIM] = (
                    buf_ref[slot, k, kh, pl.ds(s, cpp, stride=STRIDE), :])

    @pl.when(lax.rem(p, steps_per_group) == steps_per_group - 1)
    def _():
        g0 = pl.multiple_of((p // steps_per_group) * group_chunks, 8)
        for kh in range(n_kh):
            kind = kh // n_kv
            ab_scr[kh, pl.ds(g0, group_chunks), :] = _dot(x_scr[kh].astype(BF16), w1_ref[kind])

    @pl.when(p == n_steps - 1)
    def _():
        n_chunks = ab_scr.shape[1]
        comp = []
        for kh in range(n_kh):
            kind = kh // n_kv
            hpe = _dot(pe_ref[kind], w1_ref[kind])
            hpe = hpe[0:1, 0:HEAD_DIM] + hpe[8:9, HEAD_DIM:2 * HEAD_DIM]
            ab = ab_scr[kh]
            h = ab[:, 0:HEAD_DIM] + pltpu.roll(ab[:, HEAD_DIM:2 * HEAD_DIM], n_chunks - 1, 0) + hpe
            comp.append(_dot(_gelu(h).astype(BF16), w2_ref[kind]).astype(BF16))
        qrows = lax.broadcasted_iota(jnp.int32, (1, HEAD_DIM), 1)
        pos = past + lax.rem(qrows, dec_seq)
        for h in range(n_kv):
            kc, vc = comp[h], comp[n_kv + h]
            st = _dot_nt(kc, (q_ref[h] * LOGIT_SCALE).astype(BF16))
            n = lax.broadcasted_iota(jnp.int32, st.shape, 0)
            pt = _masked_softmax2(st, (n * STRIDE + (L_CMP - 1) <= pos) & (n < n_cmp), axis=0)
            oc_ref[h] = _dot(pt.T.astype(BF16), vc)
            pg = pt
            for g in range(1, gqa):
                pg = pg + pltpu.roll(pt, HEAD_DIM - g * dec_seq, 1)
            hi, lo = _split_hi_lo(pg)
            imp_t = _dot(ovt_ref[...], hi) + _dot(ovt_ref[...], lo)
            imp = imp_t.T
            tpos = past + lax.broadcasted_iota(jnp.int32, (imp.shape[0], 1), 0)
            score = _block_scores(imp, tpos, n_blocks, 1)
            score_t = score.T
            nb_pad = score.shape[1]
            ii = lax.broadcasted_iota(jnp.int32, (nb_pad, nb_pad), 0)
            jj = lax.broadcasted_iota(jnp.int32, (nb_pad, nb_pad), 1)
            for t in range(dec_seq):
                col = score_t[:, t:t + 1]
                rowv = score[t:t + 1, :]
                beats = (col > rowv) | ((col == rowv) & (ii < jj))
                rank = jnp.sum(jnp.where(beats, 1.0, 0.0), axis=0, keepdims=True)
                sel_ref[h, t:t + 1, :] = jnp.where(rank < float(min(N_SEL, n_blocks)), 1.0, 0.0)
            sel_ref[h, dec_seq:, :] = jnp.zeros((sel_ref.shape[1] - dec_seq, nb_pad), F32)


def _cmp_sample(page_table, cache_rows, q_pad, w1cat, w2, pe, *, layer, n_kv, gqa, dec_seq, name):
    batch, n_pages = page_table.shape
    page = cache_rows.shape[2]
    past = n_pages * page
    cpp = page // STRIDE
    n_chunks = past // STRIDE
    assert dec_seq < STRIDE and n_chunks % 8 == 0
    n_cmp = (past + dec_seq) // STRIDE - 1
    n_blocks = -(-(past + dec_seq) // L_SLC)
    nb_pad = -(-n_blocks // 128) * 128
    pps = _tile(n_pages, 4, 1)
    n_steps = n_pages // pps
    spg = _tile(n_steps, max(1, 128 // (cpp * pps)), 1)
    n_kh = 2 * n_kv
    ovt = jnp.asarray(_overlap_matrix(n_cmp, n_blocks, n_chunks, nb_pad).T.copy(), BF16)
    kern = functools.partial(_cmp_s_kernel, layer=layer, n_kv=n_kv, n_steps=n_steps, pages_per_step=pps,
                             steps_per_group=spg, n_cmp=n_cmp, n_blocks=n_blocks, dec_seq=dec_seq,
                             gqa=gqa, past=past)
    grid_spec = pltpu.PrefetchScalarGridSpec(
        num_scalar_prefetch=1,
        grid=(batch, n_steps),
        in_specs=[
            pl.BlockSpec(memory_space=pl.ANY),
            pl.BlockSpec((None, n_kv, 128, HEAD_DIM), lambda b, p, pt: (b, 0, 0, 0)),
            pl.BlockSpec(w1cat.shape, lambda b, p, pt: (0, 0, 0)),
            pl.BlockSpec(w2.shape, lambda b, p, pt: (0, 0, 0)),
            pl.BlockSpec(pe.shape, lambda b, p, pt: (0, 0, 0)),
            pl.BlockSpec(ovt.shape, lambda b, p, pt: (0, 0))],
        out_specs=(pl.BlockSpec((None, n_kv, 128, HEAD_DIM), lambda b, p, pt: (b, 0, 0, 0)),
                   pl.BlockSpec((None, n_kv, 8, nb_pad), lambda b, p, pt: (b, 0, 0, 0))),
        scratch_shapes=[pltpu.VMEM((n_kh, spg * pps * cpp, STRIDE * HEAD_DIM), F32),
                        pltpu.VMEM((n_kh, n_chunks, 2 * HEAD_DIM), F32),
                        pltpu.VMEM((2, pps, n_kh, page, HEAD_DIM), F32),
                        pltpu.SemaphoreType.DMA((2,))],
    )
    return pl.pallas_call(
        kern,
        out_shape=(jax.ShapeDtypeStruct((batch, n_kv, 128, HEAD_DIM), F32),
                   jax.ShapeDtypeStruct((batch, n_kv, 8, nb_pad), F32)),
        grid_spec=grid_spec,
        compiler_params=_cparams("arbitrary", "arbitrary"),
        name=name,
    )(page_table, cache_rows, q_pad, w1cat, w2, pe, ovt)


def _slc_s_kernel(pt_ref, cache_ref, q_ref, sel_ref, ex_ref, kn_ref, vn_ref, win_ref, kwn_ref, vwn_ref,
                  oc_ref, gt_ref, o_ref, qbd_scr, m_scr, l_scr, acc_scr, buf_ref, sem_ref,
                  *, layer, n_kv, n_steps, pages_per_step, rows_per_head, dec_seq, past, w_buf):
    p = pl.program_id(1)
    slot = _page_stream(cache_ref, buf_ref, sem_ref, pt_ref, layer=layer, row0=2 * n_kv,
                        n_rows=2 * n_kv, pages_per_step=pages_per_step)
    rows = n_kv * rows_per_head

    @pl.when(p == 0)
    def _():
        qbd_scr[...] = jnp.zeros(qbd_scr.shape, qbd_scr.dtype)
        for h in range(n_kv):
            qbd_scr[h * rows_per_head:(h + 1) * rows_per_head, h * HEAD_DIM:(h + 1) * HEAD_DIM] = (
                q_ref[h, 0:rows_per_head, :] * LOGIT_SCALE)
        m_scr[...] = jnp.full(m_scr.shape, NEG_BIG, F32)
        l_scr[...] = jnp.zeros(l_scr.shape, F32)
        acc_scr[...] = jnp.zeros(acc_scr.shape, F32)

    qbd = qbd_scr[...].astype(BF16)

    def online_update(s, mask, v):
        sm = jnp.where(mask, s, NEG_BIG)
        m_old = m_scr[...]
        m_new = jnp.maximum(m_old, jnp.max(sm, axis=-1, keepdims=True))
        alpha = jnp.exp2(m_old - m_new)
        e = jnp.where(mask, jnp.exp2(sm - m_new), 0.0)
        l_scr[...] = alpha * l_scr[...] + jnp.sum(e, axis=-1, keepdims=True)
        acc_scr[...] = alpha * acc_scr[...] + _dot(e.astype(BF16), v)
        m_scr[...] = m_new

    ks, vs = [], []
    for k in range(pages_per_step):
        ks.append(jnp.concatenate([buf_ref[slot, k, h] for h in range(n_kv)], axis=1))
        vs.append(jnp.concatenate([buf_ref[slot, k, n_kv + h] for h in range(n_kv)], axis=1))
    k_all = jnp.concatenate(ks, axis=0).astype(BF16)
    v_all = jnp.concatenate(vs, axis=0).astype(BF16)
    s = _dot_nt(qbd, k_all)
    sel_keys = _dot(sel_ref[...].astype(BF16), ex_ref[...])
    online_update(s, sel_keys > 0.5, v_all)

    @pl.when(p == n_steps - 1)
    def _():
        r = lax.broadcasted_iota(jnp.int32, (rows, 1), 0)
        t = lax.rem(r, dec_seq)
        s = _dot_nt(qbd, kn_ref[...].astype(BF16))
        j = lax.broadcasted_iota(jnp.int32, s.shape, 1)
        online_update(s, (j <= t) & (j < dec_seq), vn_ref[...].astype(BF16))
        o_s = acc_scr[...] / jnp.maximum(l_scr[...], 1e-30)

        sb = _dot_nt(qbd, _head_slabs(win_ref, 0, n_kv).astype(BF16))
        sn = _dot_nt(qbd, kwn_ref[...].astype(BF16))
        ib = lax.broadcasted_iota(jnp.int32, sb.shape, 1)
        kpos = past - w_buf + ib
        d = (past + t) - kpos
        mb = (d >= 0) & (d < WINDOW) & (kpos >= 0)
        jn = lax.broadcasted_iota(jnp.int32, sn.shape, 1)
        mn = (jn <= t) & (jn < dec_seq) & (t - jn < WINDOW)
        smb = jnp.where(mb, sb, NEG_BIG)
        smn = jnp.where(mn, sn, NEG_BIG)
        mx = jnp.maximum(jnp.max(smb, axis=-1, keepdims=True), jnp.max(smn, axis=-1, keepdims=True))
        eb = jnp.where(mb, jnp.exp2(smb - mx), 0.0)
        en = jnp.where(mn, jnp.exp2(smn - mx), 0.0)
        den = jnp.maximum(jnp.sum(eb, axis=-1, keepdims=True) + jnp.sum(en, axis=-1, keepdims=True),
                          1e-30)
        o_w = (_dot((eb / den).astype(BF16), _head_slabs(win_ref, n_kv, n_kv).astype(BF16))
               + _dot((en / den).astype(BF16), vwn_ref[...].astype(BF16)))

        for h in range(n_kv):
            r0 = h * rows_per_head
            c0 = h * HEAD_DIM
            gt = gt_ref[h]
            o_ref[h] = (gt[:, 0:1] * oc_ref[h, 0:rows_per_head, :]
                        + gt[:, 1:2] * o_s[r0:r0 + rows_per_head, c0:c0 + HEAD_DIM]
                        + gt[:, 2:3] * o_w[r0:r0 + rows_per_head, c0:c0 + HEAD_DIM])


def _slc_sample(page_table, cache_rows, q_pad, sel_steps, k_new, v_new, win_rows, kw_new, vw_new, o_c,
                gates, *, layer, n_kv, gqa, dec_seq, pages_per_step, name):
    batch, n_pages = page_table.shape
    page = cache_rows.shape[2]
    past = n_pages * page
    width = n_kv * HEAD_DIM
    n_kh = 2 * n_kv
    rph = gqa * dec_seq
    rows = n_kv * rph
    w_buf = win_rows.shape[2]
    n_new = k_new.shape[1]
    pps = pages_per_step
    n_steps = n_pages // pps
    ex = jnp.asarray(_block_to_key_matrix(1, sel_steps.shape[3], pps * page)[0], BF16)
    kern = functools.partial(_slc_s_kernel, layer=layer, n_kv=n_kv, n_steps=n_steps, pages_per_step=pps,
                             rows_per_head=rph, dec_seq=dec_seq, past=past, w_buf=w_buf)

    def new_spec():
        return pl.BlockSpec((None, n_new, width), lambda b, p, pt: (b, 0, 0))

    grid_spec = pltpu.PrefetchScalarGridSpec(
        num_scalar_prefetch=1,
        grid=(batch, n_steps),
        in_specs=[
            pl.BlockSpec(memory_space=pl.ANY),
            pl.BlockSpec((None, n_kv, 128, HEAD_DIM), lambda b, p, pt: (b, 0, 0, 0)),
            pl.BlockSpec((None, None, rows, sel_steps.shape[3]), lambda b, p, pt: (b, p, 0, 0)),
            pl.BlockSpec(ex.shape, lambda b, p, pt: (0, 0)),
            new_spec(), new_spec(),
            pl.BlockSpec((None, None, w_buf, n_kh, HEAD_DIM), lambda b, p, pt: (layer, b, 0, 0, 0)),
            new_spec(), new_spec(),
            pl.BlockSpec((None, n_kv, 128, HEAD_DIM), lambda b, p, pt: (b, 0, 0, 0)),
            pl.BlockSpec((None, n_kv, rph, 8), lambda b, p, pt: (b, 0, 0, 0))],
        out_specs=pl.BlockSpec((None, n_kv, rph, HEAD_DIM), lambda b, p, pt: (b, 0, 0, 0)),
        scratch_shapes=[pltpu.VMEM((rows, width), F32),
                        pltpu.VMEM((rows, 1), F32),
                        pltpu.VMEM((rows, 1), F32),
                        pltpu.VMEM((rows, width), F32),
                        pltpu.VMEM((2, pps, n_kh, page, HEAD_DIM), F32),
                        pltpu.SemaphoreType.DMA((2,))],
    )
    return pl.pallas_call(
        kern,
        out_shape=jax.ShapeDtypeStruct((batch, n_kv, rph, HEAD_DIM), F32),
        grid_spec=grid_spec,
        compiler_params=_cparams("arbitrary", "arbitrary"),
        name=name,
    )(page_table, cache_rows, q_pad, sel_steps, ex, k_new, v_new, win_rows, kw_new, vw_new, o_c,
      gates)


def _rope_tables(pos):
    inv = ROPE_THETA ** (-jnp.arange(ROT_HALF, dtype=F32) * 2.0 / ROT_DIM)
    ang = pos.astype(F32)[:, None] * inv[None, :]
    cos, sin = jnp.cos(ang), jnp.sin(ang)
    rest = HEAD_DIM - ROT_DIM
    cos_t = jnp.concatenate([cos, cos, jnp.ones((pos.shape[0], rest), F32)], axis=1)
    sin_t = jnp.concatenate([-sin, sin, jnp.zeros((pos.shape[0], rest), F32)], axis=1)
    return cos_t, sin_t


def _prep_weights(w_in, ffn_conv_b, n_gate):
    w_gate = jnp.pad(w_in[:, :, w_in.shape[2] - n_gate:], ((0, 0), (0, 0), (0, 128 - n_gate)))
    return dict(w_gate=w_gate.astype(BF16),
                f_cb=ffn_conv_b.reshape(ffn_conv_b.shape[0], 1, ffn_conv_b.shape[1]))


def _prep_compress(l, cmp_pe, cmp_w1, cmp_w2):
    half = STRIDE * HEAD_DIM
    w1 = cmp_w1[l].reshape(2, 2 * half, HEAD_DIM)
    wl = {}
    wl["cmp_w1"] = w1.astype(BF16)
    wl["cmp_w1cat"] = jnp.concatenate([w1[:, :half], w1[:, half:]], axis=2).astype(BF16)
    wl["cmp_w2"] = cmp_w2[l].astype(BF16)
    pe = cmp_pe[l].reshape(2, 1, 2 * half)
    wl["cmp_pe"] = jnp.broadcast_to(pe, (2, 16, 2 * half)).astype(BF16)
    pe2 = cmp_pe[l].reshape(2, 2, 1, half)
    wl["cmp_pecat"] = jnp.concatenate([jnp.broadcast_to(pe2[:, 0], (2, 8, half)),
                                       jnp.broadcast_to(pe2[:, 1], (2, 8, half))], axis=1).astype(BF16)
    return wl


def _project(x_bf, ws, wb, l, cos_t, sin_t, dims, tag):
    d_a, d_b, d_kv, n_heads, d_ff = dims
    n_kv = d_kv // HEAD_DIM
    emit = wb is None
    if emit:
        src = dict(uv=(ws["w_in"], l, 0), q=(ws["w_in"], l, 2 * d_a), kv=(ws["w_in"], l, 2 * d_a + d_b))
    else:
        src = dict(uv=(wb["uv"], 0, 0), q=(wb["q"], 0, 0), kv=(wb["kv"], 0, 0))
    new = {}
    w, lay, c0 = src["uv"]
    out = _matmul([x_bf], w, layer=lay, col_start=c0, n=2 * d_a, epilogue="gelu", tm=1024,
                  tn=512 if emit else 1024, emit=emit, name=f"proj_uv_{tag}")
    uv = out[0] if emit else out
    if emit:
        new["uv"] = out[1]
    w, lay, c0 = src["q"]
    out = _rope_matmul(x_bf, w, cos_t, sin_t, layer=lay, col_start=c0, n=d_b, tn=d_kv, alternate=False,
                       emit=emit, name=f"proj_q_{tag}")
    q2d = out[0]
    if emit:
        new["q"] = out[-1]
    w, lay, c0 = src["kv"]
    out = _rope_matmul(x_bf, w, cos_t, sin_t, layer=lay, col_start=c0, n=6 * d_kv, tn=d_kv, alternate=True,
                       n_kv=n_kv, emit=emit, name=f"proj_kv_{tag}")
    kv, nsa4, win4 = out[0], out[1], out[2]
    if emit:
        new["kv"] = out[-1]
    gates = _matmul([x_bf], ws["w_gate"], layer=l, epilogue="sigmoid", tm=1024, tn=128,
                    name=f"proj_gate_{tag}")
    return uv, q2d, kv, nsa4, win4, gates, new


def _mix_and_norm(x, a_out, b_out, ws, wb, l, ln_g, ln_b, alpha, tag):
    new = {}
    if wb is None:
        z, new["o_a"], new["o_b"] = _matmul([a_out, b_out], ws["w_o"], layer=l, tm=1024, tn=512,
                                            emit=True, residual=x, res_scale=alpha, name=f"w_o_{tag}")
    else:
        z = _matmul([a_out, b_out], [wb["o_a"], wb["o_b"]], layer=0, tm=1024, tn=1024,
                    residual=x, res_scale=alpha, name=f"w_o_{tag}")
    y, yb = _layer_norm(z, ln_g[l, 0], ln_b[l, 0], name=f"ln1_{tag}")
    return y, yb, new


def _layer_prompt(x, x_bf, ws, wb, wl, l, p, cos_t, sin_t, dims, batch, seq, n_kv, alpha):
    d_a, d_b, d_kv, n_heads, d_ff = dims
    gqa = n_heads // n_kv
    m = batch * seq
    uv, q2d, kv, nsa4, win4, gates, _ = _project(x_bf, ws, wb, l, cos_t, sin_t, dims, "p")
    a_out, _ = _sgu(uv, p["sgu_w"][l], p["sgu_b"][l].T, p["sgu_g"][l], rows=CHUNK, name="sgu_p")
    kcmp, vcmp = _compress_prompt(kv, wl["cmp_w1"], wl["cmp_w2"], wl["cmp_pe"], batch=batch, seq=seq,
                                  n_kv=n_kv, name="compress_p")
    gates_h = gates[:, :3 * n_heads].reshape(m, 3, n_kv, gqa).transpose(2, 0, 1, 3).reshape(n_kv, m, 3 * gqa)
    b_out = _attn_prompt(q2d, kv, kcmp, vcmp, gates_h, batch=batch, seq=seq, n_heads=n_heads, n_kv=n_kv,
                         name="nsa_p")
    x1, x1b, _ = _mix_and_norm(x, a_out, b_out, ws, wb, l, p["ln_g"], p["ln_b"], alpha, "p")
    state0 = jnp.zeros((batch, CONV_W - 1, d_ff), F32)
    act, conv_new = _ffn_in_seq(x1b, wb["f_gate"], wb["f_up"], p["ffn_conv_w"], ws["f_cb"], state0,
                                layer=l, seq_len=seq, name="ffn_in_p")
    z = _matmul([act], wb["f_down"], layer=0, tm=512, tn=512, residual=x1, res_scale=alpha,
                name="ffn_down_p")
    x2, x2b = _layer_norm(z, p["ln_g"][l, 1], p["ln_b"][l, 1], name="ln2_p")
    new_nsa = nsa4.reshape(batch, seq, 4, n_kv, HEAD_DIM)
    new_win = win4.reshape(batch, seq, 2, n_kv, HEAD_DIM)[:, -min(WINDOW, seq):]
    return x2, x2b, new_nsa, new_win, conv_new


def _layer_sample(x, x_bf, ws, wl, l, p, cos_t, sin_t, dims, batch, dec_seq, n_kv, alpha, page_table,
                  cache_rows, win_rows, conv_state):
    d_a, d_b, d_kv, n_heads, d_ff = dims
    gqa = n_heads // n_kv
    m = batch * dec_seq
    rph = gqa * dec_seq
    uv, q2d, kv, nsa4, win4, gates, wb = _project(x_bf, ws, None, l, cos_t, sin_t, dims, "s")
    w_small = p["sgu_w"][l][:, :dec_seq, :dec_seq]
    eye = jnp.eye(batch, dtype=F32)
    w_bd = jnp.einsum("ab,gts->gatbs", eye, w_small).reshape(-1, m, m)
    bt_bd = jnp.tile(p["sgu_b"][l].T[:dec_seq], (batch, 1))
    a_out, v_rows = _sgu(uv, w_bd, bt_bd, p["sgu_g"][l], rows=m, name="sgu_s")

    q = q2d.reshape(batch, dec_seq, n_kv, gqa, HEAD_DIM).transpose(0, 2, 3, 1, 4)
    q_pad = jnp.pad(q.reshape(batch, n_kv, rph, HEAD_DIM), ((0, 0), (0, 0), (0, 128 - rph), (0, 0)))
    o_c, sel = _cmp_sample(page_table, cache_rows, q_pad, wl["cmp_w1cat"], wl["cmp_w2"], wl["cmp_pecat"],
                           layer=l, n_kv=n_kv, gqa=gqa, dec_seq=dec_seq, name="cmp_s")
    n_pages = page_table.shape[1]
    bpp = cache_rows.shape[2] // L_SLC
    pps = _tile(n_pages, 4, 1)
    n_steps = n_pages // pps
    sel_steps = sel[:, :, :dec_seq, :n_pages * bpp].reshape(batch, n_kv, 1, dec_seq, n_steps, pps * bpp)
    sel_steps = jnp.broadcast_to(sel_steps, (batch, n_kv, gqa, dec_seq, n_steps, pps * bpp))
    sel_steps = sel_steps.transpose(0, 4, 1, 2, 3, 5).reshape(batch, n_steps, n_kv * rph, pps * bpp)
    sel_steps = jnp.pad(sel_steps, ((0, 0), (0, 0), (0, 0), (0, 128 - pps * bpp)))

    def new_rows(kind):
        rows = kv[:, kind * d_kv:(kind + 1) * d_kv].reshape(batch, dec_seq, d_kv)
        return jnp.pad(rows, ((0, 0), (0, 128 - dec_seq), (0, 0)))

    gates_s = gates[:, :3 * n_heads].reshape(batch, dec_seq, 3, n_kv, gqa).transpose(0, 3, 4, 1, 2)
    gates_s = jnp.pad(gates_s.reshape(batch, n_kv, rph, 3), ((0, 0), (0, 0), (0, 0), (0, 5)))
    b_rows = _slc_sample(page_table, cache_rows, q_pad, sel_steps, new_rows(2), new_rows(3), win_rows,
                         new_rows(4), new_rows(5), o_c, gates_s, layer=l, n_kv=n_kv, gqa=gqa,
                         dec_seq=dec_seq, pages_per_step=pps, name="slc_s")
    b_out = b_rows.reshape(batch, n_kv, gqa, dec_seq, HEAD_DIM).transpose(0, 3, 1, 2, 4)
    b_out = b_out.reshape(m, d_b).astype(BF16)

    x1, x1b, wb_o = _mix_and_norm(x, a_out, b_out, ws, None, l, p["ln_g"], p["ln_b"], alpha, "s")
    wb.update(wb_o)
    st = conv_state
    zero = jnp.zeros((batch, dec_seq - 1, d_ff), F32)
    h1 = jnp.concatenate([st[:, 1:2], zero], axis=1).reshape(m, d_ff)
    h2 = jnp.concatenate([st, zero[:, 1:]], axis=1).reshape(m, d_ff)
    act, gate, wb["f_gate"], wb["f_up"] = _ffn_in_short(
        x1b, ws["f_in"], p["ffn_conv_w"], ws["f_cb"], h1, h2, layer=l, seq_len=dec_seq, name="ffn_in_s")
    z, wb["f_down"] = _matmul([act], ws["f_down"], layer=l, tn=256, emit=True, residual=x1,
                              res_scale=alpha, name="ffn_down_s")
    x2, x2b = _layer_norm(z, p["ln_g"][l, 1], p["ln_b"][l, 1], name="ln2_s")
    new_nsa = nsa4.reshape(batch, dec_seq, 4, n_kv, HEAD_DIM)
    new_win = win4.reshape(batch, dec_seq, 2, n_kv, HEAD_DIM)
    conv_new = gate.reshape(batch, dec_seq, d_ff)[:, dec_seq - (CONV_W - 1):]
    return x2, x2b, new_nsa, new_win, v_rows.reshape(batch, dec_seq, d_a), conv_new, wb


def kernel(x_prompt, x_sample, cache_nsa_kv, cache_win_kv, state_ffn_conv, page_table, w_in, sgu_w,
           sgu_b, sgu_g, cmp_pe, cmp_w1, cmp_w2, w_o, ln_g, ln_b, ffn_w_in, ffn_conv_w, ffn_conv_b,
           ffn_w_down):
    bp, seq, d_model = x_prompt.shape
    bs, dec_seq, _ = x_sample.shape
    depth = w_in.shape[0]
    n_kv = cache_nsa_kv.shape[4]
    page = cache_nsa_kv.shape[2]
    past = page_table.shape[1] * page
    d_a = d_model // 2
    d_b = d_model - d_a
    n_heads = d_b // HEAD_DIM
    d_kv = n_kv * HEAD_DIM
    d_ff = ffn_conv_w.shape[-1]
    dims = (d_a, d_b, d_kv, n_heads, d_ff)
    alpha = (2 * depth) ** 0.25
    assert dec_seq >= CONV_W - 1 and seq % CHUNK == 0

    cos_p, sin_p = _rope_tables(jnp.tile(jnp.arange(seq, dtype=jnp.int32), bp))
    cos_s, sin_s = _rope_tables(jnp.tile(past + jnp.arange(dec_seq, dtype=jnp.int32), bs))
    cache_rows = cache_nsa_kv.reshape(depth, cache_nsa_kv.shape[1], page, 4 * n_kv, HEAD_DIM)
    win_rows = cache_win_kv.reshape(depth, bs, cache_win_kv.shape[2], 2 * n_kv, HEAD_DIM)
    ws = _prep_weights(w_in, ffn_conv_b, 3 * n_heads)
    ws.update(w_in=w_in, w_o=w_o, f_in=ffn_w_in, f_down=ffn_w_down)
    p = dict(sgu_w=sgu_w, sgu_b=sgu_b, sgu_g=sgu_g, ln_g=ln_g, ln_b=ln_b, ffn_conv_w=ffn_conv_w)

    xp = x_prompt.reshape(bp * seq, d_model)
    xs = x_sample.reshape(bs * dec_seq, d_model)
    xp_bf, xs_bf = xp.astype(BF16), xs.astype(BF16)
    outs = [[] for _ in range(7)]
    for l in range(depth):
        wl = _prep_compress(l, cmp_pe, cmp_w1, cmp_w2)
        xs, xs_bf, nsa_s, win_s, v_s, conv_s, wb = _layer_sample(
            xs, xs_bf, ws, wl, l, p, cos_s, sin_s, dims, bs, dec_seq, n_kv, alpha, page_table,
            cache_rows, win_rows, state_ffn_conv[l])
        xp, xp_bf, nsa_p, win_p, conv_p = _layer_prompt(
            xp, xp_bf, ws, wb, wl, l, p, cos_p, sin_p, dims, bp, seq, n_kv, alpha)
        for acc, val in zip(outs, (nsa_p, nsa_s, win_p, win_s, v_s, conv_p, conv_s)):
            acc.append(val)
    return (xp.reshape(bp, seq, d_model), xs.reshape(bs, dec_seq, d_model),
            *[jnp.stack(o) for o in outs])
```

```python
import functools
import math

import jax
import jax.numpy as jnp
import numpy as np
from jax import lax
from jax.experimental import pallas as pl
from jax.experimental.pallas import tpu as pltpu

HEAD_DIM = 128
CHUNK = 128
STRIDE = 16
L_CMP = 2 * STRIDE
L_SLC = 64
N_SEL = 16
WINDOW = 512
ROT_DIM = HEAD_DIM // 4
ROT_HALF = ROT_DIM // 2
ROPE_THETA = 500000.0
CONV_W = 3
LN_EPS = 1e-5
SCALE = HEAD_DIM ** -0.5
LOGIT_SCALE = SCALE * math.log2(math.e)
SEL_FORCE = 1e9
NEG_BIG = -3.0e38
VMEM_LIMIT = 56 * 1024 * 1024

F32 = jnp.float32
BF16 = jnp.bfloat16


def _cparams(*sem):
    return pltpu.CompilerParams(dimension_semantics=sem, vmem_limit_bytes=VMEM_LIMIT)


def _tile(n, pref, unit=128):
    if n <= pref:
        return n
    t = (pref // unit) * unit
    while t > unit and n % t:
        t -= unit
    assert n % t == 0, (n, pref, unit)
    return t


def _gelu(x):
    return jax.nn.gelu(x, approximate=True)


def _dot(a, b):
    return jnp.dot(a, b, preferred_element_type=F32)


def _dot_nt(a, b):
    return lax.dot_general(a, b, (((1,), (1,)), ((), ())), preferred_element_type=F32)


def _split_hi_lo(x):
    hi = x.astype(BF16)
    lo = (x - hi.astype(F32)).astype(BF16)
    return hi, lo


def _masked_softmax2(s, mask, axis=-1):
    sm = jnp.where(mask, s, NEG_BIG)
    m = jnp.max(sm, axis=axis, keepdims=True)
    p = jnp.where(mask, jnp.exp2(sm - m), 0.0)
    return p / jnp.maximum(jnp.sum(p, axis=axis, keepdims=True), 1e-30)


def _mm_kernel(*refs, n_lhs, epilogue, emit, res_scale, w_transposed):
    n_in = 2 * n_lhs + (res_scale is not None)
    ws = [refs[n_lhs + k][...] for k in range(n_lhs)]
    if w_transposed:
        ws = [w.T for w in ws]
    if emit:
        ws = [w.astype(BF16) for w in ws]
        for k in range(n_lhs):
            refs[n_in + 1 + k][...] = ws[k]
    acc = _dot(refs[0][...], ws[0])
    for k in range(1, n_lhs):
        acc = acc + _dot(refs[k][...], ws[k])
    if res_scale is not None:
        acc = res_scale * refs[n_in - 1][...] + acc
    o_ref = refs[n_in]
    if epilogue == "gelu":
        acc = _gelu(acc)
    elif epilogue == "sigmoid":
        acc = jax.nn.sigmoid(acc)
    o_ref[...] = acc.astype(o_ref.dtype)


def _matmul(xs, w, *, layer, col_start=0, n=None, epilogue="none", out_dtype=F32, tm=512, tn=512,
            emit=False, residual=None, res_scale=None, w_transposed=False, name):
    m, kdim = xs[0].shape
    n = (w[0] if isinstance(w, (list, tuple)) else w).shape[1 if w_transposed else 2] if n is None else n
    tm = _tile(m, tm, 8)
    tn = _tile(n, tn)
    assert col_start % tn == 0 and all(x.shape == (m, kdim) for x in xs)
    c0 = col_start // tn
    n_lhs = len(xs)
    in_specs = [pl.BlockSpec((tm, kdim), lambda i, j: (i, 0)) for _ in xs]
    separate = isinstance(w, (list, tuple))
    w_list = list(w) if separate else [w] * n_lhs
    if w_transposed:
        assert n_lhs == 1
        in_specs += [pl.BlockSpec((None, tn, kdim), lambda i, j: (layer, c0 + j, 0))]
    else:
        in_specs += [pl.BlockSpec((None, kdim, tn), lambda i, j, r=r: (layer, 0 if separate else r, c0 + j))
                     for r in range(n_lhs)]
    extra = []
    if residual is not None:
        in_specs.append(pl.BlockSpec((tm, tn), lambda i, j: (i, j)))
        extra = [residual]
    out_shape = [jax.ShapeDtypeStruct((m, n), out_dtype)]
    out_specs = [pl.BlockSpec((tm, tn), lambda i, j: (i, j))]
    if emit:
        assert m == tm
        out_shape += [jax.ShapeDtypeStruct((1, kdim, n), BF16)] * n_lhs
        out_specs += [pl.BlockSpec((None, kdim, tn), lambda i, j: (0, 0, j))] * n_lhs
    out = pl.pallas_call(
        functools.partial(_mm_kernel, n_lhs=n_lhs, epilogue=epilogue, emit=emit,
                          res_scale=res_scale if residual is not None else None,
                          w_transposed=w_transposed),
        out_shape=tuple(out_shape),
        grid=(m // tm, n // tn),
        in_specs=in_specs,
        out_specs=tuple(out_specs),
        compiler_params=_cparams("parallel", "arbitrary"),
        name=name,
    )(*xs, *w_list, *extra)
    return out if emit else out[0]


def _rope_mm_kernel(x_ref, w_ref, cos_ref, sin_ref, o_ref, *more_refs, heads_per_tile, alternate,
                    n_nsa_kinds, emit):
    j = pl.program_id(1)
    w = w_ref[...]
    cache_refs = more_refs
    if emit:
        w = w.T.astype(BF16)
        more_refs[-1][...] = w
        cache_refs = more_refs[:-1]
    acc = _dot(x_ref[...], w)
    cosv, sinv = cos_ref[...], sin_ref[...]
    if alternate:
        rot = lax.rem(j, 2) == 0
        cosv = jnp.where(rot, cosv, 1.0)
        sinv = jnp.where(rot, sinv, 0.0)
    lane = lax.broadcasted_iota(jnp.int32, cosv.shape, 1)
    heads = []
    for h in range(heads_per_tile):
        hs = acc[:, h * HEAD_DIM:(h + 1) * HEAD_DIM]
        partner = jnp.where(lane < ROT_HALF,
                            pltpu.roll(hs, HEAD_DIM - ROT_HALF, 1),
                            pltpu.roll(hs, ROT_HALF, 1))
        heads.append(hs * cosv + partner * sinv)
        o_ref[:, h * HEAD_DIM:(h + 1) * HEAD_DIM] = heads[h]
    if cache_refs:
        nsa_ref, win_ref = cache_refs

        @pl.when(j < n_nsa_kinds)
        def _():
            for h in range(heads_per_tile):
                nsa_ref[:, h, :] = heads[h]

        @pl.when(j >= n_nsa_kinds)
        def _():
            for h in range(heads_per_tile):
                win_ref[:, h, :] = heads[h]


def _rope_matmul(x, w, cos_t, sin_t, *, layer, col_start, n, tn, alternate, n_kv=None, tm=1024,
                 emit=False, name):
    m, k = x.shape
    tm = _tile(m, tm, 8)
    assert col_start % tn == 0 and n % tn == 0
    c0 = col_start // tn
    hpt = tn // HEAD_DIM
    out_shape = [jax.ShapeDtypeStruct((m, n), F32)]
    out_specs = [pl.BlockSpec((tm, tn), lambda i, j: (i, j))]
    n_nsa = 4
    if alternate:
        assert hpt == n_kv and n == 6 * tn
        out_shape += [jax.ShapeDtypeStruct((m, n_nsa, n_kv, HEAD_DIM), F32),
                      jax.ShapeDtypeStruct((m, 2, n_kv, HEAD_DIM), F32)]
        out_specs += [pl.BlockSpec((tm, None, n_kv, HEAD_DIM),
                                   lambda i, j: (i, jnp.minimum(j, n_nsa - 1), 0, 0)),
                      pl.BlockSpec((tm, None, n_kv, HEAD_DIM),
                                   lambda i, j: (i, jnp.maximum(j - n_nsa, 0), 0, 0))]
    if emit:
        assert m == tm
        out_shape += [jax.ShapeDtypeStruct((1, k, n), BF16)]
        out_specs += [pl.BlockSpec((None, k, tn), lambda i, j: (0, 0, j))]
    kern = functools.partial(_rope_mm_kernel, heads_per_tile=hpt, alternate=alternate,
                             n_nsa_kinds=n_nsa, emit=emit)
    return pl.pallas_call(
        kern,
        out_shape=tuple(out_shape),
        grid=(m // tm, n // tn),
        in_specs=[pl.BlockSpec((tm, k), lambda i, j: (i, 0)),
                  (pl.BlockSpec((None, tn, k), lambda i, j: (layer, c0 + j, 0)) if emit else
                   pl.BlockSpec((None, k, tn), lambda i, j: (layer, 0, c0 + j))),
                  pl.BlockSpec((tm, HEAD_DIM), lambda i, j: (i, 0)),
                  pl.BlockSpec((tm, HEAD_DIM), lambda i, j: (i, 0))],
        out_specs=tuple(out_specs),
        compiler_params=_cparams("arbitrary", "arbitrary"),
        name=name,
    )(x, w, cos_t, sin_t)


def _ln_kernel(z_ref, g_ref, b_ref, y_ref, yb_ref):
    z = z_ref[...]
    mu = jnp.mean(z, axis=-1, keepdims=True)
    zc = z - mu
    var = jnp.mean(zc * zc, axis=-1, keepdims=True)
    y = zc * lax.rsqrt(var + LN_EPS) * g_ref[...] + b_ref[...]
    y_ref[...] = y
    yb_ref[...] = y.astype(BF16)


def _layer_norm(z, g, b, *, name):
    m, d = z.shape
    tr = _tile(m, 256, 8)
    return pl.pallas_call(
        _ln_kernel,
        out_shape=(jax.ShapeDtypeStruct((m, d), F32), jax.ShapeDtypeStruct((m, d), BF16)),
        grid=(m // tr,),
        in_specs=[pl.BlockSpec((tr, d), lambda i: (i, 0)),
                  pl.BlockSpec((1, d), lambda i: (0, 0)),
                  pl.BlockSpec((1, d), lambda i: (0, 0))],
        out_specs=(pl.BlockSpec((tr, d), lambda i: (i, 0)),
                   pl.BlockSpec((tr, d), lambda i: (i, 0))),
        compiler_params=_cparams("parallel"),
        name=name,
    )(z, g.reshape(1, d), b.reshape(1, d))


def _sgu_kernel(uv_ref, w_ref, bt_ref, g_ref, a_ref, *vn_refs, d_a, n_groups):
    rows = w_ref.shape[1]
    r = lax.broadcasted_iota(jnp.int32, (rows, rows), 0)
    c = lax.broadcasted_iota(jnp.int32, (rows, rows), 1)
    causal = r >= c
    for g in range(n_groups):
        lo = g * HEAD_DIM
        v = uv_ref[:, d_a + lo:d_a + lo + HEAD_DIM]
        mu = jnp.mean(v, axis=-1, keepdims=True)
        vc = v - mu
        var = jnp.mean(vc * vc, axis=-1, keepdims=True)
        vn = vc * lax.rsqrt(var + LN_EPS) * g_ref[:, lo:lo + HEAD_DIM]
        if vn_refs:
            vn_refs[0][:, lo:lo + HEAD_DIM] = vn
        w = jnp.where(causal, w_ref[g], 0.0).astype(BF16)
        mixed = _dot(w, vn.astype(BF16)) + bt_ref[:, g:g + 1]
        a_ref[:, lo:lo + HEAD_DIM] = (uv_ref[:, lo:lo + HEAD_DIM] * mixed).astype(a_ref.dtype)


def _sgu(uv, w, bt, gain, *, rows, with_vn, name):
    m = uv.shape[0]
    d_a = uv.shape[1] // 2
    n_groups = d_a // HEAD_DIM
    n_out = 2 if with_vn else 1
    return pl.pallas_call(
        functools.partial(_sgu_kernel, d_a=d_a, n_groups=n_groups),
        out_shape=(jax.ShapeDtypeStruct((m, d_a), BF16), jax.ShapeDtypeStruct((m, d_a), F32))[:n_out],
        grid=(m // rows,),
        in_specs=[pl.BlockSpec((rows, 2 * d_a), lambda i: (i, 0)),
                  pl.BlockSpec((n_groups, rows, rows), lambda i: (0, 0, 0)),
                  pl.BlockSpec((rows, n_groups), lambda i: (0, 0)),
                  pl.BlockSpec((1, d_a), lambda i: (0, 0))],
        out_specs=(pl.BlockSpec((rows, d_a), lambda i: (i, 0)),
                   pl.BlockSpec((rows, d_a), lambda i: (i, 0)))[:n_out],
        compiler_params=_cparams("parallel"),
        name=name,
    )(uv, w, bt, gain.reshape(1, d_a))


def _conv_act(gate, g1, g2, up, cw_ref, cb_ref):
    c = cb_ref[...] + g2 * cw_ref[0:1, :] + g1 * cw_ref[1:2, :] + gate * cw_ref[2:3, :]
    return _gelu(c) * up


def _ffn_in_seq_kernel(x_ref, *refs, tiles_per_seq, n_sub):
    wg, wu, cw, cb, st = (refs[k * n_sub:(k + 1) * n_sub] for k in range(5))
    a_ref, cn_ref, carry_ref = refs[5 * n_sub:]
    i = pl.program_id(1)
    x = x_ref[...]
    tm = x.shape[0]
    tn = wg[0].shape[1]
    first = lax.rem(i, tiles_per_seq) == 0
    row = lax.broadcasted_iota(jnp.int32, (tm, tn), 0)
    for k in range(n_sub):
        cols = slice(k * tn, (k + 1) * tn)
        gate = _dot(x, wg[k][...])
        up = _dot(x, wu[k][...])
        prev2 = jnp.where(first, st[k][0:1, :], carry_ref[0:1, cols])
        prev1 = jnp.where(first, st[k][1:2, :], carry_ref[1:2, cols])
        g1 = jnp.where(row == 0, prev1, pltpu.roll(gate, 1, 0))
        g2 = jnp.where(row == 0, prev2, jnp.where(row == 1, prev1, pltpu.roll(gate, 2, 0)))
        a_ref[:, cols] = _conv_act(gate, g1, g2, up, cw[k], cb[k]).astype(a_ref.dtype)
        tail = gate[tm - 2:tm, :]
        carry_ref[0:2, cols] = tail
        cn_ref[:, cols] = tail


def _ffn_in_seq(x, wg, wu, cw, cb, state, *, layer, seq_len, tm=1024, tn=256, n_sub=2, name):
    m, k = x.shape
    d_ff = cw.shape[2]
    tm = _tile(seq_len, tm, 8)
    tn = _tile(d_ff, tn)
    n_tiles = d_ff // tn
    tps = seq_len // tm
    n_seq = m // seq_len

    def col(j, s):
        return jnp.minimum(j * n_sub + s, n_tiles - 1)

    subs = range(n_sub)
    in_specs = [pl.BlockSpec((tm, k), lambda j, i: (i, 0))]
    in_specs += [pl.BlockSpec((None, k, tn), lambda j, i, s=s: (0, 0, col(j, s))) for s in subs] * 2
    in_specs += [pl.BlockSpec((None, CONV_W, tn), lambda j, i, s=s: (layer, 0, col(j, s))) for s in subs]
    in_specs += [pl.BlockSpec((None, 1, tn), lambda j, i, s=s: (layer, 0, col(j, s))) for s in subs]
    in_specs += [pl.BlockSpec((None, CONV_W - 1, tn), lambda j, i, s=s: (i // tps, 0, col(j, s)))
                 for s in subs]
    return pl.pallas_call(
        functools.partial(_ffn_in_seq_kernel, tiles_per_seq=tps, n_sub=n_sub),
        out_shape=(jax.ShapeDtypeStruct((m, d_ff), BF16),
                   jax.ShapeDtypeStruct((n_seq, CONV_W - 1, d_ff), F32)),
        grid=(pl.cdiv(n_tiles, n_sub), m // tm),
        in_specs=in_specs,
        out_specs=(pl.BlockSpec((tm, n_sub * tn), lambda j, i: (i, j)),
                   pl.BlockSpec((None, CONV_W - 1, n_sub * tn), lambda j, i: (i // tps, 0, j))),
        scratch_shapes=[pltpu.VMEM((8, n_sub * tn), F32)],
        compiler_params=_cparams("arbitrary", "arbitrary"),
        name=name,
    )(x, *([wg] * n_sub), *([wu] * n_sub), *([cw] * n_sub), *([cb] * n_sub), *([state] * n_sub))


def _ffn_in_short_kernel(x_ref, wg_ref, wu_ref, cw_ref, cb_ref, h1_ref, h2_ref, a_ref, gate_ref,
                         wgb_ref, wub_ref, *, seq_len):
    x = x_ref[...]
    wg = wg_ref[...].astype(BF16)
    wu = wu_ref[...].astype(BF16)
    wgb_ref[...] = wg
    wub_ref[...] = wu
    gate = _dot(x, wg)
    up = _dot(x, wu)
    t = lax.rem(lax.broadcasted_iota(jnp.int32, gate.shape, 0), seq_len)
    g1 = jnp.where(t >= 1, pltpu.roll(gate, 1, 0), h1_ref[...])
    g2 = jnp.where(t >= 2, pltpu.roll(gate, 2, 0), h2_ref[...])
    a_ref[...] = _conv_act(gate, g1, g2, up, cw_ref, cb_ref).astype(a_ref.dtype)
    gate_ref[...] = gate


def _ffn_in_short(x, w, cw, cb, h1, h2, *, layer, seq_len, tn=256, name):
    m, k = x.shape
    d_ff = cw.shape[2]
    tn = _tile(d_ff, tn)
    n_tiles = d_ff // tn
    return pl.pallas_call(
        functools.partial(_ffn_in_short_kernel, seq_len=seq_len),
        out_shape=(jax.ShapeDtypeStruct((m, d_ff), BF16), jax.ShapeDtypeStruct((m, d_ff), F32),
                   jax.ShapeDtypeStruct((1, k, d_ff), BF16), jax.ShapeDtypeStruct((1, k, d_ff), BF16)),
        grid=(n_tiles,),
        in_specs=[pl.BlockSpec((m, k), lambda j: (0, 0)),
                  pl.BlockSpec((None, k, tn), lambda j: (layer, 0, j)),
                  pl.BlockSpec((None, k, tn), lambda j: (layer, 0, n_tiles + j)),
                  pl.BlockSpec((None, CONV_W, tn), lambda j: (layer, 0, j)),
                  pl.BlockSpec((None, 1, tn), lambda j: (layer, 0, j)),
                  pl.BlockSpec((m, tn), lambda j: (0, j)),
                  pl.BlockSpec((m, tn), lambda j: (0, j))],
        out_specs=(pl.BlockSpec((m, tn), lambda j: (0, j)),
                   pl.BlockSpec((m, tn), lambda j: (0, j)),
                   pl.BlockSpec((None, k, tn), lambda j: (0, 0, j)),
                   pl.BlockSpec((None, k, tn), lambda j: (0, 0, j))),
        compiler_params=_cparams("parallel"),
        name=name,
    )(x, w, w, cw, cb, h1, h2)


def _compress_rows(xa, w1a, w1b, w2, hpe):
    a = _dot(xa, w1a)
    b = _dot(xa, w1b)
    n = a.shape[0]
    h = a + pltpu.roll(b, n - 1, 0) + hpe
    return _dot(_gelu(h).astype(BF16), w2)


def _pos_embed_term(pe_ref, w1_ref, kind):
    return _dot(pe_ref[kind], w1_ref[kind])[0:1, :]


def _compress_p_kernel(k_ref, v_ref, w1_ref, w2_ref, pe_ref, kc_ref, vc_ref, *, n_chunks):
    half = STRIDE * HEAD_DIM
    for kind, (src, dst) in enumerate(((k_ref, kc_ref), (v_ref, vc_ref))):
        xa = jnp.concatenate(
            [src[pl.ds(s, n_chunks, stride=STRIDE), :] for s in range(STRIDE)], axis=1).astype(BF16)
        hpe = _pos_embed_term(pe_ref, w1_ref, kind)
        out = _compress_rows(xa, w1_ref[kind, 0:half, :], w1_ref[kind, half:2 * half, :],
                             w2_ref[kind], hpe)
        dst[...] = out.astype(dst.dtype)


def _compress_prompt(kv, w1, w2, pe, *, batch, seq, n_kv, name):
    n_chunks = seq // STRIDE
    out = jax.ShapeDtypeStruct((batch, n_kv, n_chunks, HEAD_DIM), BF16)
    ospec = pl.BlockSpec((None, None, n_chunks, HEAD_DIM), lambda b, h: (b, h, 0, 0))
    return pl.pallas_call(
        functools.partial(_compress_p_kernel, n_chunks=n_chunks),
        out_shape=(out, out),
        grid=(batch, n_kv),
        in_specs=[pl.BlockSpec((seq, HEAD_DIM), lambda b, h: (b, h)),
                  pl.BlockSpec((seq, HEAD_DIM), lambda b, h: (b, n_kv + h)),
                  pl.BlockSpec(w1.shape, lambda b, h: (0, 0, 0)),
                  pl.BlockSpec(w2.shape, lambda b, h: (0, 0, 0)),
                  pl.BlockSpec(pe.shape, lambda b, h: (0, 0, 0))],
        out_specs=(ospec, ospec),
        compiler_params=_cparams("parallel", "parallel"),
        name=name,
    )(kv, kv, w1, w2, pe)


def _block_scores(imp, pos, n_blocks, block_axis):
    j = lax.broadcasted_iota(jnp.int32, imp.shape, block_axis)
    cur = pos // L_SLC
    forced = (j == 0) | (j == cur) | (j == cur - 1)
    valid = j * L_SLC <= pos
    score = jnp.where(valid, jnp.where(forced, SEL_FORCE, imp), -SEL_FORCE)
    return jnp.where(j < n_blocks, score, NEG_BIG)


def _overlap_matrix(nc, nb, rows, cols):
    i = np.arange(nc)[:, None]
    j = np.arange(nb)[None, :]
    lo = np.maximum(i * STRIDE, j * L_SLC)
    hi = np.minimum(i * STRIDE + L_CMP, (j + 1) * L_SLC)
    ov = np.zeros((rows, cols), np.float32)
    ov[:nc, :nb] = np.maximum(hi - lo, 0) / STRIDE
    return ov


def _block_to_key_matrix(n_groups, lanes, keys_per_group):
    ex = np.zeros((n_groups, lanes, keys_per_group), np.float32)
    for c in range(n_groups):
        k = np.arange(keys_per_group)
        ex[c, (c * keys_per_group + k) // L_SLC, k] = 1.0
    return ex


def _attn_p_kernel(q_ref, kc_ref, vc_ref, ks_ref, vs_ref, kw_ref, vw_ref, gt_ref, ovt_ref, ex_ref,
                   o_ref, m_scr, acc_scr, *, tq, tk, gqa, hpb, n_blocks, n_cmp, win_keys):
    qi = pl.program_id(2)
    t0 = qi * tq
    q = q_ref[...] * LOGIT_SCALE
    qs = [[q[:, (hh * gqa + g) * HEAD_DIM:(hh * gqa + g + 1) * HEAD_DIM].astype(BF16)
           for g in range(gqa)] for hh in range(hpb)]
    pos_t = t0 + lax.broadcasted_iota(jnp.int32, (tq, 1), 0)
    pos_row = t0 + lax.broadcasted_iota(jnp.int32, (1, tq), 1)
    ones = jnp.ones((max(tk, win_keys), HEAD_DIM), BF16)
    lanes = ex_ref.shape[1]

    n = lax.broadcasted_iota(jnp.int32, (tq, kc_ref.shape[1]), 1)
    last_end = jnp.minimum(pos_t, (n_cmp - 1) * STRIDE + (L_CMP - 1))
    bias_c = jnp.where(n * STRIDE + (L_CMP - 1) <= last_end, 0.0, NEG_BIG)
    o_c, sel = [], []
    for hh in range(hpb):
        kc = kc_ref[hh]
        vc = vc_ref[hh]
        p_grp = None
        o_c.append([])
        for g in range(gqa):
            sm = _dot_nt(qs[hh][g], kc) + bias_c
            m = jnp.max(sm, axis=-1, keepdims=True)
            e = jnp.exp2(sm - m)
            norm = jnp.where(m > 0.5 * NEG_BIG,
                             1.0 / jnp.maximum(jnp.sum(e, axis=-1, keepdims=True), 1e-30), 0.0)
            p = e * norm
            o_c[hh].append(_dot(p.astype(BF16), vc))
            p_grp = p if p_grp is None else p_grp + p

        hi, lo = _split_hi_lo(p_grp)
        imp_t = _dot_nt(ovt_ref[...], hi) + _dot_nt(ovt_ref[...], lo)
        score = _block_scores(imp_t, pos_row, n_blocks, 0)
        blk = lax.broadcasted_iota(jnp.int32, score.shape, 0)
        rank = jnp.zeros(score.shape, F32)
        for i in range(n_blocks):
            ci = score[i:i + 1, :]
            rank = rank + jnp.where((ci > score) | ((ci == score) & (blk > i)), 1.0, 0.0)
        sel_t = jnp.where(rank < float(min(N_SEL, n_blocks)), 1.0, 0.0)
        sel_t = jnp.concatenate([sel_t, jnp.zeros((lanes - sel_t.shape[0], tq), F32)], axis=0)
        sel.append(sel_t.T.astype(BF16))

    m_scr[...] = jnp.full(m_scr.shape, NEG_BIG, F32)
    acc_scr[...] = jnp.zeros(acc_scr.shape, F32)

    def key_tile(c, carry):
        k0 = pl.multiple_of(c * tk, tk)
        kpos = k0 + lax.broadcasted_iota(jnp.int32, (tq, tk), 1)
        causal = jnp.where(kpos <= pos_t, 0.5, 2.0)
        for hh in range(hpb):
            cols = slice(hh * HEAD_DIM, (hh + 1) * HEAD_DIM)
            kt = ks_ref[pl.ds(k0, tk), cols].astype(BF16)
            vt = jnp.concatenate([vs_ref[pl.ds(k0, tk), cols].astype(BF16), ones[0:tk]], axis=1)
            bias = jnp.where(_dot(sel[hh], ex_ref[c]) > causal, 0.0, NEG_BIG)
            for g in range(gqa):
                rows = slice((hh * gqa + g) * tq, (hh * gqa + g + 1) * tq)
                sm = _dot_nt(qs[hh][g], kt) + bias
                m_old = m_scr[rows]
                m_new = jnp.maximum(m_old, jnp.max(sm, axis=-1, keepdims=True))
                e = jnp.exp2(sm - m_new)
                acc_scr[rows] = jnp.exp2(m_old - m_new) * acc_scr[rows] + _dot(e.astype(BF16), vt)
                m_scr[rows] = m_new
        return carry

    lax.fori_loop(0, (t0 + tq - 1) // tk + 1, key_tile, 0)

    start = pl.multiple_of(jnp.maximum(t0 + tq - win_keys, 0), 128)
    d = pos_t - (start + lax.broadcasted_iota(jnp.int32, (tq, win_keys), 1))
    bias_w = jnp.where(d >= 0, jnp.where(d < WINDOW, 0.0, NEG_BIG), NEG_BIG)
    for hh in range(hpb):
        cols = slice(hh * HEAD_DIM, (hh + 1) * HEAD_DIM)
        kw = kw_ref[pl.ds(start, win_keys), cols].astype(BF16)
        vw = jnp.concatenate([vw_ref[pl.ds(start, win_keys), cols].astype(BF16), ones[0:win_keys]], axis=1)
        gt = gt_ref[hh]
        for g in range(gqa):
            sm = _dot_nt(qs[hh][g], kw) + bias_w
            e = jnp.exp2(sm - jnp.max(sm, axis=-1, keepdims=True))
            ow = _dot(e.astype(BF16), vw)
            o_w = ow[:, 0:HEAD_DIM] / ow[:, HEAD_DIM:HEAD_DIM + 1]
            acc = acc_scr[(hh * gqa + g) * tq:(hh * gqa + g + 1) * tq]
            o_s = acc[:, 0:HEAD_DIM] / jnp.maximum(acc[:, HEAD_DIM:HEAD_DIM + 1], 1e-30)
            out = (gt[:, g:g + 1] * o_c[hh][g] + gt[:, gqa + g:gqa + g + 1] * o_s
                   + gt[:, 2 * gqa + g:2 * gqa + g + 1] * o_w)
            o_ref[:, (hh * gqa + g) * HEAD_DIM:(hh * gqa + g + 1) * HEAD_DIM] = out.astype(o_ref.dtype)


def _attn_prompt(q2d, kv, kcmp, vcmp, gates_h, *, batch, seq, n_heads, n_kv, tq=256, tk=1024, hpb=2,
                 name):
    gqa = n_heads // n_kv
    tq = _tile(seq, tq, 8)
    tk = _tile(seq, tk)
    hpb = _tile(n_kv, hpb, 1)
    n_grp = n_kv // hpb
    nq = seq // tq
    n_chunks = seq // STRIDE
    n_cmp = n_chunks - 1
    n_blocks = -(-seq // L_SLC)
    lanes = -(-n_blocks // 128) * 128
    block_rows = -(-n_blocks // 8) * 8
    ovt = jnp.asarray(_overlap_matrix(n_cmp, n_blocks, n_chunks, block_rows).T.copy(), BF16)
    ex = jnp.asarray(_block_to_key_matrix(seq // tk, lanes, tk), BF16)
    win_keys = min(WINDOW + tq, seq)
    rows = hpb * gqa * tq
    kern = functools.partial(_attn_p_kernel, tq=tq, tk=tk, gqa=gqa, hpb=hpb, n_blocks=n_blocks,
                             n_cmp=n_cmp, win_keys=win_keys)

    def kv_spec(kind):
        return pl.BlockSpec((seq, hpb * HEAD_DIM), lambda b, h, i: (b, kind * n_grp + h))

    cmp_spec = pl.BlockSpec((None, hpb, n_chunks, HEAD_DIM), lambda b, h, i: (b, h, 0, 0))
    return pl.pallas_call(
        kern,
        out_shape=jax.ShapeDtypeStruct((batch * seq, n_heads * HEAD_DIM), BF16),
        grid=(batch, n_grp, nq),
        in_specs=[pl.BlockSpec((tq, hpb * gqa * HEAD_DIM), lambda b, h, i: (b * nq + i, h)),
                  cmp_spec, cmp_spec,
                  kv_spec(2), kv_spec(3), kv_spec(4), kv_spec(5),
                  pl.BlockSpec((hpb, tq, 3 * gqa), lambda b, h, i: (h, b * nq + i, 0)),
                  pl.BlockSpec(ovt.shape, lambda b, h, i: (0, 0)),
                  pl.BlockSpec(ex.shape, lambda b, h, i: (0, 0, 0))],
        out_specs=pl.BlockSpec((tq, hpb * gqa * HEAD_DIM), lambda b, h, i: (b * nq + i, h)),
        scratch_shapes=[pltpu.VMEM((rows, 1), F32),
                        pltpu.VMEM((rows, 2 * HEAD_DIM), F32)],
        compiler_params=_cparams("parallel", "parallel", "arbitrary"),
        name=name,
    )(q2d, kcmp, vcmp, kv, kv, kv, kv, gates_h, ovt, ex)


def _head_slabs(rows_ref, first, n_heads):
    return jnp.concatenate([rows_ref[:, first + h, :] for h in range(n_heads)], axis=1)


def _page_copies(cache_ref, buf_ref, sem_ref, pt_ref, b, step, slot, *, layer, row0, n_rows,
                 pages_per_step):
    out = []
    for k in range(pages_per_step):
        page = pt_ref[b, step * pages_per_step + k]
        for r in range(n_rows):
            out.append(pltpu.make_async_copy(cache_ref.at[layer, page, :, row0 + r, :],
                                             buf_ref.at[slot, k, r], sem_ref.at[slot]))
    return out


def _page_stream(cache_ref, buf_ref, sem_ref, pt_ref, **kw):
    b, p = pl.program_id(0), pl.program_id(1)
    n_b, n_p = pl.num_programs(0), pl.num_programs(1)
    g = b * n_p + p
    slot = lax.rem(g, 2)

    @pl.when(g == 0)
    def _():
        for c in _page_copies(cache_ref, buf_ref, sem_ref, pt_ref, 0, 0, 0, **kw):
            c.start()

    @pl.when(g + 1 < n_b * n_p)
    def _():
        wrap = p + 1 == n_p
        for c in _page_copies(cache_ref, buf_ref, sem_ref, pt_ref, jnp.where(wrap, b + 1, b),
                              jnp.where(wrap, 0, p + 1), 1 - slot, **kw):
            c.start()

    for c in _page_copies(cache_ref, buf_ref, sem_ref, pt_ref, b, p, slot, **kw):
        c.wait()
    return slot


def _cmp_s_kernel(pt_ref, cache_ref, q_ref, w1_ref, w2_ref, pe_ref, ovt_ref, oc_ref, sel_ref,
                  x_scr, ab_scr, buf_ref, sem_ref, *, layer, n_kv, n_steps, pages_per_step,
                  steps_per_group, n_cmp, n_blocks, dec_seq, gqa, past):
    p = pl.program_id(1)
    n_kh = 2 * n_kv
    page = buf_ref.shape[3]
    slot = _page_stream(cache_ref, buf_ref, sem_ref, pt_ref, layer=layer, row0=0, n_rows=n_kh,
                        pages_per_step=pages_per_step)
    cpp = page // STRIDE
    step_chunks = cpp * pages_per_step
    group_chunks = step_chunks * steps_per_group
    c0 = pl.multiple_of(lax.rem(p, steps_per_group) * step_chunks, 8)
    for k in range(pages_per_step):
        for s in range(STRIDE):
            for kh in range(n_kh):
                x_scr[kh, pl.ds(c0 + k * cpp, cpp), s * HEAD_DIM:(s + 1) * HEAD_DIM] = (
                    buf_ref[slot, k, kh, pl.ds(s, cpp, stride=STRIDE), :])

    @pl.when(lax.rem(p, steps_per_group) == steps_per_group - 1)
    def _():
        g0 = pl.multiple_of((p // steps_per_group) * group_chunks, 8)
        for kh in range(n_kh):
            kind = kh // n_kv
            ab_scr[kh, pl.ds(g0, group_chunks), :] = _dot(x_scr[kh].astype(BF16), w1_ref[kind])

    @pl.when(p == n_steps - 1)
    def _():
        n_chunks = ab_scr.shape[1]
        comp = []
        for kh in range(n_kh):
            kind = kh // n_kv
            hpe = _dot(pe_ref[kind], w1_ref[kind])
            hpe = hpe[0:1, 0:HEAD_DIM] + hpe[8:9, HEAD_DIM:2 * HEAD_DIM]
            ab = ab_scr[kh]
            h = ab[:, 0:HEAD_DIM] + pltpu.roll(ab[:, HEAD_DIM:2 * HEAD_DIM], n_chunks - 1, 0) + hpe
            comp.append(_dot(_gelu(h).astype(BF16), w2_ref[kind]).astype(BF16))
        qrows = lax.broadcasted_iota(jnp.int32, (1, HEAD_DIM), 1)
        pos = past + lax.rem(qrows, dec_seq)
        for h in range(n_kv):
            kc, vc = comp[h], comp[n_kv + h]
            st = _dot_nt(kc, (q_ref[h] * LOGIT_SCALE).astype(BF16))
            n = lax.broadcasted_iota(jnp.int32, st.shape, 0)
            pt = _masked_softmax2(st, (n * STRIDE + (L_CMP - 1) <= pos) & (n < n_cmp), axis=0)
            oc_ref[h] = _dot(pt.T.astype(BF16), vc)
            pg = pt
            for g in range(1, gqa):
                pg = pg + pltpu.roll(pt, HEAD_DIM - g * dec_seq, 1)
            hi, lo = _split_hi_lo(pg)
            imp_t = _dot(ovt_ref[...], hi) + _dot(ovt_ref[...], lo)
            imp = imp_t.T
            tpos = past + lax.broadcasted_iota(jnp.int32, (imp.shape[0], 1), 0)
            score = _block_scores(imp, tpos, n_blocks, 1)
            score_t = score.T
            nb_pad = score.shape[1]
            ii = lax.broadcasted_iota(jnp.int32, (nb_pad, nb_pad), 0)
            jj = lax.broadcasted_iota(jnp.int32, (nb_pad, nb_pad), 1)
            for t in range(dec_seq):
                col = score_t[:, t:t + 1]
                rowv = score[t:t + 1, :]
                beats = (col > rowv) | ((col == rowv) & (ii < jj))
                rank = jnp.sum(jnp.where(beats, 1.0, 0.0), axis=0, keepdims=True)
                sel_ref[h, t:t + 1, :] = jnp.where(rank < float(min(N_SEL, n_blocks)), 1.0, 0.0)
            sel_ref[h, dec_seq:, :] = jnp.zeros((sel_ref.shape[1] - dec_seq, nb_pad), F32)


def _cmp_sample(page_table, cache_rows, q_pad, w1cat, w2, pe, *, layer, n_kv, gqa, dec_seq, name):
    batch, n_pages = page_table.shape
    page = cache_rows.shape[2]
    past = n_pages * page
    cpp = page // STRIDE
    n_chunks = past // STRIDE
    assert dec_seq < STRIDE and n_chunks % 8 == 0
    n_cmp = (past + dec_seq) // STRIDE - 1
    n_blocks = -(-(past + dec_seq) // L_SLC)
    nb_pad = -(-n_blocks // 128) * 128
    pps = _tile(n_pages, 4, 1)
    n_steps = n_pages // pps
    spg = _tile(n_steps, max(1, 128 // (cpp * pps)), 1)
    n_kh = 2 * n_kv
    ovt = jnp.asarray(_overlap_matrix(n_cmp, n_blocks, n_chunks, nb_pad).T.copy(), BF16)
    kern = functools.partial(_cmp_s_kernel, layer=layer, n_kv=n_kv, n_steps=n_steps, pages_per_step=pps,
                             steps_per_group=spg, n_cmp=n_cmp, n_blocks=n_blocks, dec_seq=dec_seq,
                             gqa=gqa, past=past)
    grid_spec = pltpu.PrefetchScalarGridSpec(
        num_scalar_prefetch=1,
        grid=(batch, n_steps),
        in_specs=[
            pl.BlockSpec(memory_space=pl.ANY),
            pl.BlockSpec((None, n_kv, 128, HEAD_DIM), lambda b, p, pt: (b, 0, 0, 0)),
            pl.BlockSpec(w1cat.shape, lambda b, p, pt: (0, 0, 0)),
            pl.BlockSpec(w2.shape, lambda b, p, pt: (0, 0, 0)),
            pl.BlockSpec(pe.shape, lambda b, p, pt: (0, 0, 0)),
            pl.BlockSpec(ovt.shape, lambda b, p, pt: (0, 0))],
        out_specs=(pl.BlockSpec((None, n_kv, 128, HEAD_DIM), lambda b, p, pt: (b, 0, 0, 0)),
                   pl.BlockSpec((None, n_kv, 8, nb_pad), lambda b, p, pt: (b, 0, 0, 0))),
        scratch_shapes=[pltpu.VMEM((n_kh, spg * pps * cpp, STRIDE * HEAD_DIM), F32),
                        pltpu.VMEM((n_kh, n_chunks, 2 * HEAD_DIM), F32),
                        pltpu.VMEM((2, pps, n_kh, page, HEAD_DIM), F32),
                        pltpu.SemaphoreType.DMA((2,))],
    )
    return pl.pallas_call(
        kern,
        out_shape=(jax.ShapeDtypeStruct((batch, n_kv, 128, HEAD_DIM), F32),
                   jax.ShapeDtypeStruct((batch, n_kv, 8, nb_pad), F32)),
        grid_spec=grid_spec,
        compiler_params=_cparams("arbitrary", "arbitrary"),
        name=name,
    )(page_table, cache_rows, q_pad, w1cat, w2, pe, ovt)


def _slc_s_kernel(pt_ref, cache_ref, q_ref, sel_ref, ex_ref, kn_ref, vn_ref, win_ref, kwn_ref, vwn_ref,
                  oc_ref, gt_ref, o_ref, qbd_scr, m_scr, l_scr, acc_scr, buf_ref, sem_ref,
                  *, layer, n_kv, n_steps, pages_per_step, rows_per_head, dec_seq, past, w_buf):
    p = pl.program_id(1)
    slot = _page_stream(cache_ref, buf_ref, sem_ref, pt_ref, layer=layer, row0=2 * n_kv,
                        n_rows=2 * n_kv, pages_per_step=pages_per_step)
    rows = n_kv * rows_per_head

    @pl.when(p == 0)
    def _():
        qbd_scr[...] = jnp.zeros(qbd_scr.shape, qbd_scr.dtype)
        for h in range(n_kv):
            qbd_scr[h * rows_per_head:(h + 1) * rows_per_head, h * HEAD_DIM:(h + 1) * HEAD_DIM] = (
                q_ref[h, 0:rows_per_head, :] * LOGIT_SCALE)
        m_scr[...] = jnp.full(m_scr.shape, NEG_BIG, F32)
        l_scr[...] = jnp.zeros(l_scr.shape, F32)
        acc_scr[...] = jnp.zeros(acc_scr.shape, F32)

    qbd = qbd_scr[...].astype(BF16)

    def online_update(s, mask, v):
        sm = jnp.where(mask, s, NEG_BIG)
        m_old = m_scr[...]
        m_new = jnp.maximum(m_old, jnp.max(sm, axis=-1, keepdims=True))
        alpha = jnp.exp2(m_old - m_new)
        e = jnp.where(mask, jnp.exp2(sm - m_new), 0.0)
        l_scr[...] = alpha * l_scr[...] + jnp.sum(e, axis=-1, keepdims=True)
        acc_scr[...] = alpha * acc_scr[...] + _dot(e.astype(BF16), v)
        m_scr[...] = m_new

    ks, vs = [], []
    for k in range(pages_per_step):
        ks.append(jnp.concatenate([buf_ref[slot, k, h] for h in range(n_kv)], axis=1))
        vs.append(jnp.concatenate([buf_ref[slot, k, n_kv + h] for h in range(n_kv)], axis=1))
    k_all = jnp.concatenate(ks, axis=0).astype(BF16)
    v_all = jnp.concatenate(vs, axis=0).astype(BF16)
    s = _dot_nt(qbd, k_all)
    sel_keys = _dot(sel_ref[...].astype(BF16), ex_ref[...])
    online_update(s, sel_keys > 0.5, v_all)

    @pl.when(p == n_steps - 1)
    def _():
        r = lax.broadcasted_iota(jnp.int32, (rows, 1), 0)
        t = lax.rem(r, dec_seq)
        s = _dot_nt(qbd, kn_ref[...].astype(BF16))
        j = lax.broadcasted_iota(jnp.int32, s.shape, 1)
        online_update(s, (j <= t) & (j < dec_seq), vn_ref[...].astype(BF16))
        o_s = acc_scr[...] / jnp.maximum(l_scr[...], 1e-30)

        sb = _dot_nt(qbd, _head_slabs(win_ref, 0, n_kv).astype(BF16))
        sn = _dot_nt(qbd, kwn_ref[...].astype(BF16))
        ib = lax.broadcasted_iota(jnp.int32, sb.shape, 1)
        kpos = past - w_buf + ib
        d = (past + t) - kpos
        mb = (d >= 0) & (d < WINDOW) & (kpos >= 0)
        jn = lax.broadcasted_iota(jnp.int32, sn.shape, 1)
        mn = (jn <= t) & (jn < dec_seq) & (t - jn < WINDOW)
        smb = jnp.where(mb, sb, NEG_BIG)
        smn = jnp.where(mn, sn, NEG_BIG)
        mx = jnp.maximum(jnp.max(smb, axis=-1, keepdims=True), jnp.max(smn, axis=-1, keepdims=True))
        eb = jnp.where(mb, jnp.exp2(smb - mx), 0.0)
        en = jnp.where(mn, jnp.exp2(smn - mx), 0.0)
        den = jnp.maximum(jnp.sum(eb, axis=-1, keepdims=True) + jnp.sum(en, axis=-1, keepdims=True),
                          1e-30)
        o_w = (_dot((eb / den).astype(BF16), _head_slabs(win_ref, n_kv, n_kv).astype(BF16))
               + _dot((en / den).astype(BF16), vwn_ref[...].astype(BF16)))

        for h in range(n_kv):
            r0 = h * rows_per_head
            c0 = h * HEAD_DIM
            gt = gt_ref[h]
            o_ref[h] = (gt[:, 0:1] * oc_ref[h, 0:rows_per_head, :]
                        + gt[:, 1:2] * o_s[r0:r0 + rows_per_head, c0:c0 + HEAD_DIM]
                        + gt[:, 2:3] * o_w[r0:r0 + rows_per_head, c0:c0 + HEAD_DIM])


def _slc_sample(page_table, cache_rows, q_pad, sel_steps, k_new, v_new, win_rows, kw_new, vw_new, o_c,
                gates, *, layer, n_kv, gqa, dec_seq, pages_per_step, name):
    batch, n_pages = page_table.shape
    page = cache_rows.shape[2]
    past = n_pages * page
    width = n_kv * HEAD_DIM
    n_kh = 2 * n_kv
    rph = gqa * dec_seq
    rows = n_kv * rph
    w_buf = win_rows.shape[2]
    n_new = k_new.shape[1]
    pps = pages_per_step
    n_steps = n_pages // pps
    ex = jnp.asarray(_block_to_key_matrix(1, sel_steps.shape[3], pps * page)[0], BF16)
    kern = functools.partial(_slc_s_kernel, layer=layer, n_kv=n_kv, n_steps=n_steps, pages_per_step=pps,
                             rows_per_head=rph, dec_seq=dec_seq, past=past, w_buf=w_buf)

    def new_spec():
        return pl.BlockSpec((None, n_new, width), lambda b, p, pt: (b, 0, 0))

    grid_spec = pltpu.PrefetchScalarGridSpec(
        num_scalar_prefetch=1,
        grid=(batch, n_steps),
        in_specs=[
            pl.BlockSpec(memory_space=pl.ANY),
            pl.BlockSpec((None, n_kv, 128, HEAD_DIM), lambda b, p, pt: (b, 0, 0, 0)),
            pl.BlockSpec((None, None, rows, sel_steps.shape[3]), lambda b, p, pt: (b, p, 0, 0)),
            pl.BlockSpec(ex.shape, lambda b, p, pt: (0, 0)),
            new_spec(), new_spec(),
            pl.BlockSpec((None, None, w_buf, n_kh, HEAD_DIM), lambda b, p, pt: (layer, b, 0, 0, 0)),
            new_spec(), new_spec(),
            pl.BlockSpec((None, n_kv, 128, HEAD_DIM), lambda b, p, pt: (b, 0, 0, 0)),
            pl.BlockSpec((None, n_kv, rph, 8), lambda b, p, pt: (b, 0, 0, 0))],
        out_specs=pl.BlockSpec((None, n_kv, rph, HEAD_DIM), lambda b, p, pt: (b, 0, 0, 0)),
        scratch_shapes=[pltpu.VMEM((rows, width), F32),
                        pltpu.VMEM((rows, 1), F32),
                        pltpu.VMEM((rows, 1), F32),
                        pltpu.VMEM((rows, width), F32),
                        pltpu.VMEM((2, pps, n_kh, page, HEAD_DIM), F32),
                        pltpu.SemaphoreType.DMA((2,))],
    )
    return pl.pallas_call(
        kern,
        out_shape=jax.ShapeDtypeStruct((batch, n_kv, rph, HEAD_DIM), F32),
        grid_spec=grid_spec,
        compiler_params=_cparams("arbitrary", "arbitrary"),
        name=name,
    )(page_table, cache_rows, q_pad, sel_steps, ex, k_new, v_new, win_rows, kw_new, vw_new, o_c,
      gates)


def _rope_tables(pos):
    inv = ROPE_THETA ** (-jnp.arange(ROT_HALF, dtype=F32) * 2.0 / ROT_DIM)
    ang = pos.astype(F32)[:, None] * inv[None, :]
    cos, sin = jnp.cos(ang), jnp.sin(ang)
    rest = HEAD_DIM - ROT_DIM
    cos_t = jnp.concatenate([cos, cos, jnp.ones((pos.shape[0], rest), F32)], axis=1)
    sin_t = jnp.concatenate([-sin, sin, jnp.zeros((pos.shape[0], rest), F32)], axis=1)
    return cos_t, sin_t


def _prep_weights(w_in, ffn_conv_b, n_gate):
    w_gate = jnp.pad(w_in[:, :, w_in.shape[2] - n_gate:], ((0, 0), (0, 0), (0, 128 - n_gate)))
    return dict(w_gate=w_gate.astype(BF16),
                f_cb=ffn_conv_b.reshape(ffn_conv_b.shape[0], 1, ffn_conv_b.shape[1]))


def _prep_compress(l, cmp_pe, cmp_w1, cmp_w2):
    half = STRIDE * HEAD_DIM
    w1 = cmp_w1[l].reshape(2, 2 * half, HEAD_DIM)
    wl = {}
    wl["cmp_w1"] = w1.astype(BF16)
    wl["cmp_w1cat"] = jnp.concatenate([w1[:, :half], w1[:, half:]], axis=2).astype(BF16)
    wl["cmp_w2"] = cmp_w2[l].astype(BF16)
    pe = cmp_pe[l].reshape(2, 1, 2 * half)
    wl["cmp_pe"] = jnp.broadcast_to(pe, (2, 16, 2 * half)).astype(BF16)
    pe2 = cmp_pe[l].reshape(2, 2, 1, half)
    wl["cmp_pecat"] = jnp.concatenate([jnp.broadcast_to(pe2[:, 0], (2, 8, half)),
                                       jnp.broadcast_to(pe2[:, 1], (2, 8, half))], axis=1).astype(BF16)
    return wl


def _project(x_bf, ws, wb, l, cos_t, sin_t, dims, tag):
    d_a, d_b, d_kv, n_heads, d_ff = dims
    n_kv = d_kv // HEAD_DIM
    emit = wb is None
    if emit:
        src = dict(uv=(ws["w_in_t"], l, 0), q=(ws["w_in_t"], l, 2 * d_a), kv=(ws["w_in_t"], l, 2 * d_a + d_b))
    else:
        src = dict(uv=(wb["uv"], 0, 0), q=(wb["q"], 0, 0), kv=(wb["kv"], 0, 0))
    new = {}
    w, lay, c0 = src["uv"]
    out = _matmul([x_bf], w, layer=lay, col_start=c0, n=2 * d_a, epilogue="gelu", tm=1024,
                  tn=512 if emit else 1024, emit=emit, w_transposed=emit, name=f"proj_uv_{tag}")
    uv = out[0] if emit else out
    if emit:
        new["uv"] = out[1]
    w, lay, c0 = src["q"]
    out = _rope_matmul(x_bf, w, cos_t, sin_t, layer=lay, col_start=c0, n=d_b, tn=d_kv, alternate=False,
                       emit=emit, name=f"proj_q_{tag}")
    q2d = out[0]
    if emit:
        new["q"] = out[-1]
    w, lay, c0 = src["kv"]
    out = _rope_matmul(x_bf, w, cos_t, sin_t, layer=lay, col_start=c0, n=6 * d_kv, tn=d_kv, alternate=True,
                       n_kv=n_kv, emit=emit, name=f"proj_kv_{tag}")
    kv, nsa4, win4 = out[0], out[1], out[2]
    if emit:
        new["kv"] = out[-1]
    gates = _matmul([x_bf], ws["w_gate"], layer=l, epilogue="sigmoid", tm=1024, tn=128,
                    name=f"proj_gate_{tag}")
    return uv, q2d, kv, nsa4, win4, gates, new


def _mix_and_norm(x, a_out, b_out, ws, wb, l, ln_g, ln_b, alpha, tag):
    new = {}
    if wb is None:
        z, new["o_a"], new["o_b"] = _matmul([a_out, b_out], ws["w_o"], layer=l, tm=1024, tn=512,
                                            emit=True, residual=x, res_scale=alpha, name=f"w_o_{tag}")
    else:
        z = _matmul([a_out, b_out], [wb["o_a"], wb["o_b"]], layer=0, tm=1024, tn=1024,
                    residual=x, res_scale=alpha, name=f"w_o_{tag}")
    y, yb = _layer_norm(z, ln_g[l, 0], ln_b[l, 0], name=f"ln1_{tag}")
    return y, yb, new


def _layer_prompt(x, x_bf, ws, wb, wl, l, p, cos_t, sin_t, dims, batch, seq, n_kv, alpha):
    d_a, d_b, d_kv, n_heads, d_ff = dims
    gqa = n_heads // n_kv
    m = batch * seq
    uv, q2d, kv, nsa4, win4, gates, _ = _project(x_bf, ws, wb, l, cos_t, sin_t, dims, "p")
    (a_out,) = _sgu(uv, p["sgu_w"][l], p["sgu_b"][l].T, p["sgu_g"][l], rows=CHUNK, with_vn=False,
                    name="sgu_p")
    kcmp, vcmp = _compress_prompt(kv, wl["cmp_w1"], wl["cmp_w2"], wl["cmp_pe"], batch=batch, seq=seq,
                                  n_kv=n_kv, name="compress_p")
    gates_h = gates[:, :3 * n_heads].reshape(m, 3, n_kv, gqa).transpose(2, 0, 1, 3).reshape(n_kv, m, 3 * gqa)
    b_out = _attn_prompt(q2d, kv, kcmp, vcmp, gates_h, batch=batch, seq=seq, n_heads=n_heads, n_kv=n_kv,
                         name="nsa_p")
    x1, x1b, _ = _mix_and_norm(x, a_out, b_out, ws, wb, l, p["ln_g"], p["ln_b"], alpha, "p")
    state0 = jnp.zeros((batch, CONV_W - 1, d_ff), F32)
    act, conv_new = _ffn_in_seq(x1b, wb["f_gate"], wb["f_up"], p["ffn_conv_w"], ws["f_cb"], state0,
                                layer=l, seq_len=seq, name="ffn_in_p")
    z = _matmul([act], wb["f_down"], layer=0, tm=512, tn=512, residual=x1, res_scale=alpha,
                name="ffn_down_p")
    x2, x2b = _layer_norm(z, p["ln_g"][l, 1], p["ln_b"][l, 1], name="ln2_p")
    new_nsa = nsa4.reshape(batch, seq, 4, n_kv, HEAD_DIM)
    new_win = win4.reshape(batch, seq, 2, n_kv, HEAD_DIM)[:, -min(WINDOW, seq):]
    return x2, x2b, new_nsa, new_win, conv_new


def _layer_sample(x, x_bf, ws, wl, l, p, cos_t, sin_t, dims, batch, dec_seq, n_kv, alpha, page_table,
                  cache_rows, win_rows, conv_state):
    d_a, d_b, d_kv, n_heads, d_ff = dims
    gqa = n_heads // n_kv
    m = batch * dec_seq
    rph = gqa * dec_seq
    uv, q2d, kv, nsa4, win4, gates, wb = _project(x_bf, ws, None, l, cos_t, sin_t, dims, "s")
    w_small = p["sgu_w"][l][:, :dec_seq, :dec_seq]
    eye = jnp.eye(batch, dtype=F32)
    w_bd = jnp.einsum("ab,gts->gatbs", eye, w_small).reshape(-1, m, m)
    bt_bd = jnp.tile(p["sgu_b"][l].T[:dec_seq], (batch, 1))
    a_out, v_rows = _sgu(uv, w_bd, bt_bd, p["sgu_g"][l], rows=m, with_vn=True, name="sgu_s")

    q = q2d.reshape(batch, dec_seq, n_kv, gqa, HEAD_DIM).transpose(0, 2, 3, 1, 4)
    q_pad = jnp.pad(q.reshape(batch, n_kv, rph, HEAD_DIM), ((0, 0), (0, 0), (0, 128 - rph), (0, 0)))
    o_c, sel = _cmp_sample(page_table, cache_rows, q_pad, wl["cmp_w1cat"], wl["cmp_w2"], wl["cmp_pecat"],
                           layer=l, n_kv=n_kv, gqa=gqa, dec_seq=dec_seq, name="cmp_s")
    n_pages = page_table.shape[1]
    bpp = cache_rows.shape[2] // L_SLC
    pps = _tile(n_pages, 4, 1)
    n_steps = n_pages // pps
    sel_steps = sel[:, :, :dec_seq, :n_pages * bpp].reshape(batch, n_kv, 1, dec_seq, n_steps, pps * bpp)
    sel_steps = jnp.broadcast_to(sel_steps, (batch, n_kv, gqa, dec_seq, n_steps, pps * bpp))
    sel_steps = sel_steps.transpose(0, 4, 1, 2, 3, 5).reshape(batch, n_steps, n_kv * rph, pps * bpp)
    sel_steps = jnp.pad(sel_steps, ((0, 0), (0, 0), (0, 0), (0, 128 - pps * bpp)))

    def new_rows(kind):
        rows = kv[:, kind * d_kv:(kind + 1) * d_kv].reshape(batch, dec_seq, d_kv)
        return jnp.pad(rows, ((0, 0), (0, 128 - dec_seq), (0, 0)))

    gates_s = gates[:, :3 * n_heads].reshape(batch, dec_seq, 3, n_kv, gqa).transpose(0, 3, 4, 1, 2)
    gates_s = jnp.pad(gates_s.reshape(batch, n_kv, rph, 3), ((0, 0), (0, 0), (0, 0), (0, 5)))
    b_rows = _slc_sample(page_table, cache_rows, q_pad, sel_steps, new_rows(2), new_rows(3), win_rows,
                         new_rows(4), new_rows(5), o_c, gates_s, layer=l, n_kv=n_kv, gqa=gqa,
                         dec_seq=dec_seq, pages_per_step=pps, name="slc_s")
    b_out = b_rows.reshape(batch, n_kv, gqa, dec_seq, HEAD_DIM).transpose(0, 3, 1, 2, 4)
    b_out = b_out.reshape(m, d_b).astype(BF16)

    x1, x1b, wb_o = _mix_and_norm(x, a_out, b_out, ws, None, l, p["ln_g"], p["ln_b"], alpha, "s")
    wb.update(wb_o)
    st = conv_state
    zero = jnp.zeros((batch, dec_seq - 1, d_ff), F32)
    h1 = jnp.concatenate([st[:, 1:2], zero], axis=1).reshape(m, d_ff)
    h2 = jnp.concatenate([st, zero[:, 1:]], axis=1).reshape(m, d_ff)
    act, gate, wb["f_gate"], wb["f_up"] = _ffn_in_short(
        x1b, ws["f_in"], p["ffn_conv_w"], ws["f_cb"], h1, h2, layer=l, seq_len=dec_seq, name="ffn_in_s")
    z, wb["f_down"] = _matmul([act], ws["f_down"], layer=l, tn=256, emit=True, residual=x1,
                              res_scale=alpha, name="ffn_down_s")
    x2, x2b = _layer_norm(z, p["ln_g"][l, 1], p["ln_b"][l, 1], name="ln2_s")
    new_nsa = nsa4.reshape(batch, dec_seq, 4, n_kv, HEAD_DIM)
    new_win = win4.reshape(batch, dec_seq, 2, n_kv, HEAD_DIM)
    conv_new = gate.reshape(batch, dec_seq, d_ff)[:, dec_seq - (CONV_W - 1):]
    return x2, x2b, new_nsa, new_win, v_rows.reshape(batch, dec_seq, d_a), conv_new, wb


def kernel(x_prompt, x_sample, cache_nsa_kv, cache_win_kv, state_ffn_conv, page_table, w_in, sgu_w,
           sgu_b, sgu_g, cmp_pe, cmp_w1, cmp_w2, w_o, ln_g, ln_b, ffn_w_in, ffn_conv_w, ffn_conv_b,
           ffn_w_down):
    bp, seq, d_model = x_prompt.shape
    bs, dec_seq, _ = x_sample.shape
    depth = w_in.shape[0]
    n_kv = cache_nsa_kv.shape[4]
    page = cache_nsa_kv.shape[2]
    past = page_table.shape[1] * page
    d_a = d_model // 2
    d_b = d_model - d_a
    n_heads = d_b // HEAD_DIM
    d_kv = n_kv * HEAD_DIM
    d_ff = ffn_conv_w.shape[-1]
    dims = (d_a, d_b, d_kv, n_heads, d_ff)
    alpha = (2 * depth) ** 0.25
    assert dec_seq >= CONV_W - 1 and seq % CHUNK == 0

    cos_p, sin_p = _rope_tables(jnp.tile(jnp.arange(seq, dtype=jnp.int32), bp))
    cos_s, sin_s = _rope_tables(jnp.tile(past + jnp.arange(dec_seq, dtype=jnp.int32), bs))
    cache_rows = cache_nsa_kv.reshape(depth, cache_nsa_kv.shape[1], page, 4 * n_kv, HEAD_DIM)
    win_rows = cache_win_kv.reshape(depth, bs, cache_win_kv.shape[2], 2 * n_kv, HEAD_DIM)
    ws = _prep_weights(w_in, ffn_conv_b, 3 * n_heads)
    ws.update(w_in_t=jnp.swapaxes(w_in, 1, 2), w_o=w_o, f_in=ffn_w_in, f_down=ffn_w_down)
    p = dict(sgu_w=sgu_w, sgu_b=sgu_b, sgu_g=sgu_g, ln_g=ln_g, ln_b=ln_b, ffn_conv_w=ffn_conv_w)

    xp = x_prompt.reshape(bp * seq, d_model)
    xs = x_sample.reshape(bs * dec_seq, d_model)
    xp_bf, xs_bf = xp.astype(BF16), xs.astype(BF16)
    outs = [[] for _ in range(7)]
    for l in range(depth):
        wl = _prep_compress(l, cmp_pe, cmp_w1, cmp_w2)
        xs, xs_bf, nsa_s, win_s, v_s, conv_s, wb = _layer_sample(
            xs, xs_bf, ws, wl, l, p, cos_s, sin_s, dims, bs, dec_seq, n_kv, alpha, page_table,
            cache_rows, win_rows, state_ffn_conv[l])
        xp, xp_bf, nsa_p, win_p, conv_p = _layer_prompt(
            xp, xp_bf, ws, wb, wl, l, p, cos_p, sin_p, dims, bp, seq, n_kv, alpha)
        for acc, val in zip(outs, (nsa_p, nsa_s, win_p, win_s, v_s, conv_p, conv_s)):
            acc.append(val)
    return (xp.reshape(bp, seq, d_model), xs.reshape(bs, dec_seq, d_model),
            *[jnp.stack(o) for o in outs])
```

```python
import functools
import math

import jax
import jax.numpy as jnp
import numpy as np
from jax import lax
from jax.experimental import pallas as pl
from jax.experimental.pallas import tpu as pltpu

HEAD_DIM = 128
CHUNK = 128
STRIDE = 16
L_CMP = 2 * STRIDE
L_SLC = 64
N_SEL = 16
WINDOW = 512
ROT_DIM = HEAD_DIM // 4
ROT_HALF = ROT_DIM // 2
ROPE_THETA = 500000.0
CONV_W = 3
LN_EPS = 1e-5
SCALE = HEAD_DIM ** -0.5
LOGIT_SCALE = SCALE * math.log2(math.e)
SEL_FORCE = 1e9
NEG_BIG = -3.0e38
VMEM_LIMIT = 56 * 1024 * 1024

F32 = jnp.float32
BF16 = jnp.bfloat16


def _cparams(*sem):
    return pltpu.CompilerParams(dimension_semantics=sem, vmem_limit_bytes=VMEM_LIMIT)


def _tile(n, pref, unit=128):
    if n <= pref:
        return n
    t = (pref // unit) * unit
    while t > unit and n % t:
        t -= unit
    assert n % t == 0, (n, pref, unit)
    return t


def _gelu(x):
    return jax.nn.gelu(x, approximate=True)


def _dot(a, b):
    return jnp.dot(a, b, preferred_element_type=F32)


def _dot_nt(a, b):
    return lax.dot_general(a, b, (((1,), (1,)), ((), ())), preferred_element_type=F32)


def _split_hi_lo(x):
    hi = x.astype(BF16)
    lo = (x - hi.astype(F32)).astype(BF16)
    return hi, lo


def _masked_softmax2(s, mask, axis=-1):
    sm = jnp.where(mask, s, NEG_BIG)
    m = jnp.max(sm, axis=axis, keepdims=True)
    p = jnp.where(mask, jnp.exp2(sm - m), 0.0)
    return p / jnp.maximum(jnp.sum(p, axis=axis, keepdims=True), 1e-30)


def _mm_kernel(*refs, n_lhs, epilogue, emit, res_scale, w_transposed):
    n_in = 2 * n_lhs + (res_scale is not None)
    ws = [refs[n_lhs + k][...] for k in range(n_lhs)]
    if w_transposed:
        ws = [w.T for w in ws]
    if emit:
        ws = [w.astype(BF16) for w in ws]
        for k in range(n_lhs):
            refs[n_in + 1 + k][...] = ws[k]
    acc = _dot(refs[0][...], ws[0])
    for k in range(1, n_lhs):
        acc = acc + _dot(refs[k][...], ws[k])
    if res_scale is not None:
        acc = res_scale * refs[n_in - 1][...] + acc
    o_ref = refs[n_in]
    if epilogue == "gelu":
        acc = _gelu(acc)
    elif epilogue == "sigmoid":
        acc = jax.nn.sigmoid(acc)
    o_ref[...] = acc.astype(o_ref.dtype)


def _matmul(xs, w, *, layer, col_start=0, n=None, epilogue="none", out_dtype=F32, tm=512, tn=512,
            emit=False, residual=None, res_scale=None, w_transposed=False, name):
    m, kdim = xs[0].shape
    n = (w[0] if isinstance(w, (list, tuple)) else w).shape[1 if w_transposed else 2] if n is None else n
    tm = _tile(m, tm, 8)
    tn = _tile(n, tn)
    assert col_start % tn == 0 and all(x.shape == (m, kdim) for x in xs)
    c0 = col_start // tn
    n_lhs = len(xs)
    in_specs = [pl.BlockSpec((tm, kdim), lambda i, j: (i, 0)) for _ in xs]
    separate = isinstance(w, (list, tuple))
    w_list = list(w) if separate else [w] * n_lhs
    if w_transposed:
        assert n_lhs == 1
        in_specs += [pl.BlockSpec((None, tn, kdim), lambda i, j: (layer, c0 + j, 0))]
    else:
        in_specs += [pl.BlockSpec((None, kdim, tn), lambda i, j, r=r: (layer, 0 if separate else r, c0 + j))
                     for r in range(n_lhs)]
    extra = []
    if residual is not None:
        in_specs.append(pl.BlockSpec((tm, tn), lambda i, j: (i, j)))
        extra = [residual]
    out_shape = [jax.ShapeDtypeStruct((m, n), out_dtype)]
    out_specs = [pl.BlockSpec((tm, tn), lambda i, j: (i, j))]
    if emit:
        assert m == tm
        out_shape += [jax.ShapeDtypeStruct((1, kdim, n), BF16)] * n_lhs
        out_specs += [pl.BlockSpec((None, kdim, tn), lambda i, j: (0, 0, j))] * n_lhs
    out = pl.pallas_call(
        functools.partial(_mm_kernel, n_lhs=n_lhs, epilogue=epilogue, emit=emit,
                          res_scale=res_scale if residual is not None else None,
                          w_transposed=w_transposed),
        out_shape=tuple(out_shape),
        grid=(m // tm, n // tn),
        in_specs=in_specs,
        out_specs=tuple(out_specs),
        compiler_params=_cparams("parallel", "arbitrary"),
        name=name,
    )(*xs, *w_list, *extra)
    return out if emit else out[0]


def _rope_mm_kernel(x_ref, w_ref, cos_ref, sin_ref, o_ref, *more_refs, heads_per_tile, alternate,
                    n_nsa_kinds, emit):
    j = pl.program_id(1)
    w = w_ref[...]
    cache_refs = more_refs
    if emit:
        w = w.T.astype(BF16)
        more_refs[-1][...] = w
        cache_refs = more_refs[:-1]
    acc = _dot(x_ref[...], w)
    cosv, sinv = cos_ref[...], sin_ref[...]
    if alternate:
        rot = lax.rem(j, 2) == 0
        cosv = jnp.where(rot, cosv, 1.0)
        sinv = jnp.where(rot, sinv, 0.0)
    lane = lax.broadcasted_iota(jnp.int32, cosv.shape, 1)
    heads = []
    for h in range(heads_per_tile):
        hs = acc[:, h * HEAD_DIM:(h + 1) * HEAD_DIM]
        partner = jnp.where(lane < ROT_HALF,
                            pltpu.roll(hs, HEAD_DIM - ROT_HALF, 1),
                            pltpu.roll(hs, ROT_HALF, 1))
        heads.append(hs * cosv + partner * sinv)
        o_ref[:, h * HEAD_DIM:(h + 1) * HEAD_DIM] = heads[h]
    if cache_refs:
        nsa_ref, win_ref = cache_refs

        @pl.when(j < n_nsa_kinds)
        def _():
            for h in range(heads_per_tile):
                nsa_ref[:, h, :] = heads[h]

        @pl.when(j >= n_nsa_kinds)
        def _():
            for h in range(heads_per_tile):
                win_ref[:, h, :] = heads[h]


def _rope_matmul(x, w, cos_t, sin_t, *, layer, col_start, n, tn, alternate, n_kv=None, tm=1024,
                 emit=False, name):
    m, k = x.shape
    tm = _tile(m, tm, 8)
    assert col_start % tn == 0 and n % tn == 0
    c0 = col_start // tn
    hpt = tn // HEAD_DIM
    out_shape = [jax.ShapeDtypeStruct((m, n), F32)]
    out_specs = [pl.BlockSpec((tm, tn), lambda i, j: (i, j))]
    n_nsa = 4
    if alternate:
        assert hpt == n_kv and n == 6 * tn
        out_shape += [jax.ShapeDtypeStruct((m, n_nsa, n_kv, HEAD_DIM), F32),
                      jax.ShapeDtypeStruct((m, 2, n_kv, HEAD_DIM), F32)]
        out_specs += [pl.BlockSpec((tm, None, n_kv, HEAD_DIM),
                                   lambda i, j: (i, jnp.minimum(j, n_nsa - 1), 0, 0)),
                      pl.BlockSpec((tm, None, n_kv, HEAD_DIM),
                                   lambda i, j: (i, jnp.maximum(j - n_nsa, 0), 0, 0))]
    if emit:
        assert m == tm
        out_shape += [jax.ShapeDtypeStruct((1, k, n), BF16)]
        out_specs += [pl.BlockSpec((None, k, tn), lambda i, j: (0, 0, j))]
    kern = functools.partial(_rope_mm_kernel, heads_per_tile=hpt, alternate=alternate,
                             n_nsa_kinds=n_nsa, emit=emit)
    return pl.pallas_call(
        kern,
        out_shape=tuple(out_shape),
        grid=(m // tm, n // tn),
        in_specs=[pl.BlockSpec((tm, k), lambda i, j: (i, 0)),
                  (pl.BlockSpec((None, tn, k), lambda i, j: (layer, c0 + j, 0)) if emit else
                   pl.BlockSpec((None, k, tn), lambda i, j: (layer, 0, c0 + j))),
                  pl.BlockSpec((tm, HEAD_DIM), lambda i, j: (i, 0)),
                  pl.BlockSpec((tm, HEAD_DIM), lambda i, j: (i, 0))],
        out_specs=tuple(out_specs),
        compiler_params=_cparams("arbitrary", "arbitrary"),
        name=name,
    )(x, w, cos_t, sin_t)


def _ln_kernel(z_ref, g_ref, b_ref, y_ref, yb_ref):
    z = z_ref[...]
    mu = jnp.mean(z, axis=-1, keepdims=True)
    zc = z - mu
    var = jnp.mean(zc * zc, axis=-1, keepdims=True)
    y = zc * lax.rsqrt(var + LN_EPS) * g_ref[...] + b_ref[...]
    y_ref[...] = y
    yb_ref[...] = y.astype(BF16)


def _layer_norm(z, g, b, *, name):
    m, d = z.shape
    tr = _tile(m, 256, 8)
    return pl.pallas_call(
        _ln_kernel,
        out_shape=(jax.ShapeDtypeStruct((m, d), F32), jax.ShapeDtypeStruct((m, d), BF16)),
        grid=(m // tr,),
        in_specs=[pl.BlockSpec((tr, d), lambda i: (i, 0)),
                  pl.BlockSpec((1, d), lambda i: (0, 0)),
                  pl.BlockSpec((1, d), lambda i: (0, 0))],
        out_specs=(pl.BlockSpec((tr, d), lambda i: (i, 0)),
                   pl.BlockSpec((tr, d), lambda i: (i, 0))),
        compiler_params=_cparams("parallel"),
        name=name,
    )(z, g.reshape(1, d), b.reshape(1, d))


def _sgu_kernel(uv_ref, w_ref, bt_ref, g_ref, a_ref, *vn_refs, d_a, n_groups):
    rows = w_ref.shape[1]
    r = lax.broadcasted_iota(jnp.int32, (rows, rows), 0)
    c = lax.broadcasted_iota(jnp.int32, (rows, rows), 1)
    causal = r >= c
    for g in range(n_groups):
        lo = g * HEAD_DIM
        v = uv_ref[:, d_a + lo:d_a + lo + HEAD_DIM]
        mu = jnp.mean(v, axis=-1, keepdims=True)
        vc = v - mu
        var = jnp.mean(vc * vc, axis=-1, keepdims=True)
        vn = vc * lax.rsqrt(var + LN_EPS) * g_ref[:, lo:lo + HEAD_DIM]
        if vn_refs:
            vn_refs[0][:, lo:lo + HEAD_DIM] = vn
        w = jnp.where(causal, w_ref[g], 0.0).astype(BF16)
        mixed = _dot(w, vn.astype(BF16)) + bt_ref[:, g:g + 1]
        a_ref[:, lo:lo + HEAD_DIM] = (uv_ref[:, lo:lo + HEAD_DIM] * mixed).astype(a_ref.dtype)


def _sgu(uv, w, bt, gain, *, rows, with_vn, name):
    m = uv.shape[0]
    d_a = uv.shape[1] // 2
    n_groups = d_a // HEAD_DIM
    n_out = 2 if with_vn else 1
    return pl.pallas_call(
        functools.partial(_sgu_kernel, d_a=d_a, n_groups=n_groups),
        out_shape=(jax.ShapeDtypeStruct((m, d_a), BF16), jax.ShapeDtypeStruct((m, d_a), F32))[:n_out],
        grid=(m // rows,),
        in_specs=[pl.BlockSpec((rows, 2 * d_a), lambda i: (i, 0)),
                  pl.BlockSpec((n_groups, rows, rows), lambda i: (0, 0, 0)),
                  pl.BlockSpec((rows, n_groups), lambda i: (0, 0)),
                  pl.BlockSpec((1, d_a), lambda i: (0, 0))],
        out_specs=(pl.BlockSpec((rows, d_a), lambda i: (i, 0)),
                   pl.BlockSpec((rows, d_a), lambda i: (i, 0)))[:n_out],
        compiler_params=_cparams("parallel"),
        name=name,
    )(uv, w, bt, gain.reshape(1, d_a))


def _conv_act(gate, g1, g2, up, cw_ref, cb_ref):
    c = cb_ref[...] + g2 * cw_ref[0:1, :] + g1 * cw_ref[1:2, :] + gate * cw_ref[2:3, :]
    return _gelu(c) * up


def _ffn_in_seq_kernel(x_ref, *refs, tiles_per_seq, n_sub):
    wg, wu, cw, cb, st = (refs[k * n_sub:(k + 1) * n_sub] for k in range(5))
    a_ref, cn_ref, carry_ref = refs[5 * n_sub:]
    i = pl.program_id(1)
    x = x_ref[...]
    tm = x.shape[0]
    tn = wg[0].shape[1]
    first = lax.rem(i, tiles_per_seq) == 0
    row = lax.broadcasted_iota(jnp.int32, (tm, tn), 0)
    for k in range(n_sub):
        cols = slice(k * tn, (k + 1) * tn)
        gate = _dot(x, wg[k][...])
        up = _dot(x, wu[k][...])
        prev2 = jnp.where(first, st[k][0:1, :], carry_ref[0:1, cols])
        prev1 = jnp.where(first, st[k][1:2, :], carry_ref[1:2, cols])
        g1 = jnp.where(row == 0, prev1, pltpu.roll(gate, 1, 0))
        g2 = jnp.where(row == 0, prev2, jnp.where(row == 1, prev1, pltpu.roll(gate, 2, 0)))
        a_ref[:, cols] = _conv_act(gate, g1, g2, up, cw[k], cb[k]).astype(a_ref.dtype)
        tail = gate[tm - 2:tm, :]
        carry_ref[0:2, cols] = tail
        cn_ref[:, cols] = tail


def _ffn_in_seq(x, wg, wu, cw, cb, state, *, layer, seq_len, tm=1024, tn=256, n_sub=2, name):
    m, k = x.shape
    d_ff = cw.shape[2]
    tm = _tile(seq_len, tm, 8)
    tn = _tile(d_ff, tn)
    n_tiles = d_ff // tn
    tps = seq_len // tm
    n_seq = m // seq_len

    def col(j, s):
        return jnp.minimum(j * n_sub + s, n_tiles - 1)

    subs = range(n_sub)
    in_specs = [pl.BlockSpec((tm, k), lambda j, i: (i, 0))]
    in_specs += [pl.BlockSpec((None, k, tn), lambda j, i, s=s: (0, 0, col(j, s))) for s in subs] * 2
    in_specs += [pl.BlockSpec((None, CONV_W, tn), lambda j, i, s=s: (layer, 0, col(j, s))) for s in subs]
    in_specs += [pl.BlockSpec((None, 1, tn), lambda j, i, s=s: (layer, 0, col(j, s))) for s in subs]
    in_specs += [pl.BlockSpec((None, CONV_W - 1, tn), lambda j, i, s=s: (i // tps, 0, col(j, s)))
                 for s in subs]
    return pl.pallas_call(
        functools.partial(_ffn_in_seq_kernel, tiles_per_seq=tps, n_sub=n_sub),
        out_shape=(jax.ShapeDtypeStruct((m, d_ff), BF16),
                   jax.ShapeDtypeStruct((n_seq, CONV_W - 1, d_ff), F32)),
        grid=(pl.cdiv(n_tiles, n_sub), m // tm),
        in_specs=in_specs,
        out_specs=(pl.BlockSpec((tm, n_sub * tn), lambda j, i: (i, j)),
                   pl.BlockSpec((None, CONV_W - 1, n_sub * tn), lambda j, i: (i // tps, 0, j))),
        scratch_shapes=[pltpu.VMEM((8, n_sub * tn), F32)],
        compiler_params=_cparams("arbitrary", "arbitrary"),
        name=name,
    )(x, *([wg] * n_sub), *([wu] * n_sub), *([cw] * n_sub), *([cb] * n_sub), *([state] * n_sub))


def _ffn_in_short_kernel(x_ref, wg_ref, wu_ref, cw_ref, cb_ref, h1_ref, h2_ref, a_ref, gate_ref,
                         wgb_ref, wub_ref, *, seq_len):
    x = x_ref[...]
    wg = wg_ref[...].astype(BF16)
    wu = wu_ref[...].astype(BF16)
    wgb_ref[...] = wg
    wub_ref[...] = wu
    gate = _dot(x, wg)
    up = _dot(x, wu)
    t = lax.rem(lax.broadcasted_iota(jnp.int32, gate.shape, 0), seq_len)
    g1 = jnp.where(t >= 1, pltpu.roll(gate, 1, 0), h1_ref[...])
    g2 = jnp.where(t >= 2, pltpu.roll(gate, 2, 0), h2_ref[...])
    a_ref[...] = _conv_act(gate, g1, g2, up, cw_ref, cb_ref).astype(a_ref.dtype)
    gate_ref[...] = gate


def _ffn_in_short(x, w, cw, cb, h1, h2, *, layer, seq_len, tn=256, name):
    m, k = x.shape
    d_ff = cw.shape[2]
    tn = _tile(d_ff, tn)
    n_tiles = d_ff // tn
    return pl.pallas_call(
        functools.partial(_ffn_in_short_kernel, seq_len=seq_len),
        out_shape=(jax.ShapeDtypeStruct((m, d_ff), BF16), jax.ShapeDtypeStruct((m, d_ff), F32),
                   jax.ShapeDtypeStruct((1, k, d_ff), BF16), jax.ShapeDtypeStruct((1, k, d_ff), BF16)),
        grid=(n_tiles,),
        in_specs=[pl.BlockSpec((m, k), lambda j: (0, 0)),
                  pl.BlockSpec((None, k, tn), lambda j: (layer, 0, j)),
                  pl.BlockSpec((None, k, tn), lambda j: (layer, 0, n_tiles + j)),
                  pl.BlockSpec((None, CONV_W, tn), lambda j: (layer, 0, j)),
                  pl.BlockSpec((None, 1, tn), lambda j: (layer, 0, j)),
                  pl.BlockSpec((m, tn), lambda j: (0, j)),
                  pl.BlockSpec((m, tn), lambda j: (0, j))],
        out_specs=(pl.BlockSpec((m, tn), lambda j: (0, j)),
                   pl.BlockSpec((m, tn), lambda j: (0, j)),
                   pl.BlockSpec((None, k, tn), lambda j: (0, 0, j)),
                   pl.BlockSpec((None, k, tn), lambda j: (0, 0, j))),
        compiler_params=_cparams("parallel"),
        name=name,
    )(x, w, w, cw, cb, h1, h2)


def _compress_rows(xa, w1a, w1b, w2, hpe):
    a = _dot(xa, w1a)
    b = _dot(xa, w1b)
    n = a.shape[0]
    h = a + pltpu.roll(b, n - 1, 0) + hpe
    return _dot(_gelu(h).astype(BF16), w2)


def _pos_embed_term(pe_ref, w1_ref, kind):
    return _dot(pe_ref[kind], w1_ref[kind])[0:1, :]


def _compress_p_kernel(k_ref, v_ref, w1_ref, w2_ref, pe_ref, kc_ref, vc_ref, *, n_chunks):
    half = STRIDE * HEAD_DIM
    for kind, (src, dst) in enumerate(((k_ref, kc_ref), (v_ref, vc_ref))):
        xa = jnp.concatenate(
            [src[pl.ds(s, n_chunks, stride=STRIDE), :] for s in range(STRIDE)], axis=1).astype(BF16)
        hpe = _pos_embed_term(pe_ref, w1_ref, kind)
        out = _compress_rows(xa, w1_ref[kind, 0:half, :], w1_ref[kind, half:2 * half, :],
                             w2_ref[kind], hpe)
        dst[...] = out.astype(dst.dtype)


def _compress_prompt(kv, w1, w2, pe, *, batch, seq, n_kv, name):
    n_chunks = seq // STRIDE
    out = jax.ShapeDtypeStruct((batch, n_kv, n_chunks, HEAD_DIM), BF16)
    ospec = pl.BlockSpec((None, None, n_chunks, HEAD_DIM), lambda b, h: (b, h, 0, 0))
    return pl.pallas_call(
        functools.partial(_compress_p_kernel, n_chunks=n_chunks),
        out_shape=(out, out),
        grid=(batch, n_kv),
        in_specs=[pl.BlockSpec((seq, HEAD_DIM), lambda b, h: (b, h)),
                  pl.BlockSpec((seq, HEAD_DIM), lambda b, h: (b, n_kv + h)),
                  pl.BlockSpec(w1.shape, lambda b, h: (0, 0, 0)),
                  pl.BlockSpec(w2.shape, lambda b, h: (0, 0, 0)),
                  pl.BlockSpec(pe.shape, lambda b, h: (0, 0, 0))],
        out_specs=(ospec, ospec),
        compiler_params=_cparams("parallel", "parallel"),
        name=name,
    )(kv, kv, w1, w2, pe)


def _block_scores(imp, pos, n_blocks, block_axis):
    j = lax.broadcasted_iota(jnp.int32, imp.shape, block_axis)
    cur = pos // L_SLC
    forced = (j == 0) | (j == cur) | (j == cur - 1)
    valid = j * L_SLC <= pos
    score = jnp.where(valid, jnp.where(forced, SEL_FORCE, imp), -SEL_FORCE)
    return jnp.where(j < n_blocks, score, NEG_BIG)


def _overlap_matrix(nc, nb, rows, cols):
    i = np.arange(nc)[:, None]
    j = np.arange(nb)[None, :]
    lo = np.maximum(i * STRIDE, j * L_SLC)
    hi = np.minimum(i * STRIDE + L_CMP, (j + 1) * L_SLC)
    ov = np.zeros((rows, cols), np.float32)
    ov[:nc, :nb] = np.maximum(hi - lo, 0) / STRIDE
    return ov


def _block_to_key_matrix(n_groups, lanes, keys_per_group):
    ex = np.zeros((n_groups, lanes, keys_per_group), np.float32)
    for c in range(n_groups):
        k = np.arange(keys_per_group)
        ex[c, (c * keys_per_group + k) // L_SLC, k] = 1.0
    return ex


def _attn_p_kernel(q_ref, kc_ref, vc_ref, ks_ref, vs_ref, kw_ref, vw_ref, gt_ref, ovt_ref, ex_ref,
                   o_ref, m_scr, acc_scr, *, tq, tk, gqa, hpb, n_blocks, n_cmp, win_keys):
    qi = pl.program_id(2)
    t0 = qi * tq
    q = q_ref[...] * LOGIT_SCALE
    qs = [[q[:, (hh * gqa + g) * HEAD_DIM:(hh * gqa + g + 1) * HEAD_DIM].astype(BF16)
           for g in range(gqa)] for hh in range(hpb)]
    pos_t = t0 + lax.broadcasted_iota(jnp.int32, (tq, 1), 0)
    pos_row = t0 + lax.broadcasted_iota(jnp.int32, (1, tq), 1)
    ones = jnp.ones((max(tk, win_keys), HEAD_DIM), BF16)
    lanes = ex_ref.shape[1]

    n = lax.broadcasted_iota(jnp.int32, (tq, kc_ref.shape[1]), 1)
    last_end = jnp.minimum(pos_t, (n_cmp - 1) * STRIDE + (L_CMP - 1))
    bias_c = jnp.where(n * STRIDE + (L_CMP - 1) <= last_end, 0.0, NEG_BIG)
    o_c, sel = [], []
    for hh in range(hpb):
        kc = kc_ref[hh]
        vc = vc_ref[hh]
        p_grp = None
        o_c.append([])
        for g in range(gqa):
            sm = _dot_nt(qs[hh][g], kc) + bias_c
            m = jnp.max(sm, axis=-1, keepdims=True)
            e = jnp.exp2(sm - m)
            norm = jnp.where(m > 0.5 * NEG_BIG,
                             1.0 / jnp.maximum(jnp.sum(e, axis=-1, keepdims=True), 1e-30), 0.0)
            p = e * norm
            o_c[hh].append(_dot(p.astype(BF16), vc))
            p_grp = p if p_grp is None else p_grp + p

        hi, lo = _split_hi_lo(p_grp)
        imp_t = _dot_nt(ovt_ref[...], hi) + _dot_nt(ovt_ref[...], lo)
        score = _block_scores(imp_t, pos_row, n_blocks, 0)
        blk = lax.broadcasted_iota(jnp.int32, score.shape, 0)
        rank = jnp.zeros(score.shape, F32)
        for i in range(n_blocks):
            ci = score[i:i + 1, :]
            rank = rank + jnp.where((ci > score) | ((ci == score) & (blk > i)), 1.0, 0.0)
        sel_t = jnp.where(rank < float(min(N_SEL, n_blocks)), 1.0, 0.0)
        sel_t = jnp.concatenate([sel_t, jnp.zeros((lanes - sel_t.shape[0], tq), F32)], axis=0)
        sel.append(sel_t.T.astype(BF16))

    m_scr[...] = jnp.full(m_scr.shape, NEG_BIG, F32)
    acc_scr[...] = jnp.zeros(acc_scr.shape, F32)

    def key_tile(c, carry):
        k0 = pl.multiple_of(c * tk, tk)
        kpos = k0 + lax.broadcasted_iota(jnp.int32, (tq, tk), 1)
        causal = jnp.where(kpos <= pos_t, 0.5, 2.0)
        for hh in range(hpb):
            cols = slice(hh * HEAD_DIM, (hh + 1) * HEAD_DIM)
            kt = ks_ref[pl.ds(k0, tk), cols].astype(BF16)
            vt = jnp.concatenate([vs_ref[pl.ds(k0, tk), cols].astype(BF16), ones[0:tk]], axis=1)
            bias = jnp.where(_dot(sel[hh], ex_ref[c]) > causal, 0.0, NEG_BIG)
            for g in range(gqa):
                rows = slice((hh * gqa + g) * tq, (hh * gqa + g + 1) * tq)
                sm = _dot_nt(qs[hh][g], kt) + bias
                m_old = m_scr[rows]
                m_new = jnp.maximum(m_old, jnp.max(sm, axis=-1, keepdims=True))
                e = jnp.exp2(sm - m_new)
                acc_scr[rows] = jnp.exp2(m_old - m_new) * acc_scr[rows] + _dot(e.astype(BF16), vt)
                m_scr[rows] = m_new
        return carry

    lax.fori_loop(0, (t0 + tq - 1) // tk + 1, key_tile, 0)

    start = pl.multiple_of(jnp.maximum(t0 + tq - win_keys, 0), 128)
    d = pos_t - (start + lax.broadcasted_iota(jnp.int32, (tq, win_keys), 1))
    bias_w = jnp.where(d >= 0, jnp.where(d < WINDOW, 0.0, NEG_BIG), NEG_BIG)
    for hh in range(hpb):
        cols = slice(hh * HEAD_DIM, (hh + 1) * HEAD_DIM)
        kw = kw_ref[pl.ds(start, win_keys), cols].astype(BF16)
        vw = jnp.concatenate([vw_ref[pl.ds(start, win_keys), cols].astype(BF16), ones[0:win_keys]], axis=1)
        gt = gt_ref[hh]
        for g in range(gqa):
            sm = _dot_nt(qs[hh][g], kw) + bias_w
            e = jnp.exp2(sm - jnp.max(sm, axis=-1, keepdims=True))
            ow = _dot(e.astype(BF16), vw)
            o_w = ow[:, 0:HEAD_DIM] / ow[:, HEAD_DIM:HEAD_DIM + 1]
            acc = acc_scr[(hh * gqa + g) * tq:(hh * gqa + g + 1) * tq]
            o_s = acc[:, 0:HEAD_DIM] / jnp.maximum(acc[:, HEAD_DIM:HEAD_DIM + 1], 1e-30)
            out = (gt[:, g:g + 1] * o_c[hh][g] + gt[:, gqa + g:gqa + g + 1] * o_s
                   + gt[:, 2 * gqa + g:2 * gqa + g + 1] * o_w)
            o_ref[:, (hh * gqa + g) * HEAD_DIM:(hh * gqa + g + 1) * HEAD_DIM] = out.astype(o_ref.dtype)


def _attn_prompt(q2d, kv, kcmp, vcmp, gates_h, *, batch, seq, n_heads, n_kv, tq=256, tk=1024, hpb=2,
                 name):
    gqa = n_heads // n_kv
    tq = _tile(seq, tq, 8)
    tk = _tile(seq, tk)
    hpb = _tile(n_kv, hpb, 1)
    n_grp = n_kv // hpb
    nq = seq // tq
    n_chunks = seq // STRIDE
    n_cmp = n_chunks - 1
    n_blocks = -(-seq // L_SLC)
    lanes = -(-n_blocks // 128) * 128
    block_rows = -(-n_blocks // 8) * 8
    ovt = jnp.asarray(_overlap_matrix(n_cmp, n_blocks, n_chunks, block_rows).T.copy(), BF16)
    ex = jnp.asarray(_block_to_key_matrix(seq // tk, lanes, tk), BF16)
    win_keys = min(WINDOW + tq, seq)
    rows = hpb * gqa * tq
    kern = functools.partial(_attn_p_kernel, tq=tq, tk=tk, gqa=gqa, hpb=hpb, n_blocks=n_blocks,
                             n_cmp=n_cmp, win_keys=win_keys)

    def kv_spec(kind):
        return pl.BlockSpec((seq, hpb * HEAD_DIM), lambda b, h, i: (b, kind * n_grp + h))

    cmp_spec = pl.BlockSpec((None, hpb, n_chunks, HEAD_DIM), lambda b, h, i: (b, h, 0, 0))
    return pl.pallas_call(
        kern,
        out_shape=jax.ShapeDtypeStruct((batch * seq, n_heads * HEAD_DIM), BF16),
        grid=(batch, n_grp, nq),
        in_specs=[pl.BlockSpec((tq, hpb * gqa * HEAD_DIM), lambda b, h, i: (b * nq + i, h)),
                  cmp_spec, cmp_spec,
                  kv_spec(2), kv_spec(3), kv_spec(4), kv_spec(5),
                  pl.BlockSpec((hpb, tq, 3 * gqa), lambda b, h, i: (h, b * nq + i, 0)),
                  pl.BlockSpec(ovt.shape, lambda b, h, i: (0, 0)),
                  pl.BlockSpec(ex.shape, lambda b, h, i: (0, 0, 0))],
        out_specs=pl.BlockSpec((tq, hpb * gqa * HEAD_DIM), lambda b, h, i: (b * nq + i, h)),
        scratch_shapes=[pltpu.VMEM((rows, 1), F32),
                        pltpu.VMEM((rows, 2 * HEAD_DIM), F32)],
        compiler_params=_cparams("parallel", "parallel", "arbitrary"),
        name=name,
    )(q2d, kcmp, vcmp, kv, kv, kv, kv, gates_h, ovt, ex)


def _head_slabs(rows_ref, first, n_heads):
    return jnp.concatenate([rows_ref[:, first + h, :] for h in range(n_heads)], axis=1)


def _page_copies(cache_ref, buf_ref, sem_ref, pt_ref, b, step, slot, *, layer, row0, n_rows,
                 pages_per_step):
    out = []
    for k in range(pages_per_step):
        page = pt_ref[b, step * pages_per_step + k]
        for r in range(n_rows):
            out.append(pltpu.make_async_copy(cache_ref.at[layer, page, :, row0 + r, :],
                                             buf_ref.at[slot, k, r], sem_ref.at[slot]))
    return out


def _page_stream(cache_ref, buf_ref, sem_ref, pt_ref, **kw):
    b, p = pl.program_id(0), pl.program_id(1)
    n_b, n_p = pl.num_programs(0), pl.num_programs(1)
    g = b * n_p + p
    slot = lax.rem(g, 2)

    def start_all(copies):
        for n, c in enumerate(copies):
            c.start(priority=n % 2)

    @pl.when(g == 0)
    def _():
        start_all(_page_copies(cache_ref, buf_ref, sem_ref, pt_ref, 0, 0, 0, **kw))

    @pl.when(g + 1 < n_b * n_p)
    def _():
        wrap = p + 1 == n_p
        start_all(_page_copies(cache_ref, buf_ref, sem_ref, pt_ref, jnp.where(wrap, b + 1, b),
                               jnp.where(wrap, 0, p + 1), 1 - slot, **kw))

    for c in _page_copies(cache_ref, buf_ref, sem_ref, pt_ref, b, p, slot, **kw):
        c.wait()
    return slot


def _cmp_s_kernel(pt_ref, cache_ref, q_ref, w1_ref, w2_ref, pe_ref, ovt_ref, oc_ref, sel_ref,
                  x_scr, ab_scr, buf_ref, sem_ref, *, layer, n_kv, n_steps, pages_per_step,
                  steps_per_group, n_cmp, n_blocks, dec_seq, gqa, past):
    p = pl.program_id(1)
    n_kh = 2 * n_kv
    page = buf_ref.shape[3]
    slot = _page_stream(cache_ref, buf_ref, sem_ref, pt_ref, layer=layer, row0=0, n_rows=n_kh,
                        pages_per_step=pages_per_step)
    cpp = page // STRIDE
    step_chunks = cpp * pages_per_step
    group_chunks = step_chunks * steps_per_group
    c0 = pl.multiple_of(lax.rem(p, steps_per_group) * step_chunks, 8)
    for k in range(pages_per_step):
        for s in range(STRIDE):
            for kh in range(n_kh):
                x_scr[kh, pl.ds(c0 + k * cpp, cpp), s * HEAD_DIM:(s + 1) * HEAD_DIM] = (
                    buf_ref[slot, k, kh, pl.ds(s, cpp, stride=STRIDE), :])

    @pl.when(lax.rem(p, steps_per_group) == steps_per_group - 1)
    def _():
        g0 = pl.multiple_of((p // steps_per_group) * group_chunks, 8)
        for kh in range(n_kh):
            kind = kh // n_kv
            ab_scr[kh, pl.ds(g0, group_chunks), :] = _dot(x_scr[kh].astype(BF16), w1_ref[kind])

    @pl.when(p == n_steps - 1)
    def _():
        n_chunks = ab_scr.shape[1]
        comp = []
        for kh in range(n_kh):
            kind = kh // n_kv
            hpe = _dot(pe_ref[kind], w1_ref[kind])
            hpe = hpe[0:1, 0:HEAD_DIM] + hpe[8:9, HEAD_DIM:2 * HEAD_DIM]
            ab = ab_scr[kh]
            h = ab[:, 0:HEAD_DIM] + pltpu.roll(ab[:, HEAD_DIM:2 * HEAD_DIM], n_chunks - 1, 0) + hpe
            comp.append(_dot(_gelu(h).astype(BF16), w2_ref[kind]).astype(BF16))
        qrows = lax.broadcasted_iota(jnp.int32, (1, HEAD_DIM), 1)
        pos = past + lax.rem(qrows, dec_seq)
        for h in range(n_kv):
            kc, vc = comp[h], comp[n_kv + h]
            st = _dot_nt(kc, (q_ref[h] * LOGIT_SCALE).astype(BF16))
            n = lax.broadcasted_iota(jnp.int32, st.shape, 0)
            pt = _masked_softmax2(st, (n * STRIDE + (L_CMP - 1) <= pos) & (n < n_cmp), axis=0)
            oc_ref[h] = _dot(pt.T.astype(BF16), vc)
            pg = pt
            for g in range(1, gqa):
                pg = pg + pltpu.roll(pt, HEAD_DIM - g * dec_seq, 1)
            hi, lo = _split_hi_lo(pg)
            imp_t = _dot(ovt_ref[...], hi) + _dot(ovt_ref[...], lo)
            imp = imp_t.T
            tpos = past + lax.broadcasted_iota(jnp.int32, (imp.shape[0], 1), 0)
            score = _block_scores(imp, tpos, n_blocks, 1)
            score_t = score.T
            nb_pad = score.shape[1]
            ii = lax.broadcasted_iota(jnp.int32, (nb_pad, nb_pad), 0)
            jj = lax.broadcasted_iota(jnp.int32, (nb_pad, nb_pad), 1)
            for t in range(dec_seq):
                col = score_t[:, t:t + 1]
                rowv = score[t:t + 1, :]
                beats = (col > rowv) | ((col == rowv) & (ii < jj))
                rank = jnp.sum(jnp.where(beats, 1.0, 0.0), axis=0, keepdims=True)
                sel_ref[h, t:t + 1, :] = jnp.where(rank < float(min(N_SEL, n_blocks)), 1.0, 0.0)
            sel_ref[h, dec_seq:, :] = jnp.zeros((sel_ref.shape[1] - dec_seq, nb_pad), F32)


def _cmp_sample(page_table, cache_rows, q_pad, w1cat, w2, pe, *, layer, n_kv, gqa, dec_seq, name):
    batch, n_pages = page_table.shape
    page = cache_rows.shape[2]
    past = n_pages * page
    cpp = page // STRIDE
    n_chunks = past // STRIDE
    assert dec_seq < STRIDE and n_chunks % 8 == 0
    n_cmp = (past + dec_seq) // STRIDE - 1
    n_blocks = -(-(past + dec_seq) // L_SLC)
    nb_pad = -(-n_blocks // 128) * 128
    pps = _tile(n_pages, 4, 1)
    n_steps = n_pages // pps
    spg = _tile(n_steps, max(1, 128 // (cpp * pps)), 1)
    n_kh = 2 * n_kv
    ovt = jnp.asarray(_overlap_matrix(n_cmp, n_blocks, n_chunks, nb_pad).T.copy(), BF16)
    kern = functools.partial(_cmp_s_kernel, layer=layer, n_kv=n_kv, n_steps=n_steps, pages_per_step=pps,
                             steps_per_group=spg, n_cmp=n_cmp, n_blocks=n_blocks, dec_seq=dec_seq,
                             gqa=gqa, past=past)
    grid_spec = pltpu.PrefetchScalarGridSpec(
        num_scalar_prefetch=1,
        grid=(batch, n_steps),
        in_specs=[
            pl.BlockSpec(memory_space=pl.ANY),
            pl.BlockSpec((None, n_kv, 128, HEAD_DIM), lambda b, p, pt: (b, 0, 0, 0)),
            pl.BlockSpec(w1cat.shape, lambda b, p, pt: (0, 0, 0)),
            pl.BlockSpec(w2.shape, lambda b, p, pt: (0, 0, 0)),
            pl.BlockSpec(pe.shape, lambda b, p, pt: (0, 0, 0)),
            pl.BlockSpec(ovt.shape, lambda b, p, pt: (0, 0))],
        out_specs=(pl.BlockSpec((None, n_kv, 128, HEAD_DIM), lambda b, p, pt: (b, 0, 0, 0)),
                   pl.BlockSpec((None, n_kv, 8, nb_pad), lambda b, p, pt: (b, 0, 0, 0))),
        scratch_shapes=[pltpu.VMEM((n_kh, spg * pps * cpp, STRIDE * HEAD_DIM), F32),
                        pltpu.VMEM((n_kh, n_chunks, 2 * HEAD_DIM), F32),
                        pltpu.VMEM((2, pps, n_kh, page, HEAD_DIM), F32),
                        pltpu.SemaphoreType.DMA((2,))],
    )
    return pl.pallas_call(
        kern,
        out_shape=(jax.ShapeDtypeStruct((batch, n_kv, 128, HEAD_DIM), F32),
                   jax.ShapeDtypeStruct((batch, n_kv, 8, nb_pad), F32)),
        grid_spec=grid_spec,
        compiler_params=_cparams("arbitrary", "arbitrary"),
        name=name,
    )(page_table, cache_rows, q_pad, w1cat, w2, pe, ovt)


def _slc_s_kernel(pt_ref, cache_ref, q_ref, sel_ref, ex_ref, kn_ref, vn_ref, win_ref, kwn_ref, vwn_ref,
                  oc_ref, gt_ref, o_ref, qbd_scr, m_scr, l_scr, acc_scr, buf_ref, sem_ref,
                  *, layer, n_kv, n_steps, pages_per_step, rows_per_head, dec_seq, past, w_buf):
    p = pl.program_id(1)
    slot = _page_stream(cache_ref, buf_ref, sem_ref, pt_ref, layer=layer, row0=2 * n_kv,
                        n_rows=2 * n_kv, pages_per_step=pages_per_step)
    rows = n_kv * rows_per_head

    @pl.when(p == 0)
    def _():
        qbd_scr[...] = jnp.zeros(qbd_scr.shape, qbd_scr.dtype)
        for h in range(n_kv):
            qbd_scr[h * rows_per_head:(h + 1) * rows_per_head, h * HEAD_DIM:(h + 1) * HEAD_DIM] = (
                q_ref[h, 0:rows_per_head, :] * LOGIT_SCALE)
        m_scr[...] = jnp.full(m_scr.shape, NEG_BIG, F32)
        l_scr[...] = jnp.zeros(l_scr.shape, F32)
        acc_scr[...] = jnp.zeros(acc_scr.shape, F32)

    qbd = qbd_scr[...].astype(BF16)

    def online_update(s, mask, v):
        sm = jnp.where(mask, s, NEG_BIG)
        m_old = m_scr[...]
        m_new = jnp.maximum(m_old, jnp.max(sm, axis=-1, keepdims=True))
        alpha = jnp.exp2(m_old - m_new)
        e = jnp.where(mask, jnp.exp2(sm - m_new), 0.0)
        l_scr[...] = alpha * l_scr[...] + jnp.sum(e, axis=-1, keepdims=True)
        acc_scr[...] = alpha * acc_scr[...] + _dot(e.astype(BF16), v)
        m_scr[...] = m_new

    ks, vs = [], []
    for k in range(pages_per_step):
        ks.append(jnp.concatenate([buf_ref[slot, k, h] for h in range(n_kv)], axis=1))
        vs.append(jnp.concatenate([buf_ref[slot, k, n_kv + h] for h in range(n_kv)], axis=1))
    k_all = jnp.concatenate(ks, axis=0).astype(BF16)
    v_all = jnp.concatenate(vs, axis=0).astype(BF16)
    s = _dot_nt(qbd, k_all)
    sel_keys = _dot(sel_ref[...].astype(BF16), ex_ref[...])
    online_update(s, sel_keys > 0.5, v_all)

    @pl.when(p == n_steps - 1)
    def _():
        r = lax.broadcasted_iota(jnp.int32, (rows, 1), 0)
        t = lax.rem(r, dec_seq)
        s = _dot_nt(qbd, kn_ref[...].astype(BF16))
        j = lax.broadcasted_iota(jnp.int32, s.shape, 1)
        online_update(s, (j <= t) & (j < dec_seq), vn_ref[...].astype(BF16))
        o_s = acc_scr[...] / jnp.maximum(l_scr[...], 1e-30)

        sb = _dot_nt(qbd, _head_slabs(win_ref, 0, n_kv).astype(BF16))
        sn = _dot_nt(qbd, kwn_ref[...].astype(BF16))
        ib = lax.broadcasted_iota(jnp.int32, sb.shape, 1)
        kpos = past - w_buf + ib
        d = (past + t) - kpos
        mb = (d >= 0) & (d < WINDOW) & (kpos >= 0)
        jn = lax.broadcasted_iota(jnp.int32, sn.shape, 1)
        mn = (jn <= t) & (jn < dec_seq) & (t - jn < WINDOW)
        smb = jnp.where(mb, sb, NEG_BIG)
        smn = jnp.where(mn, sn, NEG_BIG)
        mx = jnp.maximum(jnp.max(smb, axis=-1, keepdims=True), jnp.max(smn, axis=-1, keepdims=True))
        eb = jnp.where(mb, jnp.exp2(smb - mx), 0.0)
        en = jnp.where(mn, jnp.exp2(smn - mx), 0.0)
        den = jnp.maximum(jnp.sum(eb, axis=-1, keepdims=True) + jnp.sum(en, axis=-1, keepdims=True),
                          1e-30)
        o_w = (_dot((eb / den).astype(BF16), _head_slabs(win_ref, n_kv, n_kv).astype(BF16))
               + _dot((en / den).astype(BF16), vwn_ref[...].astype(BF16)))

        for h in range(n_kv):
            r0 = h * rows_per_head
            c0 = h * HEAD_DIM
            gt = gt_ref[h]
            o_ref[h] = (gt[:, 0:1] * oc_ref[h, 0:rows_per_head, :]
                        + gt[:, 1:2] * o_s[r0:r0 + rows_per_head, c0:c0 + HEAD_DIM]
                        + gt[:, 2:3] * o_w[r0:r0 + rows_per_head, c0:c0 + HEAD_DIM])


def _slc_sample(page_table, cache_rows, q_pad, sel_steps, k_new, v_new, win_rows, kw_new, vw_new, o_c,
                gates, *, layer, n_kv, gqa, dec_seq, pages_per_step, name):
    batch, n_pages = page_table.shape
    page = cache_rows.shape[2]
    past = n_pages * page
    width = n_kv * HEAD_DIM
    n_kh = 2 * n_kv
    rph = gqa * dec_seq
    rows = n_kv * rph
    w_buf = win_rows.shape[2]
    n_new = k_new.shape[1]
    pps = pages_per_step
    n_steps = n_pages // pps
    ex = jnp.asarray(_block_to_key_matrix(1, sel_steps.shape[3], pps * page)[0], BF16)
    kern = functools.partial(_slc_s_kernel, layer=layer, n_kv=n_kv, n_steps=n_steps, pages_per_step=pps,
                             rows_per_head=rph, dec_seq=dec_seq, past=past, w_buf=w_buf)

    def new_spec():
        return pl.BlockSpec((None, n_new, width), lambda b, p, pt: (b, 0, 0))

    grid_spec = pltpu.PrefetchScalarGridSpec(
        num_scalar_prefetch=1,
        grid=(batch, n_steps),
        in_specs=[
            pl.BlockSpec(memory_space=pl.ANY),
            pl.BlockSpec((None, n_kv, 128, HEAD_DIM), lambda b, p, pt: (b, 0, 0, 0)),
            pl.BlockSpec((None, None, rows, sel_steps.shape[3]), lambda b, p, pt: (b, p, 0, 0)),
            pl.BlockSpec(ex.shape, lambda b, p, pt: (0, 0)),
            new_spec(), new_spec(),
            pl.BlockSpec((None, None, w_buf, n_kh, HEAD_DIM), lambda b, p, pt: (layer, b, 0, 0, 0)),
            new_spec(), new_spec(),
            pl.BlockSpec((None, n_kv, 128, HEAD_DIM), lambda b, p, pt: (b, 0, 0, 0)),
            pl.BlockSpec((None, n_kv, rph, 8), lambda b, p, pt: (b, 0, 0, 0))],
        out_specs=pl.BlockSpec((None, n_kv, rph, HEAD_DIM), lambda b, p, pt: (b, 0, 0, 0)),
        scratch_shapes=[pltpu.VMEM((rows, width), F32),
                        pltpu.VMEM((rows, 1), F32),
                        pltpu.VMEM((rows, 1), F32),
                        pltpu.VMEM((rows, width), F32),
                        pltpu.VMEM((2, pps, n_kh, page, HEAD_DIM), F32),
                        pltpu.SemaphoreType.DMA((2,))],
    )
    return pl.pallas_call(
        kern,
        out_shape=jax.ShapeDtypeStruct((batch, n_kv, rph, HEAD_DIM), F32),
        grid_spec=grid_spec,
        compiler_params=_cparams("arbitrary", "arbitrary"),
        name=name,
    )(page_table, cache_rows, q_pad, sel_steps, ex, k_new, v_new, win_rows, kw_new, vw_new, o_c,
      gates)


def _rope_tables(pos):
    inv = ROPE_THETA ** (-jnp.arange(ROT_HALF, dtype=F32) * 2.0 / ROT_DIM)
    ang = pos.astype(F32)[:, None] * inv[None, :]
    cos, sin = jnp.cos(ang), jnp.sin(ang)
    rest = HEAD_DIM - ROT_DIM
    cos_t = jnp.concatenate([cos, cos, jnp.ones((pos.shape[0], rest), F32)], axis=1)
    sin_t = jnp.concatenate([-sin, sin, jnp.zeros((pos.shape[0], rest), F32)], axis=1)
    return cos_t, sin_t


def _prep_weights(w_in, ffn_conv_b, n_gate):
    w_gate = jnp.pad(w_in[:, :, w_in.shape[2] - n_gate:], ((0, 0), (0, 0), (0, 128 - n_gate)))
    return dict(w_gate=w_gate.astype(BF16),
                f_cb=ffn_conv_b.reshape(ffn_conv_b.shape[0], 1, ffn_conv_b.shape[1]))


def _prep_compress(l, cmp_pe, cmp_w1, cmp_w2):
    half = STRIDE * HEAD_DIM
    w1 = cmp_w1[l].reshape(2, 2 * half, HEAD_DIM)
    wl = {}
    wl["cmp_w1"] = w1.astype(BF16)
    wl["cmp_w1cat"] = jnp.concatenate([w1[:, :half], w1[:, half:]], axis=2).astype(BF16)
    wl["cmp_w2"] = cmp_w2[l].astype(BF16)
    pe = cmp_pe[l].reshape(2, 1, 2 * half)
    wl["cmp_pe"] = jnp.broadcast_to(pe, (2, 16, 2 * half)).astype(BF16)
    pe2 = cmp_pe[l].reshape(2, 2, 1, half)
    wl["cmp_pecat"] = jnp.concatenate([jnp.broadcast_to(pe2[:, 0], (2, 8, half)),
                                       jnp.broadcast_to(pe2[:, 1], (2, 8, half))], axis=1).astype(BF16)
    return wl


def _project(x_bf, ws, wb, l, cos_t, sin_t, dims, tag):
    d_a, d_b, d_kv, n_heads, d_ff = dims
    n_kv = d_kv // HEAD_DIM
    emit = wb is None
    if emit:
        src = dict(uv=(ws["w_in_t"], l, 0), q=(ws["w_in_t"], l, 2 * d_a), kv=(ws["w_in_t"], l, 2 * d_a + d_b))
    else:
        src = dict(uv=(wb["uv"], 0, 0), q=(wb["q"], 0, 0), kv=(wb["kv"], 0, 0))
    new = {}
    w, lay, c0 = src["uv"]
    out = _matmul([x_bf], w, layer=lay, col_start=c0, n=2 * d_a, epilogue="gelu", tm=1024,
                  tn=512 if emit else 1024, emit=emit, w_transposed=emit, name=f"proj_uv_{tag}")
    uv = out[0] if emit else out
    if emit:
        new["uv"] = out[1]
    w, lay, c0 = src["q"]
    out = _rope_matmul(x_bf, w, cos_t, sin_t, layer=lay, col_start=c0, n=d_b,
                       tn=d_kv if emit else _tile(d_b, 2 * d_kv), alternate=False, emit=emit,
                       name=f"proj_q_{tag}")
    q2d = out[0]
    if emit:
        new["q"] = out[-1]
    w, lay, c0 = src["kv"]
    out = _rope_matmul(x_bf, w, cos_t, sin_t, layer=lay, col_start=c0, n=6 * d_kv, tn=d_kv, alternate=True,
                       n_kv=n_kv, emit=emit, name=f"proj_kv_{tag}")
    kv, nsa4, win4 = out[0], out[1], out[2]
    if emit:
        new["kv"] = out[-1]
    gates = _matmul([x_bf], ws["w_gate"], layer=l, epilogue="sigmoid", tm=1024, tn=128,
                    name=f"proj_gate_{tag}")
    return uv, q2d, kv, nsa4, win4, gates, new


def _mix_and_norm(x, a_out, b_out, ws, wb, l, ln_g, ln_b, alpha, tag):
    new = {}
    if wb is None:
        z, new["o_a"], new["o_b"] = _matmul([a_out, b_out], ws["w_o"], layer=l, tm=1024, tn=512,
                                            emit=True, residual=x, res_scale=alpha, name=f"w_o_{tag}")
    else:
        z = _matmul([a_out, b_out], [wb["o_a"], wb["o_b"]], layer=0, tm=1024, tn=1024,
                    residual=x, res_scale=alpha, name=f"w_o_{tag}")
    y, yb = _layer_norm(z, ln_g[l, 0], ln_b[l, 0], name=f"ln1_{tag}")
    return y, yb, new


def _layer_prompt(x, x_bf, ws, wb, wl, l, p, cos_t, sin_t, dims, batch, seq, n_kv, alpha):
    d_a, d_b, d_kv, n_heads, d_ff = dims
    gqa = n_heads // n_kv
    m = batch * seq
    uv, q2d, kv, nsa4, win4, gates, _ = _project(x_bf, ws, wb, l, cos_t, sin_t, dims, "p")
    (a_out,) = _sgu(uv, p["sgu_w"][l], p["sgu_b"][l].T, p["sgu_g"][l], rows=CHUNK, with_vn=False,
                    name="sgu_p")
    kcmp, vcmp = _compress_prompt(kv, wl["cmp_w1"], wl["cmp_w2"], wl["cmp_pe"], batch=batch, seq=seq,
                                  n_kv=n_kv, name="compress_p")
    gates_h = gates[:, :3 * n_heads].reshape(m, 3, n_kv, gqa).transpose(2, 0, 1, 3).reshape(n_kv, m, 3 * gqa)
    b_out = _attn_prompt(q2d, kv, kcmp, vcmp, gates_h, batch=batch, seq=seq, n_heads=n_heads, n_kv=n_kv,
                         name="nsa_p")
    x1, x1b, _ = _mix_and_norm(x, a_out, b_out, ws, wb, l, p["ln_g"], p["ln_b"], alpha, "p")
    state0 = jnp.zeros((batch, CONV_W - 1, d_ff), F32)
    act, conv_new = _ffn_in_seq(x1b, wb["f_gate"], wb["f_up"], p["ffn_conv_w"], ws["f_cb"], state0,
                                layer=l, seq_len=seq, name="ffn_in_p")
    z = _matmul([act], wb["f_down"], layer=0, tm=512, tn=512, residual=x1, res_scale=alpha,
                name="ffn_down_p")
    x2, x2b = _layer_norm(z, p["ln_g"][l, 1], p["ln_b"][l, 1], name="ln2_p")
    new_nsa = nsa4.reshape(batch, seq, 4, n_kv, HEAD_DIM)
    new_win = win4.reshape(batch, seq, 2, n_kv, HEAD_DIM)[:, -min(WINDOW, seq):]
    return x2, x2b, new_nsa, new_win, conv_new


def _layer_sample(x, x_bf, ws, wl, l, p, cos_t, sin_t, dims, batch, dec_seq, n_kv, alpha, page_table,
                  cache_rows, win_rows, conv_state):
    d_a, d_b, d_kv, n_heads, d_ff = dims
    gqa = n_heads // n_kv
    m = batch * dec_seq
    rph = gqa * dec_seq
    uv, q2d, kv, nsa4, win4, gates, wb = _project(x_bf, ws, None, l, cos_t, sin_t, dims, "s")
    w_small = p["sgu_w"][l][:, :dec_seq, :dec_seq]
    eye = jnp.eye(batch, dtype=F32)
    w_bd = jnp.einsum("ab,gts->gatbs", eye, w_small).reshape(-1, m, m)
    bt_bd = jnp.tile(p["sgu_b"][l].T[:dec_seq], (batch, 1))
    a_out, v_rows = _sgu(uv, w_bd, bt_bd, p["sgu_g"][l], rows=m, with_vn=True, name="sgu_s")

    q = q2d.reshape(batch, dec_seq, n_kv, gqa, HEAD_DIM).transpose(0, 2, 3, 1, 4)
    q_pad = jnp.pad(q.reshape(batch, n_kv, rph, HEAD_DIM), ((0, 0), (0, 0), (0, 128 - rph), (0, 0)))
    o_c, sel = _cmp_sample(page_table, cache_rows, q_pad, wl["cmp_w1cat"], wl["cmp_w2"], wl["cmp_pecat"],
                           layer=l, n_kv=n_kv, gqa=gqa, dec_seq=dec_seq, name="cmp_s")
    n_pages = page_table.shape[1]
    bpp = cache_rows.shape[2] // L_SLC
    pps = _tile(n_pages, 4, 1)
    n_steps = n_pages // pps
    sel_steps = sel[:, :, :dec_seq, :n_pages * bpp].reshape(batch, n_kv, 1, dec_seq, n_steps, pps * bpp)
    sel_steps = jnp.broadcast_to(sel_steps, (batch, n_kv, gqa, dec_seq, n_steps, pps * bpp))
    sel_steps = sel_steps.transpose(0, 4, 1, 2, 3, 5).reshape(batch, n_steps, n_kv * rph, pps * bpp)
    sel_steps = jnp.pad(sel_steps, ((0, 0), (0, 0), (0, 0), (0, 128 - pps * bpp)))

    def new_rows(kind):
        rows = kv[:, kind * d_kv:(kind + 1) * d_kv].reshape(batch, dec_seq, d_kv)
        return jnp.pad(rows, ((0, 0), (0, 128 - dec_seq), (0, 0)))

    gates_s = gates[:, :3 * n_heads].reshape(batch, dec_seq, 3, n_kv, gqa).transpose(0, 3, 4, 1, 2)
    gates_s = jnp.pad(gates_s.reshape(batch, n_kv, rph, 3), ((0, 0), (0, 0), (0, 0), (0, 5)))
    b_rows = _slc_sample(page_table, cache_rows, q_pad, sel_steps, new_rows(2), new_rows(3), win_rows,
                         new_rows(4), new_rows(5), o_c, gates_s, layer=l, n_kv=n_kv, gqa=gqa,
                         dec_seq=dec_seq, pages_per_step=pps, name="slc_s")
    b_out = b_rows.reshape(batch, n_kv, gqa, dec_seq, HEAD_DIM).transpose(0, 3, 1, 2, 4)
    b_out = b_out.reshape(m, d_b).astype(BF16)

    x1, x1b, wb_o = _mix_and_norm(x, a_out, b_out, ws, None, l, p["ln_g"], p["ln_b"], alpha, "s")
    wb.update(wb_o)
    st = conv_state
    zero = jnp.zeros((batch, dec_seq - 1, d_ff), F32)
    h1 = jnp.concatenate([st[:, 1:2], zero], axis=1).reshape(m, d_ff)
    h2 = jnp.concatenate([st, zero[:, 1:]], axis=1).reshape(m, d_ff)
    act, gate, wb["f_gate"], wb["f_up"] = _ffn_in_short(
        x1b, ws["f_in"], p["ffn_conv_w"], ws["f_cb"], h1, h2, layer=l, seq_len=dec_seq, name="ffn_in_s")
    z, wb["f_down"] = _matmul([act], ws["f_down"], layer=l, tn=256, emit=True, residual=x1,
                              res_scale=alpha, name="ffn_down_s")
    x2, x2b = _layer_norm(z, p["ln_g"][l, 1], p["ln_b"][l, 1], name="ln2_s")
    new_nsa = nsa4.reshape(batch, dec_seq, 4, n_kv, HEAD_DIM)
    new_win = win4.reshape(batch, dec_seq, 2, n_kv, HEAD_DIM)
    conv_new = gate.reshape(batch, dec_seq, d_ff)[:, dec_seq - (CONV_W - 1):]
    return x2, x2b, new_nsa, new_win, v_rows.reshape(batch, dec_seq, d_a), conv_new, wb


def kernel(x_prompt, x_sample, cache_nsa_kv, cache_win_kv, state_ffn_conv, page_table, w_in, sgu_w,
           sgu_b, sgu_g, cmp_pe, cmp_w1, cmp_w2, w_o, ln_g, ln_b, ffn_w_in, ffn_conv_w, ffn_conv_b,
           ffn_w_down):
    bp, seq, d_model = x_prompt.shape
    bs, dec_seq, _ = x_sample.shape
    depth = w_in.shape[0]
    n_kv = cache_nsa_kv.shape[4]
    page = cache_nsa_kv.shape[2]
    past = page_table.shape[1] * page
    d_a = d_model // 2
    d_b = d_model - d_a
    n_heads = d_b // HEAD_DIM
    d_kv = n_kv * HEAD_DIM
    d_ff = ffn_conv_w.shape[-1]
    dims = (d_a, d_b, d_kv, n_heads, d_ff)
    alpha = (2 * depth) ** 0.25
    assert dec_seq >= CONV_W - 1 and seq % CHUNK == 0

    cos_p, sin_p = _rope_tables(jnp.tile(jnp.arange(seq, dtype=jnp.int32), bp))
    cos_s, sin_s = _rope_tables(jnp.tile(past + jnp.arange(dec_seq, dtype=jnp.int32), bs))
    cache_rows = cache_nsa_kv.reshape(depth, cache_nsa_kv.shape[1], page, 4 * n_kv, HEAD_DIM)
    win_rows = cache_win_kv.reshape(depth, bs, cache_win_kv.shape[2], 2 * n_kv, HEAD_DIM)
    ws = _prep_weights(w_in, ffn_conv_b, 3 * n_heads)
    ws.update(w_in_t=jnp.swapaxes(w_in, 1, 2), w_o=w_o, f_in=ffn_w_in, f_down=ffn_w_down)
    p = dict(sgu_w=sgu_w, sgu_b=sgu_b, sgu_g=sgu_g, ln_g=ln_g, ln_b=ln_b, ffn_conv_w=ffn_conv_w)

    xp = x_prompt.reshape(bp * seq, d_model)
    xs = x_sample.reshape(bs * dec_seq, d_model)
    xp_bf, xs_bf = xp.astype(BF16), xs.astype(BF16)
    outs = [[] for _ in range(7)]
    for l in range(depth):
        wl = _prep_compress(l, cmp_pe, cmp_w1, cmp_w2)
        xs, xs_bf, nsa_s, win_s, v_s, conv_s, wb = _layer_sample(
            xs, xs_bf, ws, wl, l, p, cos_s, sin_s, dims, bs, dec_seq, n_kv, alpha, page_table,
            cache_rows, win_rows, state_ffn_conv[l])
        xp, xp_bf, nsa_p, win_p, conv_p = _layer_prompt(
            xp, xp_bf, ws, wb, wl, l, p, cos_p, sin_p, dims, bp, seq, n_kv, alpha)
        for acc, val in zip(outs, (nsa_p, nsa_s, win_p, win_s, v_s, conv_p, conv_s)):
            acc.append(val)
    return (xp.reshape(bp, seq, d_model), xs.reshape(bs, dec_seq, d_model),
            *[jnp.stack(o) for o in outs])
```

```python
import functools
import math

import jax
import jax.numpy as jnp
import numpy as np
from jax import lax
from jax.experimental import pallas as pl
from jax.experimental.pallas import tpu as pltpu

HEAD_DIM = 128
CHUNK = 128
STRIDE = 16
L_CMP = 2 * STRIDE
L_SLC = 64
N_SEL = 16
WINDOW = 512
ROT_DIM = HEAD_DIM // 4
ROT_HALF = ROT_DIM // 2
ROPE_THETA = 500000.0
CONV_W = 3
LN_EPS = 1e-5
SCALE = HEAD_DIM ** -0.5
LOGIT_SCALE = SCALE * math.log2(math.e)
SEL_FORCE = 1e9
NEG_BIG = -3.0e38
VMEM_LIMIT = 56 * 1024 * 1024

F32 = jnp.float32
BF16 = jnp.bfloat16


def _cparams(*sem):
    return pltpu.CompilerParams(dimension_semantics=sem, vmem_limit_bytes=VMEM_LIMIT)


def _tile(n, pref, unit=128):
    if n <= pref:
        return n
    t = (pref // unit) * unit
    while t > unit and n % t:
        t -= unit
    assert n % t == 0, (n, pref, unit)
    return t


def _gelu(x):
    return jax.nn.gelu(x, approximate=True)


def _dot(a, b):
    return jnp.dot(a, b, preferred_element_type=F32)


def _dot_nt(a, b):
    return lax.dot_general(a, b, (((1,), (1,)), ((), ())), preferred_element_type=F32)


def _split_hi_lo(x):
    hi = x.astype(BF16)
    lo = (x - hi.astype(F32)).astype(BF16)
    return hi, lo


def _masked_softmax2(s, mask, axis=-1):
    sm = jnp.where(mask, s, NEG_BIG)
    m = jnp.max(sm, axis=axis, keepdims=True)
    p = jnp.where(mask, jnp.exp2(sm - m), 0.0)
    return p / jnp.maximum(jnp.sum(p, axis=axis, keepdims=True), 1e-30)


def _mm_kernel(*refs, n_lhs, epilogue, emit, res_scale, w_transposed):
    n_in = 2 * n_lhs + (res_scale is not None)
    ws = [refs[n_lhs + k][...] for k in range(n_lhs)]
    if w_transposed:
        ws = [w.T for w in ws]
    if emit:
        ws = [w.astype(BF16) for w in ws]
        for k in range(n_lhs):
            refs[n_in + 1 + k][...] = ws[k]
    acc = _dot(refs[0][...], ws[0])
    for k in range(1, n_lhs):
        acc = acc + _dot(refs[k][...], ws[k])
    if res_scale is not None:
        acc = res_scale * refs[n_in - 1][...] + acc
    o_ref = refs[n_in]
    if epilogue == "gelu":
        acc = _gelu(acc)
    elif epilogue == "sigmoid":
        acc = jax.nn.sigmoid(acc)
    o_ref[...] = acc.astype(o_ref.dtype)


def _matmul(xs, w, *, layer, col_start=0, n=None, epilogue="none", out_dtype=F32, tm=512, tn=512,
            emit=False, residual=None, res_scale=None, w_transposed=False, name):
    m, kdim = xs[0].shape
    n = (w[0] if isinstance(w, (list, tuple)) else w).shape[1 if w_transposed else 2] if n is None else n
    tm = _tile(m, tm, 8)
    tn = _tile(n, tn)
    assert col_start % tn == 0 and all(x.shape == (m, kdim) for x in xs)
    c0 = col_start // tn
    n_lhs = len(xs)
    in_specs = [pl.BlockSpec((tm, kdim), lambda i, j: (i, 0)) for _ in xs]
    separate = isinstance(w, (list, tuple))
    w_list = list(w) if separate else [w] * n_lhs
    if w_transposed:
        assert n_lhs == 1
        in_specs += [pl.BlockSpec((None, tn, kdim), lambda i, j: (layer, c0 + j, 0))]
    else:
        in_specs += [pl.BlockSpec((None, kdim, tn), lambda i, j, r=r: (layer, 0 if separate else r, c0 + j))
                     for r in range(n_lhs)]
    extra = []
    if residual is not None:
        in_specs.append(pl.BlockSpec((tm, tn), lambda i, j: (i, j)))
        extra = [residual]
    out_shape = [jax.ShapeDtypeStruct((m, n), out_dtype)]
    out_specs = [pl.BlockSpec((tm, tn), lambda i, j: (i, j))]
    if emit:
        assert m == tm
        out_shape += [jax.ShapeDtypeStruct((1, kdim, n), BF16)] * n_lhs
        out_specs += [pl.BlockSpec((None, kdim, tn), lambda i, j: (0, 0, j))] * n_lhs
    out = pl.pallas_call(
        functools.partial(_mm_kernel, n_lhs=n_lhs, epilogue=epilogue, emit=emit,
                          res_scale=res_scale if residual is not None else None,
                          w_transposed=w_transposed),
        out_shape=tuple(out_shape),
        grid=(m // tm, n // tn),
        in_specs=in_specs,
        out_specs=tuple(out_specs),
        compiler_params=_cparams("parallel", "arbitrary"),
        name=name,
    )(*xs, *w_list, *extra)
    return out if emit else out[0]


def _rope_mm_kernel(x_ref, w_ref, cos_ref, sin_ref, o_ref, *more_refs, heads_per_tile, alternate,
                    n_nsa_kinds, emit):
    j = pl.program_id(1)
    w = w_ref[...]
    cache_refs = more_refs
    if emit:
        w = w.T.astype(BF16)
        more_refs[-1][...] = w
        cache_refs = more_refs[:-1]
    acc = _dot(x_ref[...], w)
    cosv, sinv = cos_ref[...], sin_ref[...]
    if alternate:
        rot = lax.rem(j, 2) == 0
        cosv = jnp.where(rot, cosv, 1.0)
        sinv = jnp.where(rot, sinv, 0.0)
    lane = lax.broadcasted_iota(jnp.int32, cosv.shape, 1)
    heads = []
    for h in range(heads_per_tile):
        hs = acc[:, h * HEAD_DIM:(h + 1) * HEAD_DIM]
        partner = jnp.where(lane < ROT_HALF,
                            pltpu.roll(hs, HEAD_DIM - ROT_HALF, 1),
                            pltpu.roll(hs, ROT_HALF, 1))
        heads.append(hs * cosv + partner * sinv)
        o_ref[:, h * HEAD_DIM:(h + 1) * HEAD_DIM] = heads[h]
    if cache_refs:
        nsa_ref, win_ref = cache_refs

        @pl.when(j < n_nsa_kinds)
        def _():
            for h in range(heads_per_tile):
                nsa_ref[:, h, :] = heads[h]

        @pl.when(j >= n_nsa_kinds)
        def _():
            for h in range(heads_per_tile):
                win_ref[:, h, :] = heads[h]


def _rope_matmul(x, w, cos_t, sin_t, *, layer, col_start, n, tn, alternate, n_kv=None, tm=1024,
                 emit=False, name):
    m, k = x.shape
    tm = _tile(m, tm, 8)
    assert col_start % tn == 0 and n % tn == 0
    c0 = col_start // tn
    hpt = tn // HEAD_DIM
    out_shape = [jax.ShapeDtypeStruct((m, n), F32)]
    out_specs = [pl.BlockSpec((tm, tn), lambda i, j: (i, j))]
    n_nsa = 4
    if alternate:
        assert hpt == n_kv and n == 6 * tn
        out_shape += [jax.ShapeDtypeStruct((m, n_nsa, n_kv, HEAD_DIM), F32),
                      jax.ShapeDtypeStruct((m, 2, n_kv, HEAD_DIM), F32)]
        out_specs += [pl.BlockSpec((tm, None, n_kv, HEAD_DIM),
                                   lambda i, j: (i, jnp.minimum(j, n_nsa - 1), 0, 0)),
                      pl.BlockSpec((tm, None, n_kv, HEAD_DIM),
                                   lambda i, j: (i, jnp.maximum(j - n_nsa, 0), 0, 0))]
    if emit:
        assert m == tm
        out_shape += [jax.ShapeDtypeStruct((1, k, n), BF16)]
        out_specs += [pl.BlockSpec((None, k, tn), lambda i, j: (0, 0, j))]
    kern = functools.partial(_rope_mm_kernel, heads_per_tile=hpt, alternate=alternate,
                             n_nsa_kinds=n_nsa, emit=emit)
    return pl.pallas_call(
        kern,
        out_shape=tuple(out_shape),
        grid=(m // tm, n // tn),
        in_specs=[pl.BlockSpec((tm, k), lambda i, j: (i, 0)),
                  (pl.BlockSpec((None, tn, k), lambda i, j: (layer, c0 + j, 0)) if emit else
                   pl.BlockSpec((None, k, tn), lambda i, j: (layer, 0, c0 + j))),
                  pl.BlockSpec((tm, HEAD_DIM), lambda i, j: (i, 0)),
                  pl.BlockSpec((tm, HEAD_DIM), lambda i, j: (i, 0))],
        out_specs=tuple(out_specs),
        compiler_params=_cparams("arbitrary", "arbitrary"),
        name=name,
    )(x, w, cos_t, sin_t)


def _ln_kernel(z_ref, g_ref, b_ref, y_ref, yb_ref):
    z = z_ref[...]
    mu = jnp.mean(z, axis=-1, keepdims=True)
    zc = z - mu
    var = jnp.mean(zc * zc, axis=-1, keepdims=True)
    y = zc * lax.rsqrt(var + LN_EPS) * g_ref[...] + b_ref[...]
    y_ref[...] = y
    yb_ref[...] = y.astype(BF16)


def _layer_norm(z, g, b, *, name):
    m, d = z.shape
    tr = _tile(m, 256, 8)
    return pl.pallas_call(
        _ln_kernel,
        out_shape=(jax.ShapeDtypeStruct((m, d), F32), jax.ShapeDtypeStruct((m, d), BF16)),
        grid=(m // tr,),
        in_specs=[pl.BlockSpec((tr, d), lambda i: (i, 0)),
                  pl.BlockSpec((1, d), lambda i: (0, 0)),
                  pl.BlockSpec((1, d), lambda i: (0, 0))],
        out_specs=(pl.BlockSpec((tr, d), lambda i: (i, 0)),
                   pl.BlockSpec((tr, d), lambda i: (i, 0))),
        compiler_params=_cparams("parallel"),
        name=name,
    )(z, g.reshape(1, d), b.reshape(1, d))


def _sgu_kernel(uv_ref, w_ref, bt_ref, g_ref, a_ref, *vn_refs, d_a, n_groups):
    rows = w_ref.shape[1]
    r = lax.broadcasted_iota(jnp.int32, (rows, rows), 0)
    c = lax.broadcasted_iota(jnp.int32, (rows, rows), 1)
    causal = r >= c
    for g in range(n_groups):
        lo = g * HEAD_DIM
        v = uv_ref[:, d_a + lo:d_a + lo + HEAD_DIM]
        mu = jnp.mean(v, axis=-1, keepdims=True)
        vc = v - mu
        var = jnp.mean(vc * vc, axis=-1, keepdims=True)
        vn = vc * lax.rsqrt(var + LN_EPS) * g_ref[:, lo:lo + HEAD_DIM]
        if vn_refs:
            vn_refs[0][:, lo:lo + HEAD_DIM] = vn
        w = jnp.where(causal, w_ref[g], 0.0).astype(BF16)
        mixed = _dot(w, vn.astype(BF16)) + bt_ref[:, g:g + 1]
        a_ref[:, lo:lo + HEAD_DIM] = (uv_ref[:, lo:lo + HEAD_DIM] * mixed).astype(a_ref.dtype)


def _sgu(uv, w, bt, gain, *, rows, with_vn, name):
    m = uv.shape[0]
    d_a = uv.shape[1] // 2
    n_groups = d_a // HEAD_DIM
    n_out = 2 if with_vn else 1
    return pl.pallas_call(
        functools.partial(_sgu_kernel, d_a=d_a, n_groups=n_groups),
        out_shape=(jax.ShapeDtypeStruct((m, d_a), BF16), jax.ShapeDtypeStruct((m, d_a), F32))[:n_out],
        grid=(m // rows,),
        in_specs=[pl.BlockSpec((rows, 2 * d_a), lambda i: (i, 0)),
                  pl.BlockSpec((n_groups, rows, rows), lambda i: (0, 0, 0)),
                  pl.BlockSpec((rows, n_groups), lambda i: (0, 0)),
                  pl.BlockSpec((1, d_a), lambda i: (0, 0))],
        out_specs=(pl.BlockSpec((rows, d_a), lambda i: (i, 0)),
                   pl.BlockSpec((rows, d_a), lambda i: (i, 0)))[:n_out],
        compiler_params=_cparams("parallel"),
        name=name,
    )(uv, w, bt, gain.reshape(1, d_a))


def _conv_act(gate, g1, g2, up, cw_ref, cb_ref):
    c = cb_ref[...] + g2 * cw_ref[0:1, :] + g1 * cw_ref[1:2, :] + gate * cw_ref[2:3, :]
    return _gelu(c) * up


def _ffn_in_seq_kernel(x_ref, *refs, tiles_per_seq, n_sub):
    wg, wu, cw, cb, st = (refs[k * n_sub:(k + 1) * n_sub] for k in range(5))
    a_ref, cn_ref, carry_ref = refs[5 * n_sub:]
    i = pl.program_id(1)
    x = x_ref[...]
    tm = x.shape[0]
    tn = wg[0].shape[1]
    first = lax.rem(i, tiles_per_seq) == 0
    row = lax.broadcasted_iota(jnp.int32, (tm, tn), 0)
    for k in range(n_sub):
        cols = slice(k * tn, (k + 1) * tn)
        gate = _dot(x, wg[k][...])
        up = _dot(x, wu[k][...])
        prev2 = jnp.where(first, st[k][0:1, :], carry_ref[0:1, cols])
        prev1 = jnp.where(first, st[k][1:2, :], carry_ref[1:2, cols])
        g1 = jnp.where(row == 0, prev1, pltpu.roll(gate, 1, 0))
        g2 = jnp.where(row == 0, prev2, jnp.where(row == 1, prev1, pltpu.roll(gate, 2, 0)))
        a_ref[:, cols] = _conv_act(gate, g1, g2, up, cw[k], cb[k]).astype(a_ref.dtype)
        tail = gate[tm - 2:tm, :]
        carry_ref[0:2, cols] = tail
        cn_ref[:, cols] = tail


def _ffn_in_seq(x, wg, wu, cw, cb, state, *, layer, seq_len, tm=1024, tn=256, n_sub=2, name):
    m, k = x.shape
    d_ff = cw.shape[2]
    tm = _tile(seq_len, tm, 8)
    tn = _tile(d_ff, tn)
    n_tiles = d_ff // tn
    tps = seq_len // tm
    n_seq = m // seq_len

    def col(j, s):
        return jnp.minimum(j * n_sub + s, n_tiles - 1)

    subs = range(n_sub)
    in_specs = [pl.BlockSpec((tm, k), lambda j, i: (i, 0))]
    in_specs += [pl.BlockSpec((None, k, tn), lambda j, i, s=s: (0, 0, col(j, s))) for s in subs] * 2
    in_specs += [pl.BlockSpec((None, CONV_W, tn), lambda j, i, s=s: (layer, 0, col(j, s))) for s in subs]
    in_specs += [pl.BlockSpec((None, 1, tn), lambda j, i, s=s: (layer, 0, col(j, s))) for s in subs]
    in_specs += [pl.BlockSpec((None, CONV_W - 1, tn), lambda j, i, s=s: (i // tps, 0, col(j, s)))
                 for s in subs]
    return pl.pallas_call(
        functools.partial(_ffn_in_seq_kernel, tiles_per_seq=tps, n_sub=n_sub),
        out_shape=(jax.ShapeDtypeStruct((m, d_ff), BF16),
                   jax.ShapeDtypeStruct((n_seq, CONV_W - 1, d_ff), F32)),
        grid=(pl.cdiv(n_tiles, n_sub), m // tm),
        in_specs=in_specs,
        out_specs=(pl.BlockSpec((tm, n_sub * tn), lambda j, i: (i, j)),
                   pl.BlockSpec((None, CONV_W - 1, n_sub * tn), lambda j, i: (i // tps, 0, j))),
        scratch_shapes=[pltpu.VMEM((8, n_sub * tn), F32)],
        compiler_params=_cparams("arbitrary", "arbitrary"),
        name=name,
    )(x, *([wg] * n_sub), *([wu] * n_sub), *([cw] * n_sub), *([cb] * n_sub), *([state] * n_sub))


def _ffn_in_short_kernel(x_ref, wg_ref, wu_ref, cw_ref, cb_ref, h1_ref, h2_ref, a_ref, gate_ref,
                         wgb_ref, wub_ref, *, seq_len):
    x = x_ref[...]
    wg = wg_ref[...].astype(BF16)
    wu = wu_ref[...].astype(BF16)
    wgb_ref[...] = wg
    wub_ref[...] = wu
    gate = _dot(x, wg)
    up = _dot(x, wu)
    t = lax.rem(lax.broadcasted_iota(jnp.int32, gate.shape, 0), seq_len)
    g1 = jnp.where(t >= 1, pltpu.roll(gate, 1, 0), h1_ref[...])
    g2 = jnp.where(t >= 2, pltpu.roll(gate, 2, 0), h2_ref[...])
    a_ref[...] = _conv_act(gate, g1, g2, up, cw_ref, cb_ref).astype(a_ref.dtype)
    gate_ref[...] = gate


def _ffn_in_short(x, w, cw, cb, h1, h2, *, layer, seq_len, tn=256, name):
    m, k = x.shape
    d_ff = cw.shape[2]
    tn = _tile(d_ff, tn)
    n_tiles = d_ff // tn
    return pl.pallas_call(
        functools.partial(_ffn_in_short_kernel, seq_len=seq_len),
        out_shape=(jax.ShapeDtypeStruct((m, d_ff), BF16), jax.ShapeDtypeStruct((m, d_ff), F32),
                   jax.ShapeDtypeStruct((1, k, d_ff), BF16), jax.ShapeDtypeStruct((1, k, d_ff), BF16)),
        grid=(n_tiles,),
        in_specs=[pl.BlockSpec((m, k), lambda j: (0, 0)),
                  pl.BlockSpec((None, k, tn), lambda j: (layer, 0, j)),
                  pl.BlockSpec((None, k, tn), lambda j: (layer, 0, n_tiles + j)),
                  pl.BlockSpec((None, CONV_W, tn), lambda j: (layer, 0, j)),
                  pl.BlockSpec((None, 1, tn), lambda j: (layer, 0, j)),
                  pl.BlockSpec((m, tn), lambda j: (0, j)),
                  pl.BlockSpec((m, tn), lambda j: (0, j))],
        out_specs=(pl.BlockSpec((m, tn), lambda j: (0, j)),
                   pl.BlockSpec((m, tn), lambda j: (0, j)),
                   pl.BlockSpec((None, k, tn), lambda j: (0, 0, j)),
                   pl.BlockSpec((None, k, tn), lambda j: (0, 0, j))),
        compiler_params=_cparams("parallel"),
        name=name,
    )(x, w, w, cw, cb, h1, h2)


def _compress_rows(xa, w1a, w1b, w2, hpe):
    a = _dot(xa, w1a)
    b = _dot(xa, w1b)
    n = a.shape[0]
    h = a + pltpu.roll(b, n - 1, 0) + hpe
    return _dot(_gelu(h).astype(BF16), w2)


def _pos_embed_term(pe_ref, w1_ref, kind):
    return _dot(pe_ref[kind], w1_ref[kind])[0:1, :]


def _compress_p_kernel(k_ref, v_ref, w1_ref, w2_ref, pe_ref, kc_ref, vc_ref, *, n_chunks):
    half = STRIDE * HEAD_DIM
    for kind, (src, dst) in enumerate(((k_ref, kc_ref), (v_ref, vc_ref))):
        xa = jnp.concatenate(
            [src[pl.ds(s, n_chunks, stride=STRIDE), :] for s in range(STRIDE)], axis=1).astype(BF16)
        hpe = _pos_embed_term(pe_ref, w1_ref, kind)
        out = _compress_rows(xa, w1_ref[kind, 0:half, :], w1_ref[kind, half:2 * half, :],
                             w2_ref[kind], hpe)
        dst[...] = out.astype(dst.dtype)


def _compress_prompt(kv, w1, w2, pe, *, batch, seq, n_kv, name):
    n_chunks = seq // STRIDE
    out = jax.ShapeDtypeStruct((batch, n_kv, n_chunks, HEAD_DIM), BF16)
    ospec = pl.BlockSpec((None, None, n_chunks, HEAD_DIM), lambda b, h: (b, h, 0, 0))
    return pl.pallas_call(
        functools.partial(_compress_p_kernel, n_chunks=n_chunks),
        out_shape=(out, out),
        grid=(batch, n_kv),
        in_specs=[pl.BlockSpec((seq, HEAD_DIM), lambda b, h: (b, h)),
                  pl.BlockSpec((seq, HEAD_DIM), lambda b, h: (b, n_kv + h)),
                  pl.BlockSpec(w1.shape, lambda b, h: (0, 0, 0)),
                  pl.BlockSpec(w2.shape, lambda b, h: (0, 0, 0)),
                  pl.BlockSpec(pe.shape, lambda b, h: (0, 0, 0))],
        out_specs=(ospec, ospec),
        compiler_params=_cparams("parallel", "parallel"),
        name=name,
    )(kv, kv, w1, w2, pe)


def _block_scores(imp, pos, n_blocks, block_axis):
    j = lax.broadcasted_iota(jnp.int32, imp.shape, block_axis)
    cur = pos // L_SLC
    forced = (j == 0) | (j == cur) | (j == cur - 1)
    valid = j * L_SLC <= pos
    score = jnp.where(valid, jnp.where(forced, SEL_FORCE, imp), -SEL_FORCE)
    return jnp.where(j < n_blocks, score, NEG_BIG)


def _overlap_matrix(nc, nb, rows, cols):
    i = np.arange(nc)[:, None]
    j = np.arange(nb)[None, :]
    lo = np.maximum(i * STRIDE, j * L_SLC)
    hi = np.minimum(i * STRIDE + L_CMP, (j + 1) * L_SLC)
    ov = np.zeros((rows, cols), np.float32)
    ov[:nc, :nb] = np.maximum(hi - lo, 0) / STRIDE
    return ov


def _block_to_key_matrix(n_groups, lanes, keys_per_group):
    ex = np.zeros((n_groups, lanes, keys_per_group), np.float32)
    for c in range(n_groups):
        k = np.arange(keys_per_group)
        ex[c, (c * keys_per_group + k) // L_SLC, k] = 1.0
    return ex


def _attn_p_kernel(q_ref, kc_ref, vc_ref, ks_ref, vs_ref, kw_ref, vw_ref, gt_ref, ovt_ref, ex_ref,
                   o_ref, m_scr, acc_scr, *, tq, tk, gqa, hpb, n_blocks, n_cmp, win_keys):
    qi = pl.program_id(2)
    t0 = qi * tq
    q = q_ref[...] * LOGIT_SCALE
    qs = [[q[:, (hh * gqa + g) * HEAD_DIM:(hh * gqa + g + 1) * HEAD_DIM].astype(BF16)
           for g in range(gqa)] for hh in range(hpb)]
    pos_t = t0 + lax.broadcasted_iota(jnp.int32, (tq, 1), 0)
    pos_row = t0 + lax.broadcasted_iota(jnp.int32, (1, tq), 1)
    ones = jnp.ones((max(tk, win_keys), HEAD_DIM), BF16)
    lanes = ex_ref.shape[1]

    n = lax.broadcasted_iota(jnp.int32, (tq, kc_ref.shape[1]), 1)
    last_end = jnp.minimum(pos_t, (n_cmp - 1) * STRIDE + (L_CMP - 1))
    bias_c = jnp.where(n * STRIDE + (L_CMP - 1) <= last_end, 0.0, NEG_BIG)
    o_c, sel = [], []
    for hh in range(hpb):
        kc = kc_ref[hh]
        vc = vc_ref[hh]
        p_grp = None
        o_c.append([])
        for g in range(gqa):
            sm = _dot_nt(qs[hh][g], kc) + bias_c
            m = jnp.max(sm, axis=-1, keepdims=True)
            e = jnp.exp2(sm - m)
            norm = jnp.where(m > 0.5 * NEG_BIG,
                             1.0 / jnp.maximum(jnp.sum(e, axis=-1, keepdims=True), 1e-30), 0.0)
            p = e * norm
            o_c[hh].append(_dot(p.astype(BF16), vc))
            p_grp = p if p_grp is None else p_grp + p

        hi, lo = _split_hi_lo(p_grp)
        imp_t = _dot_nt(ovt_ref[...], hi) + _dot_nt(ovt_ref[...], lo)
        score = _block_scores(imp_t, pos_row, n_blocks, 0)
        blk = lax.broadcasted_iota(jnp.int32, score.shape, 0)
        rank = jnp.zeros(score.shape, F32)
        for i in range(n_blocks):
            ci = score[i:i + 1, :]
            rank = rank + jnp.where((ci > score) | ((ci == score) & (blk > i)), 1.0, 0.0)
        sel_t = jnp.where(rank < float(min(N_SEL, n_blocks)), 1.0, 0.0)
        sel_t = jnp.concatenate([sel_t, jnp.zeros((lanes - sel_t.shape[0], tq), F32)], axis=0)
        sel.append(sel_t.T.astype(BF16))

    m_scr[...] = jnp.full(m_scr.shape, NEG_BIG, F32)
    acc_scr[...] = jnp.zeros(acc_scr.shape, F32)

    def key_tile(c, carry):
        k0 = pl.multiple_of(c * tk, tk)
        kpos = k0 + lax.broadcasted_iota(jnp.int32, (tq, tk), 1)
        causal = jnp.where(kpos <= pos_t, 0.5, 2.0)
        for hh in range(hpb):
            cols = slice(hh * HEAD_DIM, (hh + 1) * HEAD_DIM)
            kt = ks_ref[pl.ds(k0, tk), cols].astype(BF16)
            vt = jnp.concatenate([vs_ref[pl.ds(k0, tk), cols].astype(BF16), ones[0:tk]], axis=1)
            bias = jnp.where(_dot(sel[hh], ex_ref[c]) > causal, 0.0, NEG_BIG)
            for g in range(gqa):
                rows = slice((hh * gqa + g) * tq, (hh * gqa + g + 1) * tq)
                sm = _dot_nt(qs[hh][g], kt) + bias
                m_old = m_scr[rows]
                m_new = jnp.maximum(m_old, jnp.max(sm, axis=-1, keepdims=True))
                e = jnp.exp2(sm - m_new)
                acc_scr[rows] = jnp.exp2(m_old - m_new) * acc_scr[rows] + _dot(e.astype(BF16), vt)
                m_scr[rows] = m_new
        return carry

    lax.fori_loop(0, (t0 + tq - 1) // tk + 1, key_tile, 0)

    start = pl.multiple_of(jnp.maximum(t0 + tq - win_keys, 0), 128)
    d = pos_t - (start + lax.broadcasted_iota(jnp.int32, (tq, win_keys), 1))
    bias_w = jnp.where(d >= 0, jnp.where(d < WINDOW, 0.0, NEG_BIG), NEG_BIG)
    for hh in range(hpb):
        cols = slice(hh * HEAD_DIM, (hh + 1) * HEAD_DIM)
        kw = kw_ref[pl.ds(start, win_keys), cols].astype(BF16)
        vw = jnp.concatenate([vw_ref[pl.ds(start, win_keys), cols].astype(BF16), ones[0:win_keys]], axis=1)
        gt = gt_ref[hh]
        for g in range(gqa):
            sm = _dot_nt(qs[hh][g], kw) + bias_w
            e = jnp.exp2(sm - jnp.max(sm, axis=-1, keepdims=True))
            ow = _dot(e.astype(BF16), vw)
            o_w = ow[:, 0:HEAD_DIM] / ow[:, HEAD_DIM:HEAD_DIM + 1]
            acc = acc_scr[(hh * gqa + g) * tq:(hh * gqa + g + 1) * tq]
            o_s = acc[:, 0:HEAD_DIM] / jnp.maximum(acc[:, HEAD_DIM:HEAD_DIM + 1], 1e-30)
            out = (gt[:, g:g + 1] * o_c[hh][g] + gt[:, gqa + g:gqa + g + 1] * o_s
                   + gt[:, 2 * gqa + g:2 * gqa + g + 1] * o_w)
            o_ref[:, (hh * gqa + g) * HEAD_DIM:(hh * gqa + g + 1) * HEAD_DIM] = out.astype(o_ref.dtype)


def _attn_prompt(q2d, kv, kcmp, vcmp, gates_h, *, batch, seq, n_heads, n_kv, tq=256, tk=1024, hpb=2,
                 name):
    gqa = n_heads // n_kv
    tq = _tile(seq, tq, 8)
    tk = _tile(seq, tk)
    hpb = _tile(n_kv, hpb, 1)
    n_grp = n_kv // hpb
    nq = seq // tq
    n_chunks = seq // STRIDE
    n_cmp = n_chunks - 1
    n_blocks = -(-seq // L_SLC)
    lanes = -(-n_blocks // 128) * 128
    block_rows = -(-n_blocks // 8) * 8
    ovt = jnp.asarray(_overlap_matrix(n_cmp, n_blocks, n_chunks, block_rows).T.copy(), BF16)
    ex = jnp.asarray(_block_to_key_matrix(seq // tk, lanes, tk), BF16)
    win_keys = min(WINDOW + tq, seq)
    rows = hpb * gqa * tq
    kern = functools.partial(_attn_p_kernel, tq=tq, tk=tk, gqa=gqa, hpb=hpb, n_blocks=n_blocks,
                             n_cmp=n_cmp, win_keys=win_keys)

    def kv_spec(kind):
        return pl.BlockSpec((seq, hpb * HEAD_DIM), lambda b, h, i: (b, kind * n_grp + h))

    cmp_spec = pl.BlockSpec((None, hpb, n_chunks, HEAD_DIM), lambda b, h, i: (b, h, 0, 0))
    return pl.pallas_call(
        kern,
        out_shape=jax.ShapeDtypeStruct((batch * seq, n_heads * HEAD_DIM), BF16),
        grid=(batch, n_grp, nq),
        in_specs=[pl.BlockSpec((tq, hpb * gqa * HEAD_DIM), lambda b, h, i: (b * nq + i, h)),
                  cmp_spec, cmp_spec,
                  kv_spec(2), kv_spec(3), kv_spec(4), kv_spec(5),
                  pl.BlockSpec((hpb, tq, 3 * gqa), lambda b, h, i: (h, b * nq + i, 0)),
                  pl.BlockSpec(ovt.shape, lambda b, h, i: (0, 0)),
                  pl.BlockSpec(ex.shape, lambda b, h, i: (0, 0, 0))],
        out_specs=pl.BlockSpec((tq, hpb * gqa * HEAD_DIM), lambda b, h, i: (b * nq + i, h)),
        scratch_shapes=[pltpu.VMEM((rows, 1), F32),
                        pltpu.VMEM((rows, 2 * HEAD_DIM), F32)],
        compiler_params=_cparams("parallel", "parallel", "arbitrary"),
        name=name,
    )(q2d, kcmp, vcmp, kv, kv, kv, kv, gates_h, ovt, ex)


def _head_slabs(rows_ref, first, n_heads):
    return jnp.concatenate([rows_ref[:, first + h, :] for h in range(n_heads)], axis=1)


def _page_copies(cache_ref, buf_ref, sem_ref, pt_ref, b, step, slot, *, layer, row0, n_rows,
                 pages_per_step, need_ref=None, n_heads=None):
    groups = []
    for k in range(pages_per_step):
        page = pt_ref[b, step * pages_per_step + k]
        copies = [pltpu.make_async_copy(cache_ref.at[layer, page, :, row0 + r, :],
                                        buf_ref.at[slot, k, r], sem_ref.at[slot])
                  for r in range(n_rows)]
        if need_ref is None:
            groups.append((None, copies))
        else:
            for h in range(n_heads):
                flag = need_ref[b, (step * pages_per_step + k) * n_heads + h] != 0
                groups.append((flag, copies[h::n_heads]))
    return groups


def _page_stream(cache_ref, buf_ref, sem_ref, pt_ref, **kw):
    b, p = pl.program_id(0), pl.program_id(1)
    n_b, n_p = pl.num_programs(0), pl.num_programs(1)
    g = b * n_p + p
    slot = lax.rem(g, 2)

    def for_each(groups, fn):
        n = 0
        for flag, copies in groups:
            def run(copies=copies, n=n):
                for i, c in enumerate(copies):
                    fn(c, n + i)
            if flag is None:
                run()
            else:
                pl.when(flag)(run)
            n += len(copies)

    def start(c, n):
        c.start(priority=n % 2)

    @pl.when(g == 0)
    def _():
        if kw.get("need_ref") is not None:
            buf_ref[...] = jnp.zeros(buf_ref.shape, buf_ref.dtype)
        for_each(_page_copies(cache_ref, buf_ref, sem_ref, pt_ref, 0, 0, 0, **kw), start)

    @pl.when(g + 1 < n_b * n_p)
    def _():
        wrap = p + 1 == n_p
        for_each(_page_copies(cache_ref, buf_ref, sem_ref, pt_ref, jnp.where(wrap, b + 1, b),
                              jnp.where(wrap, 0, p + 1), 1 - slot, **kw), start)

    for_each(_page_copies(cache_ref, buf_ref, sem_ref, pt_ref, b, p, slot, **kw), lambda c, n: c.wait())
    return slot


def _cmp_s_kernel(pt_ref, cache_ref, q_ref, w1_ref, w2_ref, pe_ref, ovt_ref, oc_ref, sel_ref,
                  x_scr, ab_scr, buf_ref, sem_ref, *, layer, n_kv, n_steps, pages_per_step,
                  steps_per_group, n_cmp, n_blocks, dec_seq, gqa, past):
    p = pl.program_id(1)
    n_kh = 2 * n_kv
    page = buf_ref.shape[3]
    slot = _page_stream(cache_ref, buf_ref, sem_ref, pt_ref, layer=layer, row0=0, n_rows=n_kh,
                        pages_per_step=pages_per_step)
    cpp = page // STRIDE
    step_chunks = cpp * pages_per_step
    group_chunks = step_chunks * steps_per_group
    c0 = pl.multiple_of(lax.rem(p, steps_per_group) * step_chunks, 8)
    for k in range(pages_per_step):
        for s in range(STRIDE):
            for kh in range(n_kh):
                x_scr[kh, pl.ds(c0 + k * cpp, cpp), s * HEAD_DIM:(s + 1) * HEAD_DIM] = (
                    buf_ref[slot, k, kh, pl.ds(s, cpp, stride=STRIDE), :])

    @pl.when(lax.rem(p, steps_per_group) == steps_per_group - 1)
    def _():
        g0 = pl.multiple_of((p // steps_per_group) * group_chunks, 8)
        for kh in range(n_kh):
            kind = kh // n_kv
            ab_scr[kh, pl.ds(g0, group_chunks), :] = _dot(x_scr[kh].astype(BF16), w1_ref[kind])

    @pl.when(p == n_steps - 1)
    def _():
        n_chunks = ab_scr.shape[1]
        comp = []
        for kh in range(n_kh):
            kind = kh // n_kv
            hpe = _dot(pe_ref[kind], w1_ref[kind])
            hpe = hpe[0:1, 0:HEAD_DIM] + hpe[8:9, HEAD_DIM:2 * HEAD_DIM]
            ab = ab_scr[kh]
            h = ab[:, 0:HEAD_DIM] + pltpu.roll(ab[:, HEAD_DIM:2 * HEAD_DIM], n_chunks - 1, 0) + hpe
            comp.append(_dot(_gelu(h).astype(BF16), w2_ref[kind]).astype(BF16))
        qrows = lax.broadcasted_iota(jnp.int32, (1, HEAD_DIM), 1)
        pos = past + lax.rem(qrows, dec_seq)
        for h in range(n_kv):
            kc, vc = comp[h], comp[n_kv + h]
            st = _dot_nt(kc, (q_ref[h] * LOGIT_SCALE).astype(BF16))
            n = lax.broadcasted_iota(jnp.int32, st.shape, 0)
            pt = _masked_softmax2(st, (n * STRIDE + (L_CMP - 1) <= pos) & (n < n_cmp), axis=0)
            oc_ref[h] = _dot(pt.T.astype(BF16), vc)
            pg = pt
            for g in range(1, gqa):
                pg = pg + pltpu.roll(pt, HEAD_DIM - g * dec_seq, 1)
            hi, lo = _split_hi_lo(pg)
            imp_t = _dot(ovt_ref[...], hi) + _dot(ovt_ref[...], lo)
            imp = imp_t.T
            tpos = past + lax.broadcasted_iota(jnp.int32, (imp.shape[0], 1), 0)
            score = _block_scores(imp, tpos, n_blocks, 1)
            score_t = score.T
            nb_pad = score.shape[1]
            ii = lax.broadcasted_iota(jnp.int32, (nb_pad, nb_pad), 0)
            jj = lax.broadcasted_iota(jnp.int32, (nb_pad, nb_pad), 1)
            for t in range(dec_seq):
                col = score_t[:, t:t + 1]
                rowv = score[t:t + 1, :]
                beats = (col > rowv) | ((col == rowv) & (ii < jj))
                rank = jnp.sum(jnp.where(beats, 1.0, 0.0), axis=0, keepdims=True)
                sel_ref[h, t:t + 1, :] = jnp.where(rank < float(min(N_SEL, n_blocks)), 1.0, 0.0)
            sel_ref[h, dec_seq:, :] = jnp.zeros((sel_ref.shape[1] - dec_seq, nb_pad), F32)


def _cmp_sample(page_table, cache_rows, q_pad, w1cat, w2, pe, *, layer, n_kv, gqa, dec_seq, name):
    batch, n_pages = page_table.shape
    page = cache_rows.shape[2]
    past = n_pages * page
    cpp = page // STRIDE
    n_chunks = past // STRIDE
    assert dec_seq < STRIDE and n_chunks % 8 == 0
    n_cmp = (past + dec_seq) // STRIDE - 1
    n_blocks = -(-(past + dec_seq) // L_SLC)
    nb_pad = -(-n_blocks // 128) * 128
    pps = _tile(n_pages, 4, 1)
    n_steps = n_pages // pps
    spg = _tile(n_steps, max(1, 128 // (cpp * pps)), 1)
    n_kh = 2 * n_kv
    ovt = jnp.asarray(_overlap_matrix(n_cmp, n_blocks, n_chunks, nb_pad).T.copy(), BF16)
    kern = functools.partial(_cmp_s_kernel, layer=layer, n_kv=n_kv, n_steps=n_steps, pages_per_step=pps,
                             steps_per_group=spg, n_cmp=n_cmp, n_blocks=n_blocks, dec_seq=dec_seq,
                             gqa=gqa, past=past)
    grid_spec = pltpu.PrefetchScalarGridSpec(
        num_scalar_prefetch=1,
        grid=(batch, n_steps),
        in_specs=[
            pl.BlockSpec(memory_space=pl.ANY),
            pl.BlockSpec((None, n_kv, 128, HEAD_DIM), lambda b, p, pt: (b, 0, 0, 0)),
            pl.BlockSpec(w1cat.shape, lambda b, p, pt: (0, 0, 0)),
            pl.BlockSpec(w2.shape, lambda b, p, pt: (0, 0, 0)),
            pl.BlockSpec(pe.shape, lambda b, p, pt: (0, 0, 0)),
            pl.BlockSpec(ovt.shape, lambda b, p, pt: (0, 0))],
        out_specs=(pl.BlockSpec((None, n_kv, 128, HEAD_DIM), lambda b, p, pt: (b, 0, 0, 0)),
                   pl.BlockSpec((None, n_kv, 8, nb_pad), lambda b, p, pt: (b, 0, 0, 0))),
        scratch_shapes=[pltpu.VMEM((n_kh, spg * pps * cpp, STRIDE * HEAD_DIM), F32),
                        pltpu.VMEM((n_kh, n_chunks, 2 * HEAD_DIM), F32),
                        pltpu.VMEM((2, pps, n_kh, page, HEAD_DIM), F32),
                        pltpu.SemaphoreType.DMA((2,))],
    )
    return pl.pallas_call(
        kern,
        out_shape=(jax.ShapeDtypeStruct((batch, n_kv, 128, HEAD_DIM), F32),
                   jax.ShapeDtypeStruct((batch, n_kv, 8, nb_pad), F32)),
        grid_spec=grid_spec,
        compiler_params=_cparams("arbitrary", "arbitrary"),
        name=name,
    )(page_table, cache_rows, q_pad, w1cat, w2, pe, ovt)


def _slc_s_kernel(pt_ref, need_ref, cache_ref, q_ref, sel_ref, ex_ref, kn_ref, vn_ref, win_ref, kwn_ref, vwn_ref,
                  oc_ref, gt_ref, o_ref, qbd_scr, m_scr, l_scr, acc_scr, buf_ref, sem_ref,
                  *, layer, n_kv, n_steps, pages_per_step, rows_per_head, dec_seq, past, w_buf):
    p = pl.program_id(1)
    slot = _page_stream(cache_ref, buf_ref, sem_ref, pt_ref, layer=layer, row0=2 * n_kv,
                        n_rows=2 * n_kv, pages_per_step=pages_per_step, need_ref=need_ref, n_heads=n_kv)
    rows = n_kv * rows_per_head

    @pl.when(p == 0)
    def _():
        qbd_scr[...] = jnp.zeros(qbd_scr.shape, qbd_scr.dtype)
        for h in range(n_kv):
            qbd_scr[h * rows_per_head:(h + 1) * rows_per_head, h * HEAD_DIM:(h + 1) * HEAD_DIM] = (
                q_ref[h, 0:rows_per_head, :] * LOGIT_SCALE)
        m_scr[...] = jnp.full(m_scr.shape, NEG_BIG, F32)
        l_scr[...] = jnp.zeros(l_scr.shape, F32)
        acc_scr[...] = jnp.zeros(acc_scr.shape, F32)

    qbd = qbd_scr[...].astype(BF16)

    def online_update(s, mask, v):
        sm = jnp.where(mask, s, NEG_BIG)
        m_old = m_scr[...]
        m_new = jnp.maximum(m_old, jnp.max(sm, axis=-1, keepdims=True))
        alpha = jnp.exp2(m_old - m_new)
        e = jnp.where(mask, jnp.exp2(sm - m_new), 0.0)
        l_scr[...] = alpha * l_scr[...] + jnp.sum(e, axis=-1, keepdims=True)
        acc_scr[...] = alpha * acc_scr[...] + _dot(e.astype(BF16), v)
        m_scr[...] = m_new

    ks, vs = [], []
    for k in range(pages_per_step):
        ks.append(jnp.concatenate([buf_ref[slot, k, h] for h in range(n_kv)], axis=1))
        vs.append(jnp.concatenate([buf_ref[slot, k, n_kv + h] for h in range(n_kv)], axis=1))
    k_all = jnp.concatenate(ks, axis=0).astype(BF16)
    v_all = jnp.concatenate(vs, axis=0).astype(BF16)
    s = _dot_nt(qbd, k_all)
    sel_keys = _dot(sel_ref[...].astype(BF16), ex_ref[...])
    online_update(s, sel_keys > 0.5, v_all)

    @pl.when(p == n_steps - 1)
    def _():
        r = lax.broadcasted_iota(jnp.int32, (rows, 1), 0)
        t = lax.rem(r, dec_seq)
        s = _dot_nt(qbd, kn_ref[...].astype(BF16))
        j = lax.broadcasted_iota(jnp.int32, s.shape, 1)
        online_update(s, (j <= t) & (j < dec_seq), vn_ref[...].astype(BF16))
        o_s = acc_scr[...] / jnp.maximum(l_scr[...], 1e-30)

        sb = _dot_nt(qbd, _head_slabs(win_ref, 0, n_kv).astype(BF16))
        sn = _dot_nt(qbd, kwn_ref[...].astype(BF16))
        ib = lax.broadcasted_iota(jnp.int32, sb.shape, 1)
        kpos = past - w_buf + ib
        d = (past + t) - kpos
        mb = (d >= 0) & (d < WINDOW) & (kpos >= 0)
        jn = lax.broadcasted_iota(jnp.int32, sn.shape, 1)
        mn = (jn <= t) & (jn < dec_seq) & (t - jn < WINDOW)
        smb = jnp.where(mb, sb, NEG_BIG)
        smn = jnp.where(mn, sn, NEG_BIG)
        mx = jnp.maximum(jnp.max(smb, axis=-1, keepdims=True), jnp.max(smn, axis=-1, keepdims=True))
        eb = jnp.where(mb, jnp.exp2(smb - mx), 0.0)
        en = jnp.where(mn, jnp.exp2(smn - mx), 0.0)
        den = jnp.maximum(jnp.sum(eb, axis=-1, keepdims=True) + jnp.sum(en, axis=-1, keepdims=True),
                          1e-30)
        o_w = (_dot((eb / den).astype(BF16), _head_slabs(win_ref, n_kv, n_kv).astype(BF16))
               + _dot((en / den).astype(BF16), vwn_ref[...].astype(BF16)))

        for h in range(n_kv):
            r0 = h * rows_per_head
            c0 = h * HEAD_DIM
            gt = gt_ref[h]
            o_ref[h] = (gt[:, 0:1] * oc_ref[h, 0:rows_per_head, :]
                        + gt[:, 1:2] * o_s[r0:r0 + rows_per_head, c0:c0 + HEAD_DIM]
                        + gt[:, 2:3] * o_w[r0:r0 + rows_per_head, c0:c0 + HEAD_DIM])


def _slc_sample(page_table, need, cache_rows, q_pad, sel_steps, k_new, v_new, win_rows, kw_new, vw_new, o_c,
                gates, *, layer, n_kv, gqa, dec_seq, pages_per_step, name):
    batch, n_pages = page_table.shape
    page = cache_rows.shape[2]
    past = n_pages * page
    width = n_kv * HEAD_DIM
    n_kh = 2 * n_kv
    rph = gqa * dec_seq
    rows = n_kv * rph
    w_buf = win_rows.shape[2]
    n_new = k_new.shape[1]
    pps = pages_per_step
    n_steps = n_pages // pps
    ex = jnp.asarray(_block_to_key_matrix(1, sel_steps.shape[3], pps * page)[0], BF16)
    kern = functools.partial(_slc_s_kernel, layer=layer, n_kv=n_kv, n_steps=n_steps, pages_per_step=pps,
                             rows_per_head=rph, dec_seq=dec_seq, past=past, w_buf=w_buf)

    def new_spec():
        return pl.BlockSpec((None, n_new, width), lambda b, p, pt, nd: (b, 0, 0))

    grid_spec = pltpu.PrefetchScalarGridSpec(
        num_scalar_prefetch=2,
        grid=(batch, n_steps),
        in_specs=[
            pl.BlockSpec(memory_space=pl.ANY),
            pl.BlockSpec((None, n_kv, 128, HEAD_DIM), lambda b, p, pt, nd: (b, 0, 0, 0)),
            pl.BlockSpec((None, None, rows, sel_steps.shape[3]), lambda b, p, pt, nd: (b, p, 0, 0)),
            pl.BlockSpec(ex.shape, lambda b, p, pt, nd: (0, 0)),
            new_spec(), new_spec(),
            pl.BlockSpec((None, None, w_buf, n_kh, HEAD_DIM), lambda b, p, pt, nd: (layer, b, 0, 0, 0)),
            new_spec(), new_spec(),
            pl.BlockSpec((None, n_kv, 128, HEAD_DIM), lambda b, p, pt, nd: (b, 0, 0, 0)),
            pl.BlockSpec((None, n_kv, rph, 8), lambda b, p, pt, nd: (b, 0, 0, 0))],
        out_specs=pl.BlockSpec((None, n_kv, rph, HEAD_DIM), lambda b, p, pt, nd: (b, 0, 0, 0)),
        scratch_shapes=[pltpu.VMEM((rows, width), F32),
                        pltpu.VMEM((rows, 1), F32),
                        pltpu.VMEM((rows, 1), F32),
                        pltpu.VMEM((rows, width), F32),
                        pltpu.VMEM((2, pps, n_kh, page, HEAD_DIM), F32),
                        pltpu.SemaphoreType.DMA((2,))],
    )
    return pl.pallas_call(
        kern,
        out_shape=jax.ShapeDtypeStruct((batch, n_kv, rph, HEAD_DIM), F32),
        grid_spec=grid_spec,
        compiler_params=_cparams("arbitrary", "arbitrary"),
        name=name,
    )(page_table, need, cache_rows, q_pad, sel_steps, ex, k_new, v_new, win_rows, kw_new, vw_new, o_c,
      gates)


def _rope_tables(pos):
    inv = ROPE_THETA ** (-jnp.arange(ROT_HALF, dtype=F32) * 2.0 / ROT_DIM)
    ang = pos.astype(F32)[:, None] * inv[None, :]
    cos, sin = jnp.cos(ang), jnp.sin(ang)
    rest = HEAD_DIM - ROT_DIM
    cos_t = jnp.concatenate([cos, cos, jnp.ones((pos.shape[0], rest), F32)], axis=1)
    sin_t = jnp.concatenate([-sin, sin, jnp.zeros((pos.shape[0], rest), F32)], axis=1)
    return cos_t, sin_t


def _prep_weights(w_in, ffn_conv_b, n_gate):
    w_gate = jnp.pad(w_in[:, :, w_in.shape[2] - n_gate:], ((0, 0), (0, 0), (0, 128 - n_gate)))
    return dict(w_gate=w_gate.astype(BF16),
                f_cb=ffn_conv_b.reshape(ffn_conv_b.shape[0], 1, ffn_conv_b.shape[1]))


def _prep_compress(l, cmp_pe, cmp_w1, cmp_w2):
    half = STRIDE * HEAD_DIM
    w1 = cmp_w1[l].reshape(2, 2 * half, HEAD_DIM)
    wl = {}
    wl["cmp_w1"] = w1.astype(BF16)
    wl["cmp_w1cat"] = jnp.concatenate([w1[:, :half], w1[:, half:]], axis=2).astype(BF16)
    wl["cmp_w2"] = cmp_w2[l].astype(BF16)
    pe = cmp_pe[l].reshape(2, 1, 2 * half)
    wl["cmp_pe"] = jnp.broadcast_to(pe, (2, 16, 2 * half)).astype(BF16)
    pe2 = cmp_pe[l].reshape(2, 2, 1, half)
    wl["cmp_pecat"] = jnp.concatenate([jnp.broadcast_to(pe2[:, 0], (2, 8, half)),
                                       jnp.broadcast_to(pe2[:, 1], (2, 8, half))], axis=1).astype(BF16)
    return wl


def _project(x_bf, ws, wb, l, cos_t, sin_t, dims, tag):
    d_a, d_b, d_kv, n_heads, d_ff = dims
    n_kv = d_kv // HEAD_DIM
    emit = wb is None
    if emit:
        src = dict(uv=(ws["w_in_t"], l, 0), q=(ws["w_in_t"], l, 2 * d_a), kv=(ws["w_in_t"], l, 2 * d_a + d_b))
    else:
        src = dict(uv=(wb["uv"], 0, 0), q=(wb["q"], 0, 0), kv=(wb["kv"], 0, 0))
    new = {}
    w, lay, c0 = src["uv"]
    out = _matmul([x_bf], w, layer=lay, col_start=c0, n=2 * d_a, epilogue="gelu", tm=1024,
                  tn=512 if emit else 1024, emit=emit, w_transposed=emit, name=f"proj_uv_{tag}")
    uv = out[0] if emit else out
    if emit:
        new["uv"] = out[1]
    w, lay, c0 = src["q"]
    out = _rope_matmul(x_bf, w, cos_t, sin_t, layer=lay, col_start=c0, n=d_b,
                       tn=d_kv if emit else _tile(d_b, 2 * d_kv), alternate=False, emit=emit,
                       name=f"proj_q_{tag}")
    q2d = out[0]
    if emit:
        new["q"] = out[-1]
    w, lay, c0 = src["kv"]
    out = _rope_matmul(x_bf, w, cos_t, sin_t, layer=lay, col_start=c0, n=6 * d_kv, tn=d_kv, alternate=True,
                       n_kv=n_kv, emit=emit, name=f"proj_kv_{tag}")
    kv, nsa4, win4 = out[0], out[1], out[2]
    if emit:
        new["kv"] = out[-1]
    gates = _matmul([x_bf], ws["w_gate"], layer=l, epilogue="sigmoid", tm=1024, tn=128,
                    name=f"proj_gate_{tag}")
    return uv, q2d, kv, nsa4, win4, gates, new


def _mix_and_norm(x, a_out, b_out, ws, wb, l, ln_g, ln_b, alpha, tag):
    new = {}
    if wb is None:
        z, new["o_a"], new["o_b"] = _matmul([a_out, b_out], ws["w_o"], layer=l, tm=1024, tn=512,
                                            emit=True, residual=x, res_scale=alpha, name=f"w_o_{tag}")
    else:
        z = _matmul([a_out, b_out], [wb["o_a"], wb["o_b"]], layer=0, tm=1024, tn=1024,
                    residual=x, res_scale=alpha, name=f"w_o_{tag}")
    y, yb = _layer_norm(z, ln_g[l, 0], ln_b[l, 0], name=f"ln1_{tag}")
    return y, yb, new


def _layer_prompt(x, x_bf, ws, wb, wl, l, p, cos_t, sin_t, dims, batch, seq, n_kv, alpha):
    d_a, d_b, d_kv, n_heads, d_ff = dims
    gqa = n_heads // n_kv
    m = batch * seq
    uv, q2d, kv, nsa4, win4, gates, _ = _project(x_bf, ws, wb, l, cos_t, sin_t, dims, "p")
    (a_out,) = _sgu(uv, p["sgu_w"][l], p["sgu_b"][l].T, p["sgu_g"][l], rows=CHUNK, with_vn=False,
                    name="sgu_p")
    kcmp, vcmp = _compress_prompt(kv, wl["cmp_w1"], wl["cmp_w2"], wl["cmp_pe"], batch=batch, seq=seq,
                                  n_kv=n_kv, name="compress_p")
    gates_h = gates[:, :3 * n_heads].reshape(m, 3, n_kv, gqa).transpose(2, 0, 1, 3).reshape(n_kv, m, 3 * gqa)
    b_out = _attn_prompt(q2d, kv, kcmp, vcmp, gates_h, batch=batch, seq=seq, n_heads=n_heads, n_kv=n_kv,
                         name="nsa_p")
    x1, x1b, _ = _mix_and_norm(x, a_out, b_out, ws, wb, l, p["ln_g"], p["ln_b"], alpha, "p")
    state0 = jnp.zeros((batch, CONV_W - 1, d_ff), F32)
    act, conv_new = _ffn_in_seq(x1b, wb["f_gate"], wb["f_up"], p["ffn_conv_w"], ws["f_cb"], state0,
                                layer=l, seq_len=seq, name="ffn_in_p")
    z = _matmul([act], wb["f_down"], layer=0, tm=512, tn=512, residual=x1, res_scale=alpha,
                name="ffn_down_p")
    x2, x2b = _layer_norm(z, p["ln_g"][l, 1], p["ln_b"][l, 1], name="ln2_p")
    new_nsa = nsa4.reshape(batch, seq, 4, n_kv, HEAD_DIM)
    new_win = win4.reshape(batch, seq, 2, n_kv, HEAD_DIM)[:, -min(WINDOW, seq):]
    return x2, x2b, new_nsa, new_win, conv_new


def _layer_sample(x, x_bf, ws, wl, l, p, cos_t, sin_t, dims, batch, dec_seq, n_kv, alpha, page_table,
                  cache_rows, win_rows, conv_state):
    d_a, d_b, d_kv, n_heads, d_ff = dims
    gqa = n_heads // n_kv
    m = batch * dec_seq
    rph = gqa * dec_seq
    uv, q2d, kv, nsa4, win4, gates, wb = _project(x_bf, ws, None, l, cos_t, sin_t, dims, "s")
    w_small = p["sgu_w"][l][:, :dec_seq, :dec_seq]
    eye = jnp.eye(batch, dtype=F32)
    w_bd = jnp.einsum("ab,gts->gatbs", eye, w_small).reshape(-1, m, m)
    bt_bd = jnp.tile(p["sgu_b"][l].T[:dec_seq], (batch, 1))
    a_out, v_rows = _sgu(uv, w_bd, bt_bd, p["sgu_g"][l], rows=m, with_vn=True, name="sgu_s")

    q = q2d.reshape(batch, dec_seq, n_kv, gqa, HEAD_DIM).transpose(0, 2, 3, 1, 4)
    q_pad = jnp.pad(q.reshape(batch, n_kv, rph, HEAD_DIM), ((0, 0), (0, 0), (0, 128 - rph), (0, 0)))
    o_c, sel = _cmp_sample(page_table, cache_rows, q_pad, wl["cmp_w1cat"], wl["cmp_w2"], wl["cmp_pecat"],
                           layer=l, n_kv=n_kv, gqa=gqa, dec_seq=dec_seq, name="cmp_s")
    n_pages = page_table.shape[1]
    bpp = cache_rows.shape[2] // L_SLC
    pps = _tile(n_pages, 4, 1)
    n_steps = n_pages // pps
    sel_steps = sel[:, :, :dec_seq, :n_pages * bpp].reshape(batch, n_kv, 1, dec_seq, n_steps, pps * bpp)
    sel_steps = jnp.broadcast_to(sel_steps, (batch, n_kv, gqa, dec_seq, n_steps, pps * bpp))
    sel_steps = sel_steps.transpose(0, 4, 1, 2, 3, 5).reshape(batch, n_steps, n_kv * rph, pps * bpp)
    sel_steps = jnp.pad(sel_steps, ((0, 0), (0, 0), (0, 0), (0, 128 - pps * bpp)))

    def new_rows(kind):
        rows = kv[:, kind * d_kv:(kind + 1) * d_kv].reshape(batch, dec_seq, d_kv)
        return jnp.pad(rows, ((0, 0), (0, 128 - dec_seq), (0, 0)))

    gates_s = gates[:, :3 * n_heads].reshape(batch, dec_seq, 3, n_kv, gqa).transpose(0, 3, 4, 1, 2)
    gates_s = jnp.pad(gates_s.reshape(batch, n_kv, rph, 3), ((0, 0), (0, 0), (0, 0), (0, 5)))
    need = sel[:, :, :dec_seq, :n_pages * bpp].reshape(batch, n_kv, dec_seq, n_pages, bpp).max(axis=(2, 4))
    need = (need.transpose(0, 2, 1) > 0.5).astype(jnp.int32).reshape(batch, n_pages * n_kv)
    b_rows = _slc_sample(page_table, need, cache_rows, q_pad, sel_steps, new_rows(2), new_rows(3), win_rows,
                         new_rows(4), new_rows(5), o_c, gates_s, layer=l, n_kv=n_kv, gqa=gqa,
                         dec_seq=dec_seq, pages_per_step=pps, name="slc_s")
    b_out = b_rows.reshape(batch, n_kv, gqa, dec_seq, HEAD_DIM).transpose(0, 3, 1, 2, 4)
    b_out = b_out.reshape(m, d_b).astype(BF16)

    x1, x1b, wb_o = _mix_and_norm(x, a_out, b_out, ws, None, l, p["ln_g"], p["ln_b"], alpha, "s")
    wb.update(wb_o)
    st = conv_state
    zero = jnp.zeros((batch, dec_seq - 1, d_ff), F32)
    h1 = jnp.concatenate([st[:, 1:2], zero], axis=1).reshape(m, d_ff)
    h2 = jnp.concatenate([st, zero[:, 1:]], axis=1).reshape(m, d_ff)
    act, gate, wb["f_gate"], wb["f_up"] = _ffn_in_short(
        x1b, ws["f_in"], p["ffn_conv_w"], ws["f_cb"], h1, h2, layer=l, seq_len=dec_seq, name="ffn_in_s")
    z, wb["f_down"] = _matmul([act], ws["f_down"], layer=l, tn=256, emit=True, residual=x1,
                              res_scale=alpha, name="ffn_down_s")
    x2, x2b = _layer_norm(z, p["ln_g"][l, 1], p["ln_b"][l, 1], name="ln2_s")
    new_nsa = nsa4.reshape(batch, dec_seq, 4, n_kv, HEAD_DIM)
    new_win = win4.reshape(batch, dec_seq, 2, n_kv, HEAD_DIM)
    conv_new = gate.reshape(batch, dec_seq, d_ff)[:, dec_seq - (CONV_W - 1):]
    return x2, x2b, new_nsa, new_win, v_rows.reshape(batch, dec_seq, d_a), conv_new, wb


def kernel(x_prompt, x_sample, cache_nsa_kv, cache_win_kv, state_ffn_conv, page_table, w_in, sgu_w,
           sgu_b, sgu_g, cmp_pe, cmp_w1, cmp_w2, w_o, ln_g, ln_b, ffn_w_in, ffn_conv_w, ffn_conv_b,
           ffn_w_down):
    bp, seq, d_model = x_prompt.shape
    bs, dec_seq, _ = x_sample.shape
    depth = w_in.shape[0]
    n_kv = cache_nsa_kv.shape[4]
    page = cache_nsa_kv.shape[2]
    past = page_table.shape[1] * page
    d_a = d_model // 2
    d_b = d_model - d_a
    n_heads = d_b // HEAD_DIM
    d_kv = n_kv * HEAD_DIM
    d_ff = ffn_conv_w.shape[-1]
    dims = (d_a, d_b, d_kv, n_heads, d_ff)
    alpha = (2 * depth) ** 0.25
    assert dec_seq >= CONV_W - 1 and seq % CHUNK == 0

    cos_p, sin_p = _rope_tables(jnp.tile(jnp.arange(seq, dtype=jnp.int32), bp))
    cos_s, sin_s = _rope_tables(jnp.tile(past + jnp.arange(dec_seq, dtype=jnp.int32), bs))
    cache_rows = cache_nsa_kv.reshape(depth, cache_nsa_kv.shape[1], page, 4 * n_kv, HEAD_DIM)
    win_rows = cache_win_kv.reshape(depth, bs, cache_win_kv.shape[2], 2 * n_kv, HEAD_DIM)
    ws = _prep_weights(w_in, ffn_conv_b, 3 * n_heads)
    ws.update(w_in_t=jnp.swapaxes(w_in, 1, 2), w_o=w_o, f_in=ffn_w_in, f_down=ffn_w_down)
    p = dict(sgu_w=sgu_w, sgu_b=sgu_b, sgu_g=sgu_g, ln_g=ln_g, ln_b=ln_b, ffn_conv_w=ffn_conv_w)

    xp = x_prompt.reshape(bp * seq, d_model)
    xs = x_sample.reshape(bs * dec_seq, d_model)
    xp_bf, xs_bf = xp.astype(BF16), xs.astype(BF16)
    outs = [[] for _ in range(7)]
    for l in range(depth):
        wl = _prep_compress(l, cmp_pe, cmp_w1, cmp_w2)
        xs, xs_bf, nsa_s, win_s, v_s, conv_s, wb = _layer_sample(
            xs, xs_bf, ws, wl, l, p, cos_s, sin_s, dims, bs, dec_seq, n_kv, alpha, page_table,
            cache_rows, win_rows, state_ffn_conv[l])
        xp, xp_bf, nsa_p, win_p, conv_p = _layer_prompt(
            xp, xp_bf, ws, wb, wl, l, p, cos_p, sin_p, dims, bp, seq, n_kv, alpha)
        for acc, val in zip(outs, (nsa_p, nsa_s, win_p, win_s, v_s, conv_p, conv_s)):
            acc.append(val)
    return (xp.reshape(bp, seq, d_model), xs.reshape(bs, dec_seq, d_model),
            *[jnp.stack(o) for o in outs])
```

```python
import functools
import math

import jax
import jax.numpy as jnp
import numpy as np
from jax import lax
from jax.experimental import pallas as pl
from jax.experimental.pallas import tpu as pltpu

HEAD_DIM = 128
CHUNK = 128
STRIDE = 16
L_CMP = 2 * STRIDE
L_SLC = 64
N_SEL = 16
WINDOW = 512
ROT_DIM = HEAD_DIM // 4
ROT_HALF = ROT_DIM // 2
ROPE_THETA = 500000.0
CONV_W = 3
LN_EPS = 1e-5
SCALE = HEAD_DIM ** -0.5
LOGIT_SCALE = SCALE * math.log2(math.e)
SEL_FORCE = 1e9
NEG_BIG = -3.0e38
VMEM_LIMIT = 56 * 1024 * 1024

F32 = jnp.float32
BF16 = jnp.bfloat16


def _cparams(*sem):
    return pltpu.CompilerParams(dimension_semantics=sem, vmem_limit_bytes=VMEM_LIMIT)


def _tile(n, pref, unit=128):
    if n <= pref:
        return n
    t = (pref // unit) * unit
    while t > unit and n % t:
        t -= unit
    assert n % t == 0, (n, pref, unit)
    return t


def _gelu(x):
    return jax.nn.gelu(x, approximate=True)


def _dot(a, b):
    return jnp.dot(a, b, preferred_element_type=F32)


def _dot_nt(a, b):
    return lax.dot_general(a, b, (((1,), (1,)), ((), ())), preferred_element_type=F32)


def _split_hi_lo(x):
    hi = x.astype(BF16)
    lo = (x - hi.astype(F32)).astype(BF16)
    return hi, lo


def _masked_softmax2(s, mask, axis=-1):
    sm = jnp.where(mask, s, NEG_BIG)
    m = jnp.max(sm, axis=axis, keepdims=True)
    p = jnp.where(mask, jnp.exp2(sm - m), 0.0)
    return p / jnp.maximum(jnp.sum(p, axis=axis, keepdims=True), 1e-30)


def _mm_kernel(*refs, n_lhs, epilogue, emit, res_scale, w_transposed):
    n_in = 2 * n_lhs + (res_scale is not None)
    ws = [refs[n_lhs + k][...] for k in range(n_lhs)]
    if w_transposed:
        ws = [w.T for w in ws]
    if emit:
        ws = [w.astype(BF16) for w in ws]
        for k in range(n_lhs):
            refs[n_in + 1 + k][...] = ws[k]
    acc = _dot(refs[0][...], ws[0])
    for k in range(1, n_lhs):
        acc = acc + _dot(refs[k][...], ws[k])
    if res_scale is not None:
        acc = res_scale * refs[n_in - 1][...] + acc
    o_ref = refs[n_in]
    if epilogue == "gelu":
        acc = _gelu(acc)
    elif epilogue == "sigmoid":
        acc = jax.nn.sigmoid(acc)
    o_ref[...] = acc.astype(o_ref.dtype)


def _matmul(xs, w, *, layer, col_start=0, n=None, epilogue="none", out_dtype=F32, tm=512, tn=512,
            emit=False, residual=None, res_scale=None, w_transposed=False, name):
    m, kdim = xs[0].shape
    n = (w[0] if isinstance(w, (list, tuple)) else w).shape[1 if w_transposed else 2] if n is None else n
    tm = _tile(m, tm, 8)
    tn = _tile(n, tn)
    assert col_start % tn == 0 and all(x.shape == (m, kdim) for x in xs)
    c0 = col_start // tn
    n_lhs = len(xs)
    in_specs = [pl.BlockSpec((tm, kdim), lambda i, j: (i, 0)) for _ in xs]
    separate = isinstance(w, (list, tuple))
    w_list = list(w) if separate else [w] * n_lhs
    if w_transposed:
        assert n_lhs == 1
        in_specs += [pl.BlockSpec((None, tn, kdim), lambda i, j: (layer, c0 + j, 0))]
    else:
        in_specs += [pl.BlockSpec((None, kdim, tn), lambda i, j, r=r: (layer, 0 if separate else r, c0 + j))
                     for r in range(n_lhs)]
    extra = []
    if residual is not None:
        in_specs.append(pl.BlockSpec((tm, tn), lambda i, j: (i, j)))
        extra = [residual]
    out_shape = [jax.ShapeDtypeStruct((m, n), out_dtype)]
    out_specs = [pl.BlockSpec((tm, tn), lambda i, j: (i, j))]
    if emit:
        assert m == tm
        out_shape += [jax.ShapeDtypeStruct((1, kdim, n), BF16)] * n_lhs
        out_specs += [pl.BlockSpec((None, kdim, tn), lambda i, j: (0, 0, j))] * n_lhs
    out = pl.pallas_call(
        functools.partial(_mm_kernel, n_lhs=n_lhs, epilogue=epilogue, emit=emit,
                          res_scale=res_scale if residual is not None else None,
                          w_transposed=w_transposed),
        out_shape=tuple(out_shape),
        grid=(m // tm, n // tn),
        in_specs=in_specs,
        out_specs=tuple(out_specs),
        compiler_params=_cparams("parallel", "arbitrary"),
        name=name,
    )(*xs, *w_list, *extra)
    return out if emit else out[0]


def _rope_mm_kernel(x_ref, w_ref, cos_ref, sin_ref, o_ref, *more_refs, heads_per_tile, alternate,
                    n_nsa_kinds, emit):
    j = pl.program_id(1)
    w = w_ref[...]
    cache_refs = more_refs
    if emit:
        w = w.T.astype(BF16)
        more_refs[-1][...] = w
        cache_refs = more_refs[:-1]
    acc = _dot(x_ref[...], w)
    cosv, sinv = cos_ref[...], sin_ref[...]
    if alternate:
        rot = lax.rem(j, 2) == 0
        cosv = jnp.where(rot, cosv, 1.0)
        sinv = jnp.where(rot, sinv, 0.0)
    lane = lax.broadcasted_iota(jnp.int32, cosv.shape, 1)
    heads = []
    for h in range(heads_per_tile):
        hs = acc[:, h * HEAD_DIM:(h + 1) * HEAD_DIM]
        partner = jnp.where(lane < ROT_HALF,
                            pltpu.roll(hs, HEAD_DIM - ROT_HALF, 1),
                            pltpu.roll(hs, ROT_HALF, 1))
        heads.append(hs * cosv + partner * sinv)
        o_ref[:, h * HEAD_DIM:(h + 1) * HEAD_DIM] = heads[h]
    if cache_refs:
        nsa_ref, win_ref = cache_refs

        @pl.when(j < n_nsa_kinds)
        def _():
            for h in range(heads_per_tile):
                nsa_ref[:, h, :] = heads[h]

        @pl.when(j >= n_nsa_kinds)
        def _():
            for h in range(heads_per_tile):
                win_ref[:, h, :] = heads[h]


def _rope_matmul(x, w, cos_t, sin_t, *, layer, col_start, n, tn, alternate, n_kv=None, tm=1024,
                 emit=False, name):
    m, k = x.shape
    tm = _tile(m, tm, 8)
    assert col_start % tn == 0 and n % tn == 0
    c0 = col_start // tn
    hpt = tn // HEAD_DIM
    out_shape = [jax.ShapeDtypeStruct((m, n), F32)]
    out_specs = [pl.BlockSpec((tm, tn), lambda i, j: (i, j))]
    n_nsa = 4
    if alternate:
        assert hpt == n_kv and n == 6 * tn
        out_shape += [jax.ShapeDtypeStruct((m, n_nsa, n_kv, HEAD_DIM), F32),
                      jax.ShapeDtypeStruct((m, 2, n_kv, HEAD_DIM), F32)]
        out_specs += [pl.BlockSpec((tm, None, n_kv, HEAD_DIM),
                                   lambda i, j: (i, jnp.minimum(j, n_nsa - 1), 0, 0)),
                      pl.BlockSpec((tm, None, n_kv, HEAD_DIM),
                                   lambda i, j: (i, jnp.maximum(j - n_nsa, 0), 0, 0))]
    if emit:
        assert m == tm
        out_shape += [jax.ShapeDtypeStruct((1, k, n), BF16)]
        out_specs += [pl.BlockSpec((None, k, tn), lambda i, j: (0, 0, j))]
    kern = functools.partial(_rope_mm_kernel, heads_per_tile=hpt, alternate=alternate,
                             n_nsa_kinds=n_nsa, emit=emit)
    return pl.pallas_call(
        kern,
        out_shape=tuple(out_shape),
        grid=(m // tm, n // tn),
        in_specs=[pl.BlockSpec((tm, k), lambda i, j: (i, 0)),
                  (pl.BlockSpec((None, tn, k), lambda i, j: (layer, c0 + j, 0)) if emit else
                   pl.BlockSpec((None, k, tn), lambda i, j: (layer, 0, c0 + j))),
                  pl.BlockSpec((tm, HEAD_DIM), lambda i, j: (i, 0)),
                  pl.BlockSpec((tm, HEAD_DIM), lambda i, j: (i, 0))],
        out_specs=tuple(out_specs),
        compiler_params=_cparams("arbitrary", "arbitrary"),
        name=name,
    )(x, w, cos_t, sin_t)


def _ln_kernel(z_ref, g_ref, b_ref, y_ref, yb_ref):
    z = z_ref[...]
    mu = jnp.mean(z, axis=-1, keepdims=True)
    zc = z - mu
    var = jnp.mean(zc * zc, axis=-1, keepdims=True)
    y = zc * lax.rsqrt(var + LN_EPS) * g_ref[...] + b_ref[...]
    y_ref[...] = y
    yb_ref[...] = y.astype(BF16)


def _layer_norm(z, g, b, *, name):
    m, d = z.shape
    tr = _tile(m, 256, 8)
    return pl.pallas_call(
        _ln_kernel,
        out_shape=(jax.ShapeDtypeStruct((m, d), F32), jax.ShapeDtypeStruct((m, d), BF16)),
        grid=(m // tr,),
        in_specs=[pl.BlockSpec((tr, d), lambda i: (i, 0)),
                  pl.BlockSpec((1, d), lambda i: (0, 0)),
                  pl.BlockSpec((1, d), lambda i: (0, 0))],
        out_specs=(pl.BlockSpec((tr, d), lambda i: (i, 0)),
                   pl.BlockSpec((tr, d), lambda i: (i, 0))),
        compiler_params=_cparams("parallel"),
        name=name,
    )(z, g.reshape(1, d), b.reshape(1, d))


def _sgu_kernel(uv_ref, w_ref, bt_ref, g_ref, a_ref, *vn_refs, d_a, n_groups):
    rows = w_ref.shape[1]
    r = lax.broadcasted_iota(jnp.int32, (rows, rows), 0)
    c = lax.broadcasted_iota(jnp.int32, (rows, rows), 1)
    causal = r >= c
    for g in range(n_groups):
        lo = g * HEAD_DIM
        v = uv_ref[:, d_a + lo:d_a + lo + HEAD_DIM]
        mu = jnp.mean(v, axis=-1, keepdims=True)
        vc = v - mu
        var = jnp.mean(vc * vc, axis=-1, keepdims=True)
        vn = vc * lax.rsqrt(var + LN_EPS) * g_ref[:, lo:lo + HEAD_DIM]
        if vn_refs:
            vn_refs[0][:, lo:lo + HEAD_DIM] = vn
        w = jnp.where(causal, w_ref[g], 0.0).astype(BF16)
        mixed = _dot(w, vn.astype(BF16)) + bt_ref[:, g:g + 1]
        a_ref[:, lo:lo + HEAD_DIM] = (uv_ref[:, lo:lo + HEAD_DIM] * mixed).astype(a_ref.dtype)


def _sgu(uv, w, bt, gain, *, rows, with_vn, name):
    m = uv.shape[0]
    d_a = uv.shape[1] // 2
    n_groups = d_a // HEAD_DIM
    n_out = 2 if with_vn else 1
    return pl.pallas_call(
        functools.partial(_sgu_kernel, d_a=d_a, n_groups=n_groups),
        out_shape=(jax.ShapeDtypeStruct((m, d_a), BF16), jax.ShapeDtypeStruct((m, d_a), F32))[:n_out],
        grid=(m // rows,),
        in_specs=[pl.BlockSpec((rows, 2 * d_a), lambda i: (i, 0)),
                  pl.BlockSpec((n_groups, rows, rows), lambda i: (0, 0, 0)),
                  pl.BlockSpec((rows, n_groups), lambda i: (0, 0)),
                  pl.BlockSpec((1, d_a), lambda i: (0, 0))],
        out_specs=(pl.BlockSpec((rows, d_a), lambda i: (i, 0)),
                   pl.BlockSpec((rows, d_a), lambda i: (i, 0)))[:n_out],
        compiler_params=_cparams("parallel"),
        name=name,
    )(uv, w, bt, gain.reshape(1, d_a))


def _conv_act(gate, g1, g2, up, cw_ref, cb_ref):
    c = cb_ref[...] + g2 * cw_ref[0:1, :] + g1 * cw_ref[1:2, :] + gate * cw_ref[2:3, :]
    return _gelu(c) * up


def _ffn_in_seq_kernel(x_ref, *refs, tiles_per_seq, n_sub):
    wg, wu, cw, cb, st = (refs[k * n_sub:(k + 1) * n_sub] for k in range(5))
    a_ref, cn_ref, carry_ref = refs[5 * n_sub:]
    i = pl.program_id(1)
    x = x_ref[...]
    tm = x.shape[0]
    tn = wg[0].shape[1]
    first = lax.rem(i, tiles_per_seq) == 0
    row = lax.broadcasted_iota(jnp.int32, (tm, tn), 0)
    for k in range(n_sub):
        cols = slice(k * tn, (k + 1) * tn)
        gate = _dot(x, wg[k][...])
        up = _dot(x, wu[k][...])
        prev2 = jnp.where(first, st[k][0:1, :], carry_ref[0:1, cols])
        prev1 = jnp.where(first, st[k][1:2, :], carry_ref[1:2, cols])
        g1 = jnp.where(row == 0, prev1, pltpu.roll(gate, 1, 0))
        g2 = jnp.where(row == 0, prev2, jnp.where(row == 1, prev1, pltpu.roll(gate, 2, 0)))
        a_ref[:, cols] = _conv_act(gate, g1, g2, up, cw[k], cb[k]).astype(a_ref.dtype)
        tail = gate[tm - 2:tm, :]
        carry_ref[0:2, cols] = tail
        cn_ref[:, cols] = tail


def _ffn_in_seq(x, wg, wu, cw, cb, state, *, layer, seq_len, tm=1024, tn=256, n_sub=2, name):
    m, k = x.shape
    d_ff = cw.shape[2]
    tm = _tile(seq_len, tm, 8)
    tn = _tile(d_ff, tn)
    n_tiles = d_ff // tn
    tps = seq_len // tm
    n_seq = m // seq_len

    def col(j, s):
        return jnp.minimum(j * n_sub + s, n_tiles - 1)

    subs = range(n_sub)
    in_specs = [pl.BlockSpec((tm, k), lambda j, i: (i, 0))]
    in_specs += [pl.BlockSpec((None, k, tn), lambda j, i, s=s: (0, 0, col(j, s))) for s in subs] * 2
    in_specs += [pl.BlockSpec((None, CONV_W, tn), lambda j, i, s=s: (layer, 0, col(j, s))) for s in subs]
    in_specs += [pl.BlockSpec((None, 1, tn), lambda j, i, s=s: (layer, 0, col(j, s))) for s in subs]
    in_specs += [pl.BlockSpec((None, CONV_W - 1, tn), lambda j, i, s=s: (i // tps, 0, col(j, s)))
                 for s in subs]
    return pl.pallas_call(
        functools.partial(_ffn_in_seq_kernel, tiles_per_seq=tps, n_sub=n_sub),
        out_shape=(jax.ShapeDtypeStruct((m, d_ff), BF16),
                   jax.ShapeDtypeStruct((n_seq, CONV_W - 1, d_ff), F32)),
        grid=(pl.cdiv(n_tiles, n_sub), m // tm),
        in_specs=in_specs,
        out_specs=(pl.BlockSpec((tm, n_sub * tn), lambda j, i: (i, j)),
                   pl.BlockSpec((None, CONV_W - 1, n_sub * tn), lambda j, i: (i // tps, 0, j))),
        scratch_shapes=[pltpu.VMEM((8, n_sub * tn), F32)],
        compiler_params=_cparams("arbitrary", "arbitrary"),
        name=name,
    )(x, *([wg] * n_sub), *([wu] * n_sub), *([cw] * n_sub), *([cb] * n_sub), *([state] * n_sub))


def _ffn_in_short_kernel(x_ref, wg_ref, wu_ref, cw_ref, cb_ref, h1_ref, h2_ref, a_ref, gate_ref,
                         wgb_ref, wub_ref, *, seq_len):
    x = x_ref[...]
    wg = wg_ref[...].astype(BF16)
    wu = wu_ref[...].astype(BF16)
    wgb_ref[...] = wg
    wub_ref[...] = wu
    gate = _dot(x, wg)
    up = _dot(x, wu)
    t = lax.rem(lax.broadcasted_iota(jnp.int32, gate.shape, 0), seq_len)
    g1 = jnp.where(t >= 1, pltpu.roll(gate, 1, 0), h1_ref[...])
    g2 = jnp.where(t >= 2, pltpu.roll(gate, 2, 0), h2_ref[...])
    a_ref[...] = _conv_act(gate, g1, g2, up, cw_ref, cb_ref).astype(a_ref.dtype)
    gate_ref[...] = gate


def _ffn_in_short(x, w, cw, cb, h1, h2, *, layer, seq_len, tn=256, name):
    m, k = x.shape
    d_ff = cw.shape[2]
    tn = _tile(d_ff, tn)
    n_tiles = d_ff // tn
    return pl.pallas_call(
        functools.partial(_ffn_in_short_kernel, seq_len=seq_len),
        out_shape=(jax.ShapeDtypeStruct((m, d_ff), BF16), jax.ShapeDtypeStruct((m, d_ff), F32),
                   jax.ShapeDtypeStruct((1, k, d_ff), BF16), jax.ShapeDtypeStruct((1, k, d_ff), BF16)),
        grid=(n_tiles,),
        in_specs=[pl.BlockSpec((m, k), lambda j: (0, 0)),
                  pl.BlockSpec((None, k, tn), lambda j: (layer, 0, j)),
                  pl.BlockSpec((None, k, tn), lambda j: (layer, 0, n_tiles + j)),
                  pl.BlockSpec((None, CONV_W, tn), lambda j: (layer, 0, j)),
                  pl.BlockSpec((None, 1, tn), lambda j: (layer, 0, j)),
                  pl.BlockSpec((m, tn), lambda j: (0, j)),
                  pl.BlockSpec((m, tn), lambda j: (0, j))],
        out_specs=(pl.BlockSpec((m, tn), lambda j: (0, j)),
                   pl.BlockSpec((m, tn), lambda j: (0, j)),
                   pl.BlockSpec((None, k, tn), lambda j: (0, 0, j)),
                   pl.BlockSpec((None, k, tn), lambda j: (0, 0, j))),
        compiler_params=_cparams("parallel"),
        name=name,
    )(x, w, w, cw, cb, h1, h2)


def _compress_rows(xa, w1a, w1b, w2, hpe):
    a = _dot(xa, w1a)
    b = _dot(xa, w1b)
    n = a.shape[0]
    h = a + pltpu.roll(b, n - 1, 0) + hpe
    return _dot(_gelu(h).astype(BF16), w2)


def _pos_embed_term(pe_ref, w1_ref, kind):
    return _dot(pe_ref[kind], w1_ref[kind])[0:1, :]


def _compress_p_kernel(k_ref, v_ref, w1_ref, w2_ref, pe_ref, kc_ref, vc_ref, *, n_chunks):
    half = STRIDE * HEAD_DIM
    for kind, (src, dst) in enumerate(((k_ref, kc_ref), (v_ref, vc_ref))):
        xa = jnp.concatenate(
            [src[pl.ds(s, n_chunks, stride=STRIDE), :] for s in range(STRIDE)], axis=1).astype(BF16)
        hpe = _pos_embed_term(pe_ref, w1_ref, kind)
        out = _compress_rows(xa, w1_ref[kind, 0:half, :], w1_ref[kind, half:2 * half, :],
                             w2_ref[kind], hpe)
        dst[...] = out.astype(dst.dtype)


def _compress_prompt(kv, w1, w2, pe, *, batch, seq, n_kv, name):
    n_chunks = seq // STRIDE
    out = jax.ShapeDtypeStruct((batch, n_kv, n_chunks, HEAD_DIM), BF16)
    ospec = pl.BlockSpec((None, None, n_chunks, HEAD_DIM), lambda b, h: (b, h, 0, 0))
    return pl.pallas_call(
        functools.partial(_compress_p_kernel, n_chunks=n_chunks),
        out_shape=(out, out),
        grid=(batch, n_kv),
        in_specs=[pl.BlockSpec((seq, HEAD_DIM), lambda b, h: (b, h)),
                  pl.BlockSpec((seq, HEAD_DIM), lambda b, h: (b, n_kv + h)),
                  pl.BlockSpec(w1.shape, lambda b, h: (0, 0, 0)),
                  pl.BlockSpec(w2.shape, lambda b, h: (0, 0, 0)),
                  pl.BlockSpec(pe.shape, lambda b, h: (0, 0, 0))],
        out_specs=(ospec, ospec),
        compiler_params=_cparams("parallel", "parallel"),
        name=name,
    )(kv, kv, w1, w2, pe)


def _block_scores(imp, pos, n_blocks, block_axis):
    j = lax.broadcasted_iota(jnp.int32, imp.shape, block_axis)
    cur = pos // L_SLC
    forced = (j == 0) | (j == cur) | (j == cur - 1)
    valid = j * L_SLC <= pos
    score = jnp.where(valid, jnp.where(forced, SEL_FORCE, imp), -SEL_FORCE)
    return jnp.where(j < n_blocks, score, NEG_BIG)


def _overlap_matrix(nc, nb, rows, cols):
    i = np.arange(nc)[:, None]
    j = np.arange(nb)[None, :]
    lo = np.maximum(i * STRIDE, j * L_SLC)
    hi = np.minimum(i * STRIDE + L_CMP, (j + 1) * L_SLC)
    ov = np.zeros((rows, cols), np.float32)
    ov[:nc, :nb] = np.maximum(hi - lo, 0) / STRIDE
    return ov


def _block_to_key_matrix(n_groups, lanes, keys_per_group):
    ex = np.zeros((n_groups, lanes, keys_per_group), np.float32)
    for c in range(n_groups):
        k = np.arange(keys_per_group)
        ex[c, (c * keys_per_group + k) // L_SLC, k] = 1.0
    return ex


def _attn_p_kernel(q_ref, kc_ref, vc_ref, ks_ref, vs_ref, kw_ref, vw_ref, gt_ref, ovt_ref, ex_ref,
                   o_ref, m_scr, acc_scr, *, tq, tk, gqa, hpb, n_blocks, n_cmp, win_keys):
    qi = pl.program_id(2)
    t0 = qi * tq
    q = q_ref[...] * LOGIT_SCALE
    qs = [[q[:, (hh * gqa + g) * HEAD_DIM:(hh * gqa + g + 1) * HEAD_DIM].astype(BF16)
           for g in range(gqa)] for hh in range(hpb)]
    pos_t = t0 + lax.broadcasted_iota(jnp.int32, (tq, 1), 0)
    pos_row = t0 + lax.broadcasted_iota(jnp.int32, (1, tq), 1)
    ones = jnp.ones((max(tk, win_keys), HEAD_DIM), BF16)
    lanes = ex_ref.shape[1]

    n = lax.broadcasted_iota(jnp.int32, (tq, kc_ref.shape[1]), 1)
    last_end = jnp.minimum(pos_t, (n_cmp - 1) * STRIDE + (L_CMP - 1))
    bias_c = jnp.where(n * STRIDE + (L_CMP - 1) <= last_end, 0.0, NEG_BIG)
    o_c, sel = [], []
    for hh in range(hpb):
        kc = kc_ref[hh]
        vc = vc_ref[hh]
        p_grp = None
        o_c.append([])
        for g in range(gqa):
            sm = _dot_nt(qs[hh][g], kc) + bias_c
            m = jnp.max(sm, axis=-1, keepdims=True)
            e = jnp.exp2(sm - m)
            norm = jnp.where(m > 0.5 * NEG_BIG,
                             1.0 / jnp.maximum(jnp.sum(e, axis=-1, keepdims=True), 1e-30), 0.0)
            p = e * norm
            o_c[hh].append(_dot(p.astype(BF16), vc))
            p_grp = p if p_grp is None else p_grp + p

        hi, lo = _split_hi_lo(p_grp)
        imp_t = _dot_nt(ovt_ref[...], hi) + _dot_nt(ovt_ref[...], lo)
        score = _block_scores(imp_t, pos_row, n_blocks, 0)
        blk = lax.broadcasted_iota(jnp.int32, score.shape, 0)
        rank = jnp.zeros(score.shape, F32)
        for i in range(n_blocks):
            ci = score[i:i + 1, :]
            rank = rank + jnp.where((ci > score) | ((ci == score) & (blk > i)), 1.0, 0.0)
        sel_t = jnp.where(rank < float(min(N_SEL, n_blocks)), 1.0, 0.0)
        sel_t = jnp.concatenate([sel_t, jnp.zeros((lanes - sel_t.shape[0], tq), F32)], axis=0)
        sel.append(sel_t.T.astype(BF16))

    m_scr[...] = jnp.full(m_scr.shape, NEG_BIG, F32)
    acc_scr[...] = jnp.zeros(acc_scr.shape, F32)

    def key_tile(c, carry):
        k0 = pl.multiple_of(c * tk, tk)
        kpos = k0 + lax.broadcasted_iota(jnp.int32, (tq, tk), 1)
        causal = jnp.where(kpos <= pos_t, 0.5, 2.0)
        for hh in range(hpb):
            cols = slice(hh * HEAD_DIM, (hh + 1) * HEAD_DIM)
            kt = ks_ref[pl.ds(k0, tk), cols].astype(BF16)
            vt = jnp.concatenate([vs_ref[pl.ds(k0, tk), cols].astype(BF16), ones[0:tk]], axis=1)
            bias = jnp.where(_dot(sel[hh], ex_ref[c]) > causal, 0.0, NEG_BIG)
            for g in range(gqa):
                rows = slice((hh * gqa + g) * tq, (hh * gqa + g + 1) * tq)
                sm = _dot_nt(qs[hh][g], kt) + bias
                m_old = m_scr[rows]
                m_new = jnp.maximum(m_old, jnp.max(sm, axis=-1, keepdims=True))
                e = jnp.exp2(sm - m_new)
                acc_scr[rows] = jnp.exp2(m_old - m_new) * acc_scr[rows] + _dot(e.astype(BF16), vt)
                m_scr[rows] = m_new
        return carry

    lax.fori_loop(0, (t0 + tq - 1) // tk + 1, key_tile, 0)

    start = pl.multiple_of(jnp.maximum(t0 + tq - win_keys, 0), 128)
    d = pos_t - (start + lax.broadcasted_iota(jnp.int32, (tq, win_keys), 1))
    bias_w = jnp.where(d >= 0, jnp.where(d < WINDOW, 0.0, NEG_BIG), NEG_BIG)
    for hh in range(hpb):
        cols = slice(hh * HEAD_DIM, (hh + 1) * HEAD_DIM)
        kw = kw_ref[pl.ds(start, win_keys), cols].astype(BF16)
        vw = jnp.concatenate([vw_ref[pl.ds(start, win_keys), cols].astype(BF16), ones[0:win_keys]], axis=1)
        gt = gt_ref[hh]
        for g in range(gqa):
            sm = _dot_nt(qs[hh][g], kw) + bias_w
            e = jnp.exp2(sm - jnp.max(sm, axis=-1, keepdims=True))
            ow = _dot(e.astype(BF16), vw)
            o_w = ow[:, 0:HEAD_DIM] / ow[:, HEAD_DIM:HEAD_DIM + 1]
            acc = acc_scr[(hh * gqa + g) * tq:(hh * gqa + g + 1) * tq]
            o_s = acc[:, 0:HEAD_DIM] / jnp.maximum(acc[:, HEAD_DIM:HEAD_DIM + 1], 1e-30)
            out = (gt[:, g:g + 1] * o_c[hh][g] + gt[:, gqa + g:gqa + g + 1] * o_s
                   + gt[:, 2 * gqa + g:2 * gqa + g + 1] * o_w)
            o_ref[:, (hh * gqa + g) * HEAD_DIM:(hh * gqa + g + 1) * HEAD_DIM] = out.astype(o_ref.dtype)


def _attn_prompt(q2d, kv, kcmp, vcmp, gates_h, *, batch, seq, n_heads, n_kv, tq=256, tk=1024, hpb=2,
                 name):
    gqa = n_heads // n_kv
    tq = _tile(seq, tq, 8)
    tk = _tile(seq, tk)
    hpb = _tile(n_kv, hpb, 1)
    n_grp = n_kv // hpb
    nq = seq // tq
    n_chunks = seq // STRIDE
    n_cmp = n_chunks - 1
    n_blocks = -(-seq // L_SLC)
    lanes = -(-n_blocks // 128) * 128
    block_rows = -(-n_blocks // 8) * 8
    ovt = jnp.asarray(_overlap_matrix(n_cmp, n_blocks, n_chunks, block_rows).T.copy(), BF16)
    ex = jnp.asarray(_block_to_key_matrix(seq // tk, lanes, tk), BF16)
    win_keys = min(WINDOW + tq, seq)
    rows = hpb * gqa * tq
    kern = functools.partial(_attn_p_kernel, tq=tq, tk=tk, gqa=gqa, hpb=hpb, n_blocks=n_blocks,
                             n_cmp=n_cmp, win_keys=win_keys)

    def kv_spec(kind):
        return pl.BlockSpec((seq, hpb * HEAD_DIM), lambda b, h, i: (b, kind * n_grp + h))

    cmp_spec = pl.BlockSpec((None, hpb, n_chunks, HEAD_DIM), lambda b, h, i: (b, h, 0, 0))
    return pl.pallas_call(
        kern,
        out_shape=jax.ShapeDtypeStruct((batch * seq, n_heads * HEAD_DIM), BF16),
        grid=(batch, n_grp, nq),
        in_specs=[pl.BlockSpec((tq, hpb * gqa * HEAD_DIM), lambda b, h, i: (b * nq + i, h)),
                  cmp_spec, cmp_spec,
                  kv_spec(2), kv_spec(3), kv_spec(4), kv_spec(5),
                  pl.BlockSpec((hpb, tq, 3 * gqa), lambda b, h, i: (h, b * nq + i, 0)),
                  pl.BlockSpec(ovt.shape, lambda b, h, i: (0, 0)),
                  pl.BlockSpec(ex.shape, lambda b, h, i: (0, 0, 0))],
        out_specs=pl.BlockSpec((tq, hpb * gqa * HEAD_DIM), lambda b, h, i: (b * nq + i, h)),
        scratch_shapes=[pltpu.VMEM((rows, 1), F32),
                        pltpu.VMEM((rows, 2 * HEAD_DIM), F32)],
        compiler_params=_cparams("parallel", "parallel", "arbitrary"),
        name=name,
    )(q2d, kcmp, vcmp, kv, kv, kv, kv, gates_h, ovt, ex)


def _head_slabs(rows_ref, first, n_heads):
    return jnp.concatenate([rows_ref[:, first + h, :] for h in range(n_heads)], axis=1)


def _page_copies(cache_ref, buf_ref, sem_ref, pt_ref, b, step, slot, *, layer, row0, n_rows,
                 pages_per_step, need_ref=None, n_heads=None):
    groups = []
    for k in range(pages_per_step):
        page = pt_ref[b, step * pages_per_step + k]
        copies = [pltpu.make_async_copy(cache_ref.at[layer, page, :, row0 + r, :],
                                        buf_ref.at[slot, k, r], sem_ref.at[slot])
                  for r in range(n_rows)]
        if need_ref is None:
            groups.append((None, copies))
        else:
            for h in range(n_heads):
                flag = need_ref[b, (step * pages_per_step + k) * n_heads + h] != 0
                groups.append((flag, copies[h::n_heads]))
    return groups


def _page_stream(cache_ref, buf_ref, sem_ref, pt_ref, **kw):
    b, p = pl.program_id(0), pl.program_id(1)
    n_b, n_p = pl.num_programs(0), pl.num_programs(1)
    g = b * n_p + p
    n_buf = buf_ref.shape[0]
    ahead = n_buf - 1
    slot = lax.rem(g, n_buf)

    def for_each(groups, fn):
        n = 0
        for flag, copies in groups:
            def run(copies=copies, n=n):
                for i, c in enumerate(copies):
                    fn(c, n + i)
            if flag is None:
                run()
            else:
                pl.when(flag)(run)
            n += len(copies)

    def start(c, n):
        c.start(priority=n % 2)

    def start_step(gs):
        for_each(_page_copies(cache_ref, buf_ref, sem_ref, pt_ref, gs // n_p, lax.rem(gs, n_p),
                              lax.rem(gs, n_buf), **kw), start)

    @pl.when(g == 0)
    def _():
        if kw.get("need_ref") is not None:
            buf_ref[...] = jnp.zeros(buf_ref.shape, buf_ref.dtype)
        for k in range(ahead):
            pl.when(k < n_b * n_p)(functools.partial(start_step, jnp.int32(k)))

    @pl.when(g + ahead < n_b * n_p)
    def _():
        start_step(g + ahead)

    for_each(_page_copies(cache_ref, buf_ref, sem_ref, pt_ref, b, p, slot, **kw), lambda c, n: c.wait())
    return slot


def _cmp_s_kernel(pt_ref, cache_ref, q_ref, w1_ref, w2_ref, pe_ref, ovt_ref, oc_ref, sel_ref,
                  x_scr, ab_scr, buf_ref, sem_ref, *, layer, n_kv, n_steps, pages_per_step,
                  steps_per_group, n_cmp, n_blocks, dec_seq, gqa, past):
    p = pl.program_id(1)
    n_kh = 2 * n_kv
    page = buf_ref.shape[3]
    slot = _page_stream(cache_ref, buf_ref, sem_ref, pt_ref, layer=layer, row0=0, n_rows=n_kh,
                        pages_per_step=pages_per_step)
    cpp = page // STRIDE
    step_chunks = cpp * pages_per_step
    group_chunks = step_chunks * steps_per_group
    c0 = pl.multiple_of(lax.rem(p, steps_per_group) * step_chunks, 8)
    for k in range(pages_per_step):
        for s in range(STRIDE):
            for kh in range(n_kh):
                x_scr[kh, pl.ds(c0 + k * cpp, cpp), s * HEAD_DIM:(s + 1) * HEAD_DIM] = (
                    buf_ref[slot, k, kh, pl.ds(s, cpp, stride=STRIDE), :])

    @pl.when(lax.rem(p, steps_per_group) == steps_per_group - 1)
    def _():
        g0 = pl.multiple_of((p // steps_per_group) * group_chunks, 8)
        for kh in range(n_kh):
            kind = kh // n_kv
            ab_scr[kh, pl.ds(g0, group_chunks), :] = _dot(x_scr[kh].astype(BF16), w1_ref[kind])

    @pl.when(p == n_steps - 1)
    def _():
        n_chunks = ab_scr.shape[1]
        comp = []
        for kh in range(n_kh):
            kind = kh // n_kv
            hpe = _dot(pe_ref[kind], w1_ref[kind])
            hpe = hpe[0:1, 0:HEAD_DIM] + hpe[8:9, HEAD_DIM:2 * HEAD_DIM]
            ab = ab_scr[kh]
            h = ab[:, 0:HEAD_DIM] + pltpu.roll(ab[:, HEAD_DIM:2 * HEAD_DIM], n_chunks - 1, 0) + hpe
            comp.append(_dot(_gelu(h).astype(BF16), w2_ref[kind]).astype(BF16))
        qrows = lax.broadcasted_iota(jnp.int32, (1, HEAD_DIM), 1)
        pos = past + lax.rem(qrows, dec_seq)
        for h in range(n_kv):
            kc, vc = comp[h], comp[n_kv + h]
            st = _dot_nt(kc, (q_ref[h] * LOGIT_SCALE).astype(BF16))
            n = lax.broadcasted_iota(jnp.int32, st.shape, 0)
            pt = _masked_softmax2(st, (n * STRIDE + (L_CMP - 1) <= pos) & (n < n_cmp), axis=0)
            oc_ref[h] = _dot(pt.T.astype(BF16), vc)
            pg = pt
            for g in range(1, gqa):
                pg = pg + pltpu.roll(pt, HEAD_DIM - g * dec_seq, 1)
            hi, lo = _split_hi_lo(pg)
            imp_t = _dot(ovt_ref[...], hi) + _dot(ovt_ref[...], lo)
            imp = imp_t.T
            tpos = past + lax.broadcasted_iota(jnp.int32, (imp.shape[0], 1), 0)
            score = _block_scores(imp, tpos, n_blocks, 1)
            score_t = score.T
            nb_pad = score.shape[1]
            ii = lax.broadcasted_iota(jnp.int32, (nb_pad, nb_pad), 0)
            jj = lax.broadcasted_iota(jnp.int32, (nb_pad, nb_pad), 1)
            for t in range(dec_seq):
                col = score_t[:, t:t + 1]
                rowv = score[t:t + 1, :]
                beats = (col > rowv) | ((col == rowv) & (ii < jj))
                rank = jnp.sum(jnp.where(beats, 1.0, 0.0), axis=0, keepdims=True)
                sel_ref[h, t:t + 1, :] = jnp.where(rank < float(min(N_SEL, n_blocks)), 1.0, 0.0)
            sel_ref[h, dec_seq:, :] = jnp.zeros((sel_ref.shape[1] - dec_seq, nb_pad), F32)


def _cmp_sample(page_table, cache_rows, q_pad, w1cat, w2, pe, *, layer, n_kv, gqa, dec_seq, name):
    batch, n_pages = page_table.shape
    page = cache_rows.shape[2]
    past = n_pages * page
    cpp = page // STRIDE
    n_chunks = past // STRIDE
    assert dec_seq < STRIDE and n_chunks % 8 == 0
    n_cmp = (past + dec_seq) // STRIDE - 1
    n_blocks = -(-(past + dec_seq) // L_SLC)
    nb_pad = -(-n_blocks // 128) * 128
    pps = _tile(n_pages, 4, 1)
    n_steps = n_pages // pps
    spg = _tile(n_steps, max(1, 128 // (cpp * pps)), 1)
    n_kh = 2 * n_kv
    ovt = jnp.asarray(_overlap_matrix(n_cmp, n_blocks, n_chunks, nb_pad).T.copy(), BF16)
    kern = functools.partial(_cmp_s_kernel, layer=layer, n_kv=n_kv, n_steps=n_steps, pages_per_step=pps,
                             steps_per_group=spg, n_cmp=n_cmp, n_blocks=n_blocks, dec_seq=dec_seq,
                             gqa=gqa, past=past)
    grid_spec = pltpu.PrefetchScalarGridSpec(
        num_scalar_prefetch=1,
        grid=(batch, n_steps),
        in_specs=[
            pl.BlockSpec(memory_space=pl.ANY),
            pl.BlockSpec((None, n_kv, 128, HEAD_DIM), lambda b, p, pt: (b, 0, 0, 0)),
            pl.BlockSpec(w1cat.shape, lambda b, p, pt: (0, 0, 0)),
            pl.BlockSpec(w2.shape, lambda b, p, pt: (0, 0, 0)),
            pl.BlockSpec(pe.shape, lambda b, p, pt: (0, 0, 0)),
            pl.BlockSpec(ovt.shape, lambda b, p, pt: (0, 0))],
        out_specs=(pl.BlockSpec((None, n_kv, 128, HEAD_DIM), lambda b, p, pt: (b, 0, 0, 0)),
                   pl.BlockSpec((None, n_kv, 8, nb_pad), lambda b, p, pt: (b, 0, 0, 0))),
        scratch_shapes=[pltpu.VMEM((n_kh, spg * pps * cpp, STRIDE * HEAD_DIM), F32),
                        pltpu.VMEM((n_kh, n_chunks, 2 * HEAD_DIM), F32),
                        pltpu.VMEM((3, pps, n_kh, page, HEAD_DIM), F32),
                        pltpu.SemaphoreType.DMA((3,))],
    )
    return pl.pallas_call(
        kern,
        out_shape=(jax.ShapeDtypeStruct((batch, n_kv, 128, HEAD_DIM), F32),
                   jax.ShapeDtypeStruct((batch, n_kv, 8, nb_pad), F32)),
        grid_spec=grid_spec,
        compiler_params=_cparams("arbitrary", "arbitrary"),
        name=name,
    )(page_table, cache_rows, q_pad, w1cat, w2, pe, ovt)


def _slc_s_kernel(pt_ref, need_ref, cache_ref, q_ref, sel_ref, ex_ref, kn_ref, vn_ref, win_ref, kwn_ref, vwn_ref,
                  oc_ref, gt_ref, o_ref, qbd_scr, m_scr, l_scr, acc_scr, buf_ref, sem_ref,
                  *, layer, n_kv, n_steps, pages_per_step, rows_per_head, dec_seq, past, w_buf):
    p = pl.program_id(1)
    slot = _page_stream(cache_ref, buf_ref, sem_ref, pt_ref, layer=layer, row0=2 * n_kv,
                        n_rows=2 * n_kv, pages_per_step=pages_per_step, need_ref=need_ref, n_heads=n_kv)
    rows = n_kv * rows_per_head

    @pl.when(p == 0)
    def _():
        qbd_scr[...] = jnp.zeros(qbd_scr.shape, qbd_scr.dtype)
        for h in range(n_kv):
            qbd_scr[h * rows_per_head:(h + 1) * rows_per_head, h * HEAD_DIM:(h + 1) * HEAD_DIM] = (
                q_ref[h, 0:rows_per_head, :] * LOGIT_SCALE)
        m_scr[...] = jnp.full(m_scr.shape, NEG_BIG, F32)
        l_scr[...] = jnp.zeros(l_scr.shape, F32)
        acc_scr[...] = jnp.zeros(acc_scr.shape, F32)

    qbd = qbd_scr[...].astype(BF16)

    def online_update(s, mask, v):
        sm = jnp.where(mask, s, NEG_BIG)
        m_old = m_scr[...]
        m_new = jnp.maximum(m_old, jnp.max(sm, axis=-1, keepdims=True))
        alpha = jnp.exp2(m_old - m_new)
        e = jnp.where(mask, jnp.exp2(sm - m_new), 0.0)
        l_scr[...] = alpha * l_scr[...] + jnp.sum(e, axis=-1, keepdims=True)
        acc_scr[...] = alpha * acc_scr[...] + _dot(e.astype(BF16), v)
        m_scr[...] = m_new

    ks, vs = [], []
    for k in range(pages_per_step):
        ks.append(jnp.concatenate([buf_ref[slot, k, h] for h in range(n_kv)], axis=1))
        vs.append(jnp.concatenate([buf_ref[slot, k, n_kv + h] for h in range(n_kv)], axis=1))
    k_all = jnp.concatenate(ks, axis=0).astype(BF16)
    v_all = jnp.concatenate(vs, axis=0).astype(BF16)
    s = _dot_nt(qbd, k_all)
    sel_keys = _dot(sel_ref[...].astype(BF16), ex_ref[...])
    online_update(s, sel_keys > 0.5, v_all)

    @pl.when(p == n_steps - 1)
    def _():
        r = lax.broadcasted_iota(jnp.int32, (rows, 1), 0)
        t = lax.rem(r, dec_seq)
        s = _dot_nt(qbd, kn_ref[...].astype(BF16))
        j = lax.broadcasted_iota(jnp.int32, s.shape, 1)
        online_update(s, (j <= t) & (j < dec_seq), vn_ref[...].astype(BF16))
        o_s = acc_scr[...] / jnp.maximum(l_scr[...], 1e-30)

        sb = _dot_nt(qbd, _head_slabs(win_ref, 0, n_kv).astype(BF16))
        sn = _dot_nt(qbd, kwn_ref[...].astype(BF16))
        ib = lax.broadcasted_iota(jnp.int32, sb.shape, 1)
        kpos = past - w_buf + ib
        d = (past + t) - kpos
        mb = (d >= 0) & (d < WINDOW) & (kpos >= 0)
        jn = lax.broadcasted_iota(jnp.int32, sn.shape, 1)
        mn = (jn <= t) & (jn < dec_seq) & (t - jn < WINDOW)
        smb = jnp.where(mb, sb, NEG_BIG)
        smn = jnp.where(mn, sn, NEG_BIG)
        mx = jnp.maximum(jnp.max(smb, axis=-1, keepdims=True), jnp.max(smn, axis=-1, keepdims=True))
        eb = jnp.where(mb, jnp.exp2(smb - mx), 0.0)
        en = jnp.where(mn, jnp.exp2(smn - mx), 0.0)
        den = jnp.maximum(jnp.sum(eb, axis=-1, keepdims=True) + jnp.sum(en, axis=-1, keepdims=True),
                          1e-30)
        o_w = (_dot((eb / den).astype(BF16), _head_slabs(win_ref, n_kv, n_kv).astype(BF16))
               + _dot((en / den).astype(BF16), vwn_ref[...].astype(BF16)))

        for h in range(n_kv):
            r0 = h * rows_per_head
            c0 = h * HEAD_DIM
            gt = gt_ref[h]
            o_ref[h] = (gt[:, 0:1] * oc_ref[h, 0:rows_per_head, :]
                        + gt[:, 1:2] * o_s[r0:r0 + rows_per_head, c0:c0 + HEAD_DIM]
                        + gt[:, 2:3] * o_w[r0:r0 + rows_per_head, c0:c0 + HEAD_DIM])


def _slc_sample(page_table, need, cache_rows, q_pad, sel_steps, k_new, v_new, win_rows, kw_new, vw_new, o_c,
                gates, *, layer, n_kv, gqa, dec_seq, pages_per_step, name):
    batch, n_pages = page_table.shape
    page = cache_rows.shape[2]
    past = n_pages * page
    width = n_kv * HEAD_DIM
    n_kh = 2 * n_kv
    rph = gqa * dec_seq
    rows = n_kv * rph
    w_buf = win_rows.shape[2]
    n_new = k_new.shape[1]
    pps = pages_per_step
    n_steps = n_pages // pps
    ex = jnp.asarray(_block_to_key_matrix(1, sel_steps.shape[3], pps * page)[0], BF16)
    kern = functools.partial(_slc_s_kernel, layer=layer, n_kv=n_kv, n_steps=n_steps, pages_per_step=pps,
                             rows_per_head=rph, dec_seq=dec_seq, past=past, w_buf=w_buf)

    def new_spec():
        return pl.BlockSpec((None, n_new, width), lambda b, p, pt, nd: (b, 0, 0))

    grid_spec = pltpu.PrefetchScalarGridSpec(
        num_scalar_prefetch=2,
        grid=(batch, n_steps),
        in_specs=[
            pl.BlockSpec(memory_space=pl.ANY),
            pl.BlockSpec((None, n_kv, 128, HEAD_DIM), lambda b, p, pt, nd: (b, 0, 0, 0)),
            pl.BlockSpec((None, None, rows, sel_steps.shape[3]), lambda b, p, pt, nd: (b, p, 0, 0)),
            pl.BlockSpec(ex.shape, lambda b, p, pt, nd: (0, 0)),
            new_spec(), new_spec(),
            pl.BlockSpec((None, None, w_buf, n_kh, HEAD_DIM), lambda b, p, pt, nd: (layer, b, 0, 0, 0)),
            new_spec(), new_spec(),
            pl.BlockSpec((None, n_kv, 128, HEAD_DIM), lambda b, p, pt, nd: (b, 0, 0, 0)),
            pl.BlockSpec((None, n_kv, rph, 8), lambda b, p, pt, nd: (b, 0, 0, 0))],
        out_specs=pl.BlockSpec((None, n_kv, rph, HEAD_DIM), lambda b, p, pt, nd: (b, 0, 0, 0)),
        scratch_shapes=[pltpu.VMEM((rows, width), F32),
                        pltpu.VMEM((rows, 1), F32),
                        pltpu.VMEM((rows, 1), F32),
                        pltpu.VMEM((rows, width), F32),
                        pltpu.VMEM((3, pps, n_kh, page, HEAD_DIM), F32),
                        pltpu.SemaphoreType.DMA((3,))],
    )
    return pl.pallas_call(
        kern,
        out_shape=jax.ShapeDtypeStruct((batch, n_kv, rph, HEAD_DIM), F32),
        grid_spec=grid_spec,
        compiler_params=_cparams("arbitrary", "arbitrary"),
        name=name,
    )(page_table, need, cache_rows, q_pad, sel_steps, ex, k_new, v_new, win_rows, kw_new, vw_new, o_c,
      gates)


def _rope_tables(pos):
    inv = ROPE_THETA ** (-jnp.arange(ROT_HALF, dtype=F32) * 2.0 / ROT_DIM)
    ang = pos.astype(F32)[:, None] * inv[None, :]
    cos, sin = jnp.cos(ang), jnp.sin(ang)
    rest = HEAD_DIM - ROT_DIM
    cos_t = jnp.concatenate([cos, cos, jnp.ones((pos.shape[0], rest), F32)], axis=1)
    sin_t = jnp.concatenate([-sin, sin, jnp.zeros((pos.shape[0], rest), F32)], axis=1)
    return cos_t, sin_t


def _prep_weights(w_in, ffn_conv_b, n_gate):
    w_gate = jnp.pad(w_in[:, :, w_in.shape[2] - n_gate:], ((0, 0), (0, 0), (0, 128 - n_gate)))
    return dict(w_gate=w_gate.astype(BF16),
                f_cb=ffn_conv_b.reshape(ffn_conv_b.shape[0], 1, ffn_conv_b.shape[1]))


def _prep_compress(l, cmp_pe, cmp_w1, cmp_w2):
    half = STRIDE * HEAD_DIM
    w1 = cmp_w1[l].reshape(2, 2 * half, HEAD_DIM)
    wl = {}
    wl["cmp_w1"] = w1.astype(BF16)
    wl["cmp_w1cat"] = jnp.concatenate([w1[:, :half], w1[:, half:]], axis=2).astype(BF16)
    wl["cmp_w2"] = cmp_w2[l].astype(BF16)
    pe = cmp_pe[l].reshape(2, 1, 2 * half)
    wl["cmp_pe"] = jnp.broadcast_to(pe, (2, 16, 2 * half)).astype(BF16)
    pe2 = cmp_pe[l].reshape(2, 2, 1, half)
    wl["cmp_pecat"] = jnp.concatenate([jnp.broadcast_to(pe2[:, 0], (2, 8, half)),
                                       jnp.broadcast_to(pe2[:, 1], (2, 8, half))], axis=1).astype(BF16)
    return wl


def _project(x_bf, ws, wb, l, cos_t, sin_t, dims, tag):
    d_a, d_b, d_kv, n_heads, d_ff = dims
    n_kv = d_kv // HEAD_DIM
    emit = wb is None
    if emit:
        src = dict(uv=(ws["w_in_t"], l, 0), q=(ws["w_in_t"], l, 2 * d_a), kv=(ws["w_in_t"], l, 2 * d_a + d_b))
    else:
        src = dict(uv=(wb["uv"], 0, 0), q=(wb["q"], 0, 0), kv=(wb["kv"], 0, 0))
    new = {}
    w, lay, c0 = src["uv"]
    out = _matmul([x_bf], w, layer=lay, col_start=c0, n=2 * d_a, epilogue="gelu", tm=1024,
                  tn=512 if emit else 1024, emit=emit, w_transposed=emit, name=f"proj_uv_{tag}")
    uv = out[0] if emit else out
    if emit:
        new["uv"] = out[1]
    w, lay, c0 = src["q"]
    out = _rope_matmul(x_bf, w, cos_t, sin_t, layer=lay, col_start=c0, n=d_b,
                       tn=d_kv if emit else _tile(d_b, 2 * d_kv), alternate=False, emit=emit,
                       name=f"proj_q_{tag}")
    q2d = out[0]
    if emit:
        new["q"] = out[-1]
    w, lay, c0 = src["kv"]
    out = _rope_matmul(x_bf, w, cos_t, sin_t, layer=lay, col_start=c0, n=6 * d_kv, tn=d_kv, alternate=True,
                       n_kv=n_kv, emit=emit, name=f"proj_kv_{tag}")
    kv, nsa4, win4 = out[0], out[1], out[2]
    if emit:
        new["kv"] = out[-1]
    gates = _matmul([x_bf], ws["w_gate"], layer=l, epilogue="sigmoid", tm=1024, tn=128,
                    name=f"proj_gate_{tag}")
    return uv, q2d, kv, nsa4, win4, gates, new


def _mix_and_norm(x, a_out, b_out, ws, wb, l, ln_g, ln_b, alpha, tag):
    new = {}
    if wb is None:
        z, new["o_a"], new["o_b"] = _matmul([a_out, b_out], ws["w_o"], layer=l, tm=1024, tn=512,
                                            emit=True, residual=x, res_scale=alpha, name=f"w_o_{tag}")
    else:
        z = _matmul([a_out, b_out], [wb["o_a"], wb["o_b"]], layer=0, tm=1024, tn=1024,
                    residual=x, res_scale=alpha, name=f"w_o_{tag}")
    y, yb = _layer_norm(z, ln_g[l, 0], ln_b[l, 0], name=f"ln1_{tag}")
    return y, yb, new


def _layer_prompt(x, x_bf, ws, wb, wl, l, p, cos_t, sin_t, dims, batch, seq, n_kv, alpha):
    d_a, d_b, d_kv, n_heads, d_ff = dims
    gqa = n_heads // n_kv
    m = batch * seq
    uv, q2d, kv, nsa4, win4, gates, _ = _project(x_bf, ws, wb, l, cos_t, sin_t, dims, "p")
    (a_out,) = _sgu(uv, p["sgu_w"][l], p["sgu_b"][l].T, p["sgu_g"][l], rows=CHUNK, with_vn=False,
                    name="sgu_p")
    kcmp, vcmp = _compress_prompt(kv, wl["cmp_w1"], wl["cmp_w2"], wl["cmp_pe"], batch=batch, seq=seq,
                                  n_kv=n_kv, name="compress_p")
    gates_h = gates[:, :3 * n_heads].reshape(m, 3, n_kv, gqa).transpose(2, 0, 1, 3).reshape(n_kv, m, 3 * gqa)
    b_out = _attn_prompt(q2d, kv, kcmp, vcmp, gates_h, batch=batch, seq=seq, n_heads=n_heads, n_kv=n_kv,
                         name="nsa_p")
    x1, x1b, _ = _mix_and_norm(x, a_out, b_out, ws, wb, l, p["ln_g"], p["ln_b"], alpha, "p")
    state0 = jnp.zeros((batch, CONV_W - 1, d_ff), F32)
    act, conv_new = _ffn_in_seq(x1b, wb["f_gate"], wb["f_up"], p["ffn_conv_w"], ws["f_cb"], state0,
                                layer=l, seq_len=seq, name="ffn_in_p")
    z = _matmul([act], wb["f_down"], layer=0, tm=512, tn=512, residual=x1, res_scale=alpha,
                name="ffn_down_p")
    x2, x2b = _layer_norm(z, p["ln_g"][l, 1], p["ln_b"][l, 1], name="ln2_p")
    new_nsa = nsa4.reshape(batch, seq, 4, n_kv, HEAD_DIM)
    new_win = win4.reshape(batch, seq, 2, n_kv, HEAD_DIM)[:, -min(WINDOW, seq):]
    return x2, x2b, new_nsa, new_win, conv_new


def _layer_sample(x, x_bf, ws, wl, l, p, cos_t, sin_t, dims, batch, dec_seq, n_kv, alpha, page_table,
                  cache_rows, win_rows, conv_state):
    d_a, d_b, d_kv, n_heads, d_ff = dims
    gqa = n_heads // n_kv
    m = batch * dec_seq
    rph = gqa * dec_seq
    uv, q2d, kv, nsa4, win4, gates, wb = _project(x_bf, ws, None, l, cos_t, sin_t, dims, "s")
    w_small = p["sgu_w"][l][:, :dec_seq, :dec_seq]
    eye = jnp.eye(batch, dtype=F32)
    w_bd = jnp.einsum("ab,gts->gatbs", eye, w_small).reshape(-1, m, m)
    bt_bd = jnp.tile(p["sgu_b"][l].T[:dec_seq], (batch, 1))
    a_out, v_rows = _sgu(uv, w_bd, bt_bd, p["sgu_g"][l], rows=m, with_vn=True, name="sgu_s")

    q = q2d.reshape(batch, dec_seq, n_kv, gqa, HEAD_DIM).transpose(0, 2, 3, 1, 4)
    q_pad = jnp.pad(q.reshape(batch, n_kv, rph, HEAD_DIM), ((0, 0), (0, 0), (0, 128 - rph), (0, 0)))
    o_c, sel = _cmp_sample(page_table, cache_rows, q_pad, wl["cmp_w1cat"], wl["cmp_w2"], wl["cmp_pecat"],
                           layer=l, n_kv=n_kv, gqa=gqa, dec_seq=dec_seq, name="cmp_s")
    n_pages = page_table.shape[1]
    bpp = cache_rows.shape[2] // L_SLC
    pps = _tile(n_pages, 4, 1)
    n_steps = n_pages // pps
    sel_steps = sel[:, :, :dec_seq, :n_pages * bpp].reshape(batch, n_kv, 1, dec_seq, n_steps, pps * bpp)
    sel_steps = jnp.broadcast_to(sel_steps, (batch, n_kv, gqa, dec_seq, n_steps, pps * bpp))
    sel_steps = sel_steps.transpose(0, 4, 1, 2, 3, 5).reshape(batch, n_steps, n_kv * rph, pps * bpp)
    sel_steps = jnp.pad(sel_steps, ((0, 0), (0, 0), (0, 0), (0, 128 - pps * bpp)))

    def new_rows(kind):
        rows = kv[:, kind * d_kv:(kind + 1) * d_kv].reshape(batch, dec_seq, d_kv)
        return jnp.pad(rows, ((0, 0), (0, 128 - dec_seq), (0, 0)))

    gates_s = gates[:, :3 * n_heads].reshape(batch, dec_seq, 3, n_kv, gqa).transpose(0, 3, 4, 1, 2)
    gates_s = jnp.pad(gates_s.reshape(batch, n_kv, rph, 3), ((0, 0), (0, 0), (0, 0), (0, 5)))
    need = sel[:, :, :dec_seq, :n_pages * bpp].reshape(batch, n_kv, dec_seq, n_pages, bpp).max(axis=(2, 4))
    need = (need.transpose(0, 2, 1) > 0.5).astype(jnp.int32).reshape(batch, n_pages * n_kv)
    b_rows = _slc_sample(page_table, need, cache_rows, q_pad, sel_steps, new_rows(2), new_rows(3), win_rows,
                         new_rows(4), new_rows(5), o_c, gates_s, layer=l, n_kv=n_kv, gqa=gqa,
                         dec_seq=dec_seq, pages_per_step=pps, name="slc_s")
    b_out = b_rows.reshape(batch, n_kv, gqa, dec_seq, HEAD_DIM).transpose(0, 3, 1, 2, 4)
    b_out = b_out.reshape(m, d_b).astype(BF16)

    x1, x1b, wb_o = _mix_and_norm(x, a_out, b_out, ws, None, l, p["ln_g"], p["ln_b"], alpha, "s")
    wb.update(wb_o)
    st = conv_state
    zero = jnp.zeros((batch, dec_seq - 1, d_ff), F32)
    h1 = jnp.concatenate([st[:, 1:2], zero], axis=1).reshape(m, d_ff)
    h2 = jnp.concatenate([st, zero[:, 1:]], axis=1).reshape(m, d_ff)
    act, gate, wb["f_gate"], wb["f_up"] = _ffn_in_short(
        x1b, ws["f_in"], p["ffn_conv_w"], ws["f_cb"], h1, h2, layer=l, seq_len=dec_seq, name="ffn_in_s")
    z, wb["f_down"] = _matmul([act], ws["f_down"], layer=l, tn=256, emit=True, residual=x1,
                              res_scale=alpha, name="ffn_down_s")
    x2, x2b = _layer_norm(z, p["ln_g"][l, 1], p["ln_b"][l, 1], name="ln2_s")
    new_nsa = nsa4.reshape(batch, dec_seq, 4, n_kv, HEAD_DIM)
    new_win = win4.reshape(batch, dec_seq, 2, n_kv, HEAD_DIM)
    conv_new = gate.reshape(batch, dec_seq, d_ff)[:, dec_seq - (CONV_W - 1):]
    return x2, x2b, new_nsa, new_win, v_rows.reshape(batch, dec_seq, d_a), conv_new, wb


def kernel(x_prompt, x_sample, cache_nsa_kv, cache_win_kv, state_ffn_conv, page_table, w_in, sgu_w,
           sgu_b, sgu_g, cmp_pe, cmp_w1, cmp_w2, w_o, ln_g, ln_b, ffn_w_in, ffn_conv_w, ffn_conv_b,
           ffn_w_down):
    bp, seq, d_model = x_prompt.shape
    bs, dec_seq, _ = x_sample.shape
    depth = w_in.shape[0]
    n_kv = cache_nsa_kv.shape[4]
    page = cache_nsa_kv.shape[2]
    past = page_table.shape[1] * page
    d_a = d_model // 2
    d_b = d_model - d_a
    n_heads = d_b // HEAD_DIM
    d_kv = n_kv * HEAD_DIM
    d_ff = ffn_conv_w.shape[-1]
    dims = (d_a, d_b, d_kv, n_heads, d_ff)
    alpha = (2 * depth) ** 0.25
    assert dec_seq >= CONV_W - 1 and seq % CHUNK == 0

    cos_p, sin_p = _rope_tables(jnp.tile(jnp.arange(seq, dtype=jnp.int32), bp))
    cos_s, sin_s = _rope_tables(jnp.tile(past + jnp.arange(dec_seq, dtype=jnp.int32), bs))
    cache_rows = cache_nsa_kv.reshape(depth, cache_nsa_kv.shape[1], page, 4 * n_kv, HEAD_DIM)
    win_rows = cache_win_kv.reshape(depth, bs, cache_win_kv.shape[2], 2 * n_kv, HEAD_DIM)
    ws = _prep_weights(w_in, ffn_conv_b, 3 * n_heads)
    ws.update(w_in_t=jnp.swapaxes(w_in, 1, 2), w_o=w_o, f_in=ffn_w_in, f_down=ffn_w_down)
    p = dict(sgu_w=sgu_w, sgu_b=sgu_b, sgu_g=sgu_g, ln_g=ln_g, ln_b=ln_b, ffn_conv_w=ffn_conv_w)

    xp = x_prompt.reshape(bp * seq, d_model)
    xs = x_sample.reshape(bs * dec_seq, d_model)
    xp_bf, xs_bf = xp.astype(BF16), xs.astype(BF16)
    outs = [[] for _ in range(7)]
    for l in range(depth):
        wl = _prep_compress(l, cmp_pe, cmp_w1, cmp_w2)
        xs, xs_bf, nsa_s, win_s, v_s, conv_s, wb = _layer_sample(
            xs, xs_bf, ws, wl, l, p, cos_s, sin_s, dims, bs, dec_seq, n_kv, alpha, page_table,
            cache_rows, win_rows, state_ffn_conv[l])
        xp, xp_bf, nsa_p, win_p, conv_p = _layer_prompt(
            xp, xp_bf, ws, wb, wl, l, p, cos_p, sin_p, dims, bp, seq, n_kv, alpha)
        for acc, val in zip(outs, (nsa_p, nsa_s, win_p, win_s, v_s, conv_p, conv_s)):
            acc.append(val)
    return (xp.reshape(bp, seq, d_model), xs.reshape(bs, dec_seq, d_model),
            *[jnp.stack(o) for o in outs])
```
